```python
import jax, jax.numpy as jnp
from jax import lax
import numpy as np

D_MODEL = 1024
BATCH = 4
SEQ = 8192
DEPTH = 1
DEC_BATCH = 8
DEC_SEQ = 16
PAST_LEN = 2048

CHUNK = 64
N_HEADS = 8
HEAD_DIM = 64
ATT_W = N_HEADS * HEAD_DIM
POOL_WINDOWS = (2, 4, 8, 16)
POOL_GROUPS = len(POOL_WINDOWS)
POOL_GC = D_MODEL // 8
POOL_W = POOL_GROUPS * POOL_GC
POOL_HIST = max(POOL_WINDOWS) - 1
Q_BLOCK = 128
N_EXPERTS = 32
TOP_K = 4
D_FF = D_MODEL
SWIGLU_ALPHA = 1.702
SWIGLU_LIMIT = 7.0
MOE_BLOCK = 256
LN_EPS = 1e-5
FORGET_BIAS_INIT = 3.0
DEEPNORM_ALPHA = (2.0 * DEPTH) ** 0.25
DEEPNORM_BETA = (8.0 * DEPTH) ** -0.25
ATT_SCALE = HEAD_DIM ** -0.5
POOL_OFF = 0
Q_OFF = POOL_OFF + POOL_W
K_OFF = Q_OFF + ATT_W
V_OFF = K_OFF + ATT_W
F_OFF = V_OFF + ATT_W
G_OFF = F_OFF + N_HEADS
PROJ_W = G_OFF + 2 * D_MODEL

kernel_name = 'hybrid_pool_fox_moe_stream_step'


def layer_norm(x, g, b):
    xf = x.astype(jnp.float32)
    mu = jnp.mean(xf, axis=-1, keepdims=True)
    xc = xf - mu
    var = jnp.mean(xc * xc, axis=-1, keepdims=True)
    y = xc * lax.rsqrt(var + LN_EPS) * g.astype(jnp.float32) + b.astype(jnp.float32)
    return y.astype(x.dtype)


def split_projection(xn, w_in, b_in):
    B, T, _ = xn.shape
    p = xn @ w_in + b_in
    u = p[..., POOL_OFF:Q_OFF]
    q = p[..., Q_OFF:K_OFF].reshape(B, T, N_HEADS, HEAD_DIM)
    k = p[..., K_OFF:V_OFF].reshape(B, T, N_HEADS, HEAD_DIM)
    v = p[..., V_OFF:F_OFF].reshape(B, T, N_HEADS, HEAD_DIM)
    f_logit = p[..., F_OFF:G_OFF]
    g = p[..., G_OFF:]
    return u, q, k, v, f_logit, g


def pool_mix(u_full, start_pos, w_pool, s_pool):
    B, L, _ = u_full.shape
    T = L - POOL_HIST
    uf = u_full.astype(jnp.float32)
    c0 = jnp.concatenate([jnp.zeros((B, 1, POOL_W), jnp.float32), jnp.cumsum(uf, axis=1)], axis=1)
    pos = start_pos + jnp.arange(T, dtype=jnp.int32)
    end = c0[:, POOL_HIST + 1:]
    cur = uf[:, POOL_HIST:]
    groups = []
    for gi, w in enumerate(POOL_WINDOWS):
        sl = slice(gi * POOL_GC, (gi + 1) * POOL_GC)
        begin = c0[:, POOL_HIST + 1 - w:POOL_HIST + 1 - w + T, sl]
        count = jnp.minimum(pos + 1, w).astype(jnp.float32)[None, :, None]
        groups.append((end[..., sl] - begin) / count - cur[..., sl])
    d = jnp.stack(groups, axis=2).astype(u_full.dtype)
    y = jnp.einsum('btgc,gcd->btgd', d, w_pool).reshape(B, T, POOL_W)
    return y * s_pool


def fox_attend(q, Fq, q_pos, k, v, Fk_bhk, k_pos):
    s = jnp.einsum('bqhd,bkhd->bhqk', q, k, preferred_element_type=jnp.float32) * ATT_SCALE
    s = s + jnp.swapaxes(Fq, 1, 2)[..., None] - Fk_bhk[:, :, None, :]
    s = jnp.where(q_pos[:, None] >= k_pos[None, :], s, -jnp.inf)
    p = jax.nn.softmax(s, axis=-1)
    return jnp.einsum('bhqk,bkhd->bqhd', p.astype(v.dtype), v)


def fox_prompt(q, k, v, logf):
    B, T = q.shape[:2]
    F = jnp.cumsum(logf, axis=1)
    Fk = jnp.swapaxes(F, 1, 2)
    k_pos = jnp.arange(T, dtype=jnp.int32)
    nb = T // Q_BLOCK
    qb = q.reshape(B, nb, Q_BLOCK, N_HEADS, HEAD_DIM).transpose(1, 0, 2, 3, 4)
    Fqb = F.reshape(B, nb, Q_BLOCK, N_HEADS).transpose(1, 0, 2, 3)
    qpos = k_pos.reshape(nb, Q_BLOCK)

    def block(args):
        qi, Fqi, pi = args
        return fox_attend(qi, Fqi, pi, k, v, Fk, k_pos)

    out = lax.map(block, (qb, Fqb, qpos))
    return out.transpose(1, 0, 2, 3, 4).reshape(B, T, N_HEADS, HEAD_DIM)


def merge(y_pool, y_att, g, w_pool_proj, w_att_proj, w_out):
    B, T = y_att.shape[:2]
    ga, gb = jnp.split(jax.nn.sigmoid(g), 2, axis=-1)
    m = ga * (y_pool @ w_pool_proj) + gb * (y_att.reshape(B, T, ATT_W) @ w_att_proj)
    return m @ w_out


def mixer_prompt(xn, w_in, b_in, w_pool, s_pool, w_pool_proj, w_att_proj, w_out):
    B = xn.shape[0]
    u, q, k, v, f_logit, g = split_projection(xn, w_in, b_in)
    hist = jnp.zeros((B, POOL_HIST, POOL_W), u.dtype)
    y_pool = pool_mix(jnp.concatenate([hist, u], axis=1), 0, w_pool, s_pool)
    logf = jax.nn.log_sigmoid(f_logit.astype(jnp.float32))
    y_att = fox_prompt(q, k, v, logf)
    y = merge(y_pool, y_att, g, w_pool_proj, w_att_proj, w_out)
    return y, k, v, logf, u[:, -POOL_HIST:]


def mixer_sample(xn, cache_pool, cache_k, cache_v, cache_logf,
                 w_in, b_in, w_pool, s_pool, w_pool_proj, w_att_proj, w_out):
    T = xn.shape[1]
    P = cache_k.shape[1]
    u, q, k, v, f_logit, g = split_projection(xn, w_in, b_in)
    u_full = jnp.concatenate([cache_pool.astype(u.dtype), u], axis=1)
    y_pool = pool_mix(u_full, P, w_pool, s_pool)
    logf = jax.nn.log_sigmoid(f_logit.astype(jnp.float32))
    k_all = jnp.concatenate([cache_k.astype(k.dtype), k], axis=1)
    v_all = jnp.concatenate([cache_v.astype(v.dtype), v], axis=1)
    F = jnp.cumsum(jnp.concatenate([cache_logf.astype(jnp.float32), logf], axis=1), axis=1)
    k_pos = jnp.arange(P + T, dtype=jnp.int32)
    q_pos = P + jnp.arange(T, dtype=jnp.int32)
    y_att = fox_attend(q, F[:, P:], q_pos, k_all, v_all, jnp.swapaxes(F, 1, 2), k_pos)
    y = merge(y_pool, y_att, g, w_pool_proj, w_att_proj, w_out)
    return y, k, v, logf, u_full[:, -POOL_HIST:]


def moe(x, w_router, b_router, w1, b1, w2, b2):
    B_, T_, D = x.shape
    xt = x.reshape(B_ * T_, D)
    T = B_ * T_
    logits = jnp.dot(xt, w_router, preferred_element_type=jnp.float32) + b_router.astype(jnp.float32)
    top_v, top_i = lax.top_k(logits, TOP_K)
    wts = jax.nn.softmax(top_v, axis=-1)
    M = T * TOP_K
    flat_e = top_i.reshape(M).astype(jnp.int32)
    order = jnp.argsort(flat_e, stable=True)
    sorted_e = flat_e[order]
    sorted_tok = (order // TOP_K).astype(jnp.int32)
    sorted_w = wts.reshape(M)[order]
    counts = jnp.zeros((N_EXPERTS,), jnp.int32).at[flat_e].add(1)
    padded = (counts + MOE_BLOCK - 1) // MOE_BLOCK * MOE_BLOCK
    starts = jnp.cumsum(counts) - counts
    pad_ends = jnp.cumsum(padded)
    pad_starts = pad_ends - padded
    dest = pad_starts[sorted_e] + (jnp.arange(M, dtype=jnp.int32) - starts[sorted_e])
    n_blocks = (M + N_EXPERTS * (MOE_BLOCK - 1) + MOE_BLOCK - 1) // MOE_BLOCK
    P = n_blocks * MOE_BLOCK
    buf_tok = jnp.full((P,), T, jnp.int32).at[dest].set(sorted_tok)
    buf_w = jnp.zeros((P,), jnp.float32).at[dest].set(sorted_w)
    block_e = jnp.minimum(jnp.searchsorted(pad_ends, jnp.arange(n_blocks, dtype=jnp.int32) * MOE_BLOCK, side='right'), N_EXPERTS - 1)
    x_pad = jnp.concatenate([xt, jnp.zeros((1, D), xt.dtype)], axis=0)
    xb = x_pad[buf_tok].reshape(n_blocks, MOE_BLOCK, D)

    def expert_block(args):
        xblk, e = args
        h = xblk @ w1[e] + b1[e]
        glu = jnp.minimum(h[..., :D_FF], SWIGLU_LIMIT)
        lin = jnp.clip(h[..., D_FF:], -SWIGLU_LIMIT, SWIGLU_LIMIT)
        a = glu * jax.nn.sigmoid(SWIGLU_ALPHA * glu) * (lin + 1.0)
        return a @ w2[e] + b2[e]

    yb = lax.map(expert_block, (xb, block_e)).reshape(P, D)
    y = jnp.zeros((T + 1, D), x.dtype).at[buf_tok].add(yb * buf_w[:, None].astype(x.dtype))
    return y[:T].reshape(B_, T_, D)


def setup_inputs(seed: int = 0) -> dict:
    key = jax.random.key(seed)
    ks = jax.random.split(key, 26)

    def nrm(k, shape, scale):
        return scale * jax.random.normal(k, shape, jnp.float32)

    x_prompt = nrm(ks[0], (BATCH, SEQ, D_MODEL), 1.0)
    x_sample = nrm(ks[1], (DEC_BATCH, DEC_SEQ, D_MODEL), 1.0)
    cache_pool = nrm(ks[2], (DEPTH, DEC_BATCH, POOL_HIST, POOL_W), 1.0)
    cache_k = nrm(ks[3], (DEPTH, DEC_BATCH, PAST_LEN, N_HEADS, HEAD_DIM), 1.0)
    cache_v = nrm(ks[4], (DEPTH, DEC_BATCH, PAST_LEN, N_HEADS, HEAD_DIM), DEEPNORM_BETA)
    cache_logf = jax.nn.log_sigmoid(FORGET_BIAS_INIT + nrm(ks[5], (DEPTH, DEC_BATCH, PAST_LEN, N_HEADS), 1.0))
    ln_in_g = 1.0 + nrm(ks[6], (D_MODEL,), 0.02)
    ln_in_b = nrm(ks[7], (D_MODEL,), 0.02)
    w_in = nrm(ks[8], (DEPTH, D_MODEL, PROJ_W), D_MODEL ** -0.5)
    w_in = w_in.at[:, :, V_OFF:F_OFF].multiply(DEEPNORM_BETA)
    b_in = nrm(ks[9], (DEPTH, PROJ_W), 0.02).at[:, F_OFF:G_OFF].add(FORGET_BIAS_INIT)
    w_pool = nrm(ks[10], (DEPTH, POOL_GROUPS, POOL_GC, POOL_GC), POOL_GC ** -0.5)
    s_pool = 1.0 + nrm(ks[11], (DEPTH, POOL_W), 0.02)
    w_pool_proj = nrm(ks[12], (DEPTH, POOL_W, D_MODEL), POOL_W ** -0.5)
    w_att_proj = nrm(ks[13], (DEPTH, ATT_W, D_MODEL), ATT_W ** -0.5)
    w_out = nrm(ks[14], (DEPTH, D_MODEL, D_MODEL), DEEPNORM_BETA * D_MODEL ** -0.5)
    ln1_g = 1.0 + nrm(ks[15], (DEPTH, D_MODEL), 0.02)
    ln1_b = nrm(ks[16], (DEPTH, D_MODEL), 0.02)
    w_router = nrm(ks[17], (DEPTH, D_MODEL, N_EXPERTS), D_MODEL ** -0.5)
    b_router = nrm(ks[18], (DEPTH, N_EXPERTS), 0.01)
    w1 = nrm(ks[19], (DEPTH, N_EXPERTS, D_MODEL, 2 * D_FF), D_MODEL ** -0.5)
    b1 = nrm(ks[20], (DEPTH, N_EXPERTS, 2 * D_FF), 0.02)
    w2 = nrm(ks[21], (DEPTH, N_EXPERTS, D_FF, D_MODEL), DEEPNORM_BETA * D_FF ** -0.5)
    b2 = nrm(ks[22], (DEPTH, N_EXPERTS, D_MODEL), 0.02)
    ln2_g = 1.0 + nrm(ks[23], (DEPTH, D_MODEL), 0.02)
    ln2_b = nrm(ks[24], (DEPTH, D_MODEL), 0.02)
    return {'x_prompt': x_prompt, 'x_sample': x_sample,
            'cache_pool': cache_pool, 'cache_k': cache_k, 'cache_v': cache_v, 'cache_logf': cache_logf,
            'ln_in_g': ln_in_g, 'ln_in_b': ln_in_b, 'w_in': w_in, 'b_in': b_in,
            'w_pool': w_pool, 's_pool': s_pool, 'w_pool_proj': w_pool_proj, 'w_att_proj': w_att_proj,
            'w_out': w_out, 'ln1_g': ln1_g, 'ln1_b': ln1_b, 'w_router': w_router, 'b_router': b_router,
            'w1': w1, 'b1': b1, 'w2': w2, 'b2': b2, 'ln2_g': ln2_g, 'ln2_b': ln2_b}


def reference(x_prompt, x_sample, cache_pool, cache_k, cache_v, cache_logf,
              ln_in_g, ln_in_b, w_in, b_in, w_pool, s_pool, w_pool_proj, w_att_proj,
              w_out, ln1_g, ln1_b, w_router, b_router, w1, b1, w2, b2, ln2_g, ln2_b):
    xp = layer_norm(x_prompt, ln_in_g, ln_in_b)
    xs = layer_norm(x_sample, ln_in_g, ln_in_b)
    kp_l, vp_l, fp_l, pp_l = [], [], [], []
    ks_l, vs_l, fs_l, ps_l = [], [], [], []
    for l in range(DEPTH):
        a_p, kp, vp, fp, pp = mixer_prompt(xp, w_in[l], b_in[l], w_pool[l], s_pool[l],
                                           w_pool_proj[l], w_att_proj[l], w_out[l])
        a_s, ksm, vsm, fsm, psm = mixer_sample(xs, cache_pool[l], cache_k[l], cache_v[l], cache_logf[l],
                                               w_in[l], b_in[l], w_pool[l], s_pool[l],
                                               w_pool_proj[l], w_att_proj[l], w_out[l])
        xp = layer_norm(DEEPNORM_ALPHA * xp + a_p, ln1_g[l], ln1_b[l])
        xs = layer_norm(DEEPNORM_ALPHA * xs + a_s, ln1_g[l], ln1_b[l])
        xp = layer_norm(DEEPNORM_ALPHA * xp + moe(xp, w_router[l], b_router[l], w1[l], b1[l], w2[l], b2[l]), ln2_g[l], ln2_b[l])
        xs = layer_norm(DEEPNORM_ALPHA * xs + moe(xs, w_router[l], b_router[l], w1[l], b1[l], w2[l], b2[l]), ln2_g[l], ln2_b[l])
        kp_l.append(kp); vp_l.append(vp); fp_l.append(fp); pp_l.append(pp)
        ks_l.append(ksm); vs_l.append(vsm); fs_l.append(fsm); ps_l.append(psm)
    k_prompt = jnp.stack(kp_l, axis=0)
    v_prompt = jnp.stack(vp_l, axis=0)
    logf_prompt = jnp.stack(fp_l, axis=0)
    pool_prompt = jnp.stack(pp_l, axis=0)
    k_sample = jnp.stack(ks_l, axis=0)
    v_sample = jnp.stack(vs_l, axis=0)
    logf_sample = jnp.stack(fs_l, axis=0)
    pool_sample = jnp.stack(ps_l, axis=0)
    return (xp, xs, k_prompt, v_prompt, logf_prompt, pool_prompt, k_sample, v_sample, logf_sample, pool_sample)
```

```python
import functools

import jax
import jax.numpy as jnp
from jax import lax
from jax.experimental import pallas as pl
from jax.experimental.pallas import tpu as pltpu

F32 = jnp.float32
BF16 = jnp.bfloat16
I32 = jnp.int32
U32 = jnp.uint32

D_MODEL = 1024
N_HEADS = 8
HEAD_DIM = 64
ATT_W = N_HEADS * HEAD_DIM
POOL_WINDOWS = (2, 4, 8, 16)
POOL_GC = 128
POOL_W = len(POOL_WINDOWS) * POOL_GC
POOL_HIST = max(POOL_WINDOWS) - 1
HALO = 16
N_EXPERTS = 32
TOP_K = 4
D_FF = D_MODEL
SWIGLU_ALPHA = 1.702
SWIGLU_LIMIT = 7.0
LN_EPS = 1e-5
DEPTH = 1
DEEPNORM_ALPHA = (2.0 * DEPTH) ** 0.25
ATT_SCALE = HEAD_DIM ** -0.5
MAIN_W = POOL_W + 3 * ATT_W
LANES = 128
AUG_W = LANES
HALF = D_MODEL // 2
VMEM_LIMIT = 56 * 1024 * 1024

SEQ_TILE = 512
ATT_TQ = 512
ATT_TK = 512
MOE_BLOCK = 256
ROW_TILE = 128


def _layer_norm(x, g, b):
    mu = jnp.mean(x, axis=-1, keepdims=True)
    xc = x - mu
    var = jnp.mean(xc * xc, axis=-1, keepdims=True)
    return xc * lax.rsqrt(var + LN_EPS) * g + b


def _split3(x):
    a = x.astype(BF16)
    r = x - a.astype(F32)
    b = r.astype(BF16)
    c = (r - b.astype(F32)).astype(BF16)
    return a, b, c


def _prefix_rows(x, tm, inclusive):
    kp = max(tm, LANES)
    r = lax.broadcasted_iota(I32, (tm, kp), 0)
    c = lax.broadcasted_iota(I32, (tm, kp), 1)
    tri = ((c <= r) if inclusive else (c < r)).astype(BF16)
    if kp > tm:
        x = jnp.concatenate([x, jnp.zeros((kp - tm, x.shape[1]), x.dtype)], axis=0)
    out = None
    for piece in _split3(x):
        y = jnp.dot(tri, piece, preferred_element_type=F32)
        out = y if out is None else out + y
    return out


def _pack_halves(y):
    hi = pltpu.bitcast(y[:, :HALF].astype(BF16).astype(F32), U32)
    lo = pltpu.bitcast(y[:, HALF:].astype(BF16).astype(F32), U32)
    return hi | (lo >> 16)


def _unpack_halves(w):
    hi = pltpu.bitcast(w & jnp.uint32(0xFFFF0000), F32)
    lo = pltpu.bitcast(w << 16, F32)
    return hi, lo


def _head_slab(p, off, h):
    s = p[:, off + (h // 2) * LANES: off + (h // 2) * LANES + LANES]
    return s if h % 2 == 0 else pltpu.roll(s, HEAD_DIM, 1)


def _augment(qs, ks, vs, f_col, lane):
    fc = jnp.broadcast_to(f_col, lane.shape)
    hi = fc.astype(BF16).astype(F32)
    r1 = fc - hi
    mid = r1.astype(BF16).astype(F32)
    lo = r1 - mid
    one = jnp.ones_like(fc)
    zero = jnp.zeros_like(fc)
    ka = jnp.where(lane < 64, ks, jnp.where(lane < 67, one, jnp.where(
        lane == 67, -hi, jnp.where(lane == 68, -mid, jnp.where(lane == 69, -lo, zero)))))
    va = jnp.where(lane < 64, vs, jnp.where(lane == 64, one, zero))
    if qs is None:
        return None, ka.astype(BF16), va.astype(BF16)
    qa = jnp.where(lane < 64, qs, jnp.where(lane == 64, hi, jnp.where(
        lane == 65, mid, jnp.where(lane == 66, lo, jnp.where(lane < 70, one, zero)))))
    return qa.astype(BF16), ka.astype(BF16), va.astype(BF16)


def _inproj_kernel(x_ref, f0_ref, lng_ref, lnb_ref, wm_ref, bm_ref, wf_ref, bf_ref,
                   u_ref, k_ref, v_ref, logf_ref, qa_ref, ka_ref, va_ref, carry_ref, *, tm):
    t = pl.program_id(1)

    @pl.when(t == 0)
    def _():
        carry_ref[...] = f0_ref[...]

    xn = _layer_norm(x_ref[...], lng_ref[...], lnb_ref[...]).astype(BF16)
    p = jnp.dot(xn, wm_ref[...], preferred_element_type=F32) + bm_ref[...]
    fl = jnp.dot(xn, wf_ref[...], preferred_element_type=F32) + bf_ref[...]
    logf = jnp.minimum(fl, 0.0) - jnp.log(1.0 + jnp.exp(-jnp.abs(fl)))
    f_cum = _prefix_rows(logf, tm, True) + carry_ref[...]
    carry_ref[...] = f_cum[tm - 1:tm, :]

    u_ref[...] = p[:, :POOL_W]
    k_ref[...] = p[:, POOL_W + ATT_W:POOL_W + 2 * ATT_W]
    v_ref[...] = p[:, POOL_W + 2 * ATT_W:]
    logf_ref[...] = logf[:, :N_HEADS]

    lane = lax.broadcasted_iota(I32, (tm, AUG_W), 1)
    for h in range(N_HEADS):
        qs = _head_slab(p, POOL_W, h) * ATT_SCALE
        ks = _head_slab(p, POOL_W + ATT_W, h)
        vs = _head_slab(p, POOL_W + 2 * ATT_W, h)
        qa, ka, va = _augment(qs, ks, vs, f_cum[:, h:h + 1], lane)
        qa_ref[h] = qa
        ka_ref[h] = ka
        va_ref[h] = va


def _inproj(x, f0, lng, lnb, wm, bm, wf, bf):
    B, T, D = x.shape
    tm = min(SEQ_TILE, T)
    grid = (B, T // tm)
    const = lambda b, t: (0, 0)
    row = lambda b, t: (b, t, 0)
    aug = pl.BlockSpec((None, N_HEADS, tm, AUG_W), lambda b, t: (b, 0, t, 0))
    aug_shape = jax.ShapeDtypeStruct((B, N_HEADS, T, AUG_W), BF16)
    return pl.pallas_call(
        functools.partial(_inproj_kernel, tm=tm),
        grid=grid,
        in_specs=[
            pl.BlockSpec((None, tm, D), row),
            pl.BlockSpec((None, 1, LANES), lambda b, t: (b, 0, 0)),
            pl.BlockSpec((1, D), const), pl.BlockSpec((1, D), const),
            pl.BlockSpec((D, MAIN_W), const), pl.BlockSpec((1, MAIN_W), const),
            pl.BlockSpec((D, LANES), const), pl.BlockSpec((1, LANES), const),
        ],
        out_specs=[
            pl.BlockSpec((None, tm, POOL_W), row),
            pl.BlockSpec((None, tm, ATT_W), row),
            pl.BlockSpec((None, tm, ATT_W), row),
            pl.BlockSpec((None, tm, N_HEADS), row),
            aug, aug, aug,
        ],
        out_shape=[
            jax.ShapeDtypeStruct((B, T, POOL_W), F32),
            jax.ShapeDtypeStruct((B, T, ATT_W), F32),
            jax.ShapeDtypeStruct((B, T, ATT_W), F32),
            jax.ShapeDtypeStruct((B, T, N_HEADS), F32),
            aug_shape, aug_shape, aug_shape,
        ],
        scratch_shapes=[pltpu.VMEM((1, LANES), F32)],
        compiler_params=pltpu.CompilerParams(
            dimension_semantics=("arbitrary", "arbitrary"), vmem_limit_bytes=VMEM_LIMIT),
    )(x, f0, lng, lnb, wm, bm, wf, bf)


def _cache_kernel(ck_ref, cv_ref, clf_ref, ka_ref, va_ref, ftot_ref, carry_ref, *, tp):
    t = pl.program_id(1)

    @pl.when(t == 0)
    def _():
        carry_ref[...] = jnp.zeros_like(carry_ref)

    f_cum = _prefix_rows(clf_ref[...], tp, True) + carry_ref[...]
    carry_ref[...] = f_cum[tp - 1:tp, :]
    ftot_ref[...] = f_cum[tp - 1:tp, :]
    ck = ck_ref[...]
    cv = cv_ref[...]
    lane = lax.broadcasted_iota(I32, (tp, AUG_W), 1)
    for h in range(N_HEADS):
        _, ka, va = _augment(None, _head_slab(ck, 0, h), _head_slab(cv, 0, h), f_cum[:, h:h + 1], lane)
        ka_ref[h] = ka
        va_ref[h] = va


def _cache_prep(ck, cv, clf):
    B, P, _ = ck.shape
    tp = min(SEQ_TILE, P)
    row = lambda b, t: (b, t, 0)
    aug = pl.BlockSpec((None, N_HEADS, tp, AUG_W), lambda b, t: (b, 0, t, 0))
    aug_shape = jax.ShapeDtypeStruct((B, N_HEADS, P, AUG_W), BF16)
    return pl.pallas_call(
        functools.partial(_cache_kernel, tp=tp),
        grid=(B, P // tp),
        in_specs=[pl.BlockSpec((None, tp, ATT_W), row), pl.BlockSpec((None, tp, ATT_W), row),
                  pl.BlockSpec((None, tp, LANES), row)],
        out_specs=[aug, aug, pl.BlockSpec((None, 1, LANES), lambda b, t: (b, 0, 0))],
        out_shape=[aug_shape, aug_shape, jax.ShapeDtypeStruct((B, 1, LANES), F32)],
        scratch_shapes=[pltpu.VMEM((1, LANES), F32)],
        compiler_params=pltpu.CompilerParams(
            dimension_semantics=("arbitrary", "arbitrary"), vmem_limit_bytes=VMEM_LIMIT),
    )(ck, cv, clf)


def _attn_kernel(qa_ref, ka_ref, va_ref, o_ref, *, tq, tk, q_off):
    iq = pl.program_id(2)
    q_lo = q_off + iq * tq
    n_full = (q_lo + 1) // tk
    n_all = (q_lo + tq - 1) // tk + 1
    lane = lax.broadcasted_iota(I32, (tq, AUG_W), 1)
    outs = []
    for hh in range(2):
        q = qa_ref[hh]

        def step(j, carry, masked, hh=hh, q=q):
            m, acc = carry
            k0 = pl.multiple_of(j * tk, tk)
            k = ka_ref[hh, pl.ds(k0, tk), :]
            v = va_ref[hh, pl.ds(k0, tk), :]
            s = lax.dot_general(q, k, (((1,), (1,)), ((), ())), preferred_element_type=F32)
            if masked:
                qpos = q_lo + lax.broadcasted_iota(I32, (tq, tk), 0)
                kpos = k0 + lax.broadcasted_iota(I32, (tq, tk), 1)
                s = jnp.where(qpos >= kpos, s, -jnp.inf)
            m_new = jnp.maximum(m, jnp.max(s, axis=-1, keepdims=True))
            alpha = jnp.exp(m - m_new)
            p = jnp.exp(s - m_new)
            acc = acc * alpha + jnp.dot(p.astype(BF16), v, preferred_element_type=F32)
            return m_new, acc

        carry = (jnp.full((tq, 1), -jnp.inf, F32), jnp.zeros((tq, AUG_W), F32))
        carry = lax.fori_loop(0, n_full, functools.partial(step, masked=False), carry)
        _, acc = lax.fori_loop(n_full, n_all, functools.partial(step, masked=True), carry)
        outs.append(acc / acc[:, HEAD_DIM:HEAD_DIM + 1])
    o_ref[...] = jnp.where(lane < HEAD_DIM, outs[0], pltpu.roll(outs[1], HEAD_DIM, 1)).astype(BF16)


def _attention(qa, ka, va, q_off, tq, tk):
    B, H, Tq, _ = qa.shape
    Tk = ka.shape[2]
    return pl.pallas_call(
        functools.partial(_attn_kernel, tq=tq, tk=tk, q_off=q_off),
        grid=(B, H // 2, Tq // tq),
        in_specs=[
            pl.BlockSpec((None, 2, tq, AUG_W), lambda b, hp, iq: (b, hp, iq, 0)),
            pl.BlockSpec((None, 2, Tk, AUG_W), lambda b, hp, iq: (b, hp, 0, 0)),
            pl.BlockSpec((None, 2, Tk, AUG_W), lambda b, hp, iq: (b, hp, 0, 0)),
        ],
        out_specs=pl.BlockSpec((None, tq, 2 * HEAD_DIM), lambda b, hp, iq: (b, iq, hp)),
        out_shape=jax.ShapeDtypeStruct((B, Tq, ATT_W), BF16),
        compiler_params=pltpu.CompilerParams(
            dimension_semantics=("arbitrary", "arbitrary", "arbitrary"), vmem_limit_bytes=VMEM_LIMIT),
    )(qa, ka, va)


def _mixer_kernel(x_ref, u_ref, uprev_ref, hist_ref, ya_ref, cnt0_ref,
                  lng_ref, lnb_ref, wg_ref, bg_ref, wpool_ref, spool_ref, wpp_ref, watt_ref, wout_ref,
                  ln1g_ref, ln1b_ref, wrh_ref, wrl_ref, br_ref,
                  h_ref, hp_ref, ti_ref, tw_ref, rk_ref, cnt_ref, uext_ref, *, tm, start_pos):
    b = pl.program_id(0)
    t = pl.program_id(1)

    @pl.when((b == 0) & (t == 0))
    def _():
        cnt_ref[...] = cnt0_ref[...]

    xn = _layer_norm(x_ref[...], lng_ref[...], lnb_ref[...])
    g = jnp.dot(xn.astype(BF16), wg_ref[...], preferred_element_type=F32) + bg_ref[...]
    gates = jax.nn.sigmoid(g)

    @pl.when(t == 0)
    def _():
        uext_ref[0:HALO, :] = hist_ref[...]

    @pl.when(t > 0)
    def _():
        uext_ref[0:HALO, :] = uprev_ref[...]

    uext_ref[HALO:HALO + tm, :] = u_ref[...]
    pos = start_pos + t * tm + lax.broadcasted_iota(I32, (tm, 1), 0)
    pooled = None
    for gi, w in enumerate(POOL_WINDOWS):
        sl = slice(gi * POOL_GC, (gi + 1) * POOL_GC)
        cur = uext_ref[HALO:HALO + tm, sl]
        s = cur
        for j in range(1, w):
            s = s + uext_ref[HALO - j:HALO - j + tm, sl]
        count = jnp.minimum(pos + 1, w).astype(F32)
        d = s / count - cur
        yg = jnp.dot(d.astype(BF16), wpool_ref[gi], preferred_element_type=F32) * spool_ref[:, sl]
        contrib = jnp.dot(yg.astype(BF16), wpp_ref[sl, :], preferred_element_type=F32)
        pooled = contrib if pooled is None else pooled + contrib
    att = jnp.dot(ya_ref[...], watt_ref[...], preferred_element_type=F32)
    m = gates[:, :D_MODEL] * pooled + gates[:, D_MODEL:] * att
    mix = jnp.dot(m.astype(BF16), wout_ref[...], preferred_element_type=F32)
    h = _layer_norm(DEEPNORM_ALPHA * xn + mix, ln1g_ref[...], ln1b_ref[...])
    h_ref[...] = h
    hp_ref[...] = _pack_halves(h)

    hh = h.astype(BF16)
    hl = (h - hh.astype(F32)).astype(BF16)
    logits = (jnp.dot(hh, wrh_ref[...], preferred_element_type=F32)
              + jnp.dot(hl, wrh_ref[...], preferred_element_type=F32)
              + jnp.dot(hh, wrl_ref[...], preferred_element_type=F32) + br_ref[...])
    lane = lax.broadcasted_iota(I32, (tm, LANES), 1)
    work = jnp.where(lane < N_EXPERTS, logits, -jnp.inf)
    vals, idxs = [], []
    for _ in range(TOP_K):
        mx = jnp.max(work, axis=-1, keepdims=True)
        idx = jnp.min(jnp.where(work == mx, lane, LANES), axis=-1, keepdims=True)
        vals.append(mx)
        idxs.append(idx)
        work = jnp.where(lane == idx, -jnp.inf, work)
    exps = [jnp.exp(v - vals[0]) for v in vals]
    denom = exps[0] + exps[1] + exps[2] + exps[3]

    onehot = jnp.zeros((tm, LANES), F32)
    for idx in idxs:
        onehot = onehot + (lane == idx).astype(F32)
    base = _prefix_rows(onehot, tm, False) + cnt_ref[...]
    ti = jnp.zeros((tm, LANES), I32)
    tw = jnp.zeros((tm, LANES), F32)
    rk = jnp.zeros((tm, LANES), F32)
    for k in range(TOP_K):
        rank = jnp.sum(jnp.where(lane == idxs[k], base, 0.0), axis=-1, keepdims=True)
        ti = jnp.where(lane == k, idxs[k], ti)
        tw = jnp.where(lane == k, exps[k] / denom, tw)
        rk = jnp.where(lane == k, rank, rk)
    ti_ref[...] = ti
    tw_ref[...] = tw
    rk_ref[...] = rk.astype(I32)
    cnt_ref[...] = cnt_ref[...] + jnp.sum(onehot, axis=0, keepdims=True)


def _mixer(x, u, hist, yatt, cnt0, start_pos, weights):
    B, T, D = x.shape
    tm = min(SEQ_TILE, T)
    row = lambda b, t: (b, t, 0)
    hpt = tm // HALO
    full = lambda a: pl.BlockSpec(a.shape, lambda b, t, _n=a.ndim: (0,) * _n)
    lane_out = jax.ShapeDtypeStruct((B, T, LANES), I32)
    return pl.pallas_call(
        functools.partial(_mixer_kernel, tm=tm, start_pos=start_pos),
        grid=(B, T // tm),
        in_specs=[
            pl.BlockSpec((None, tm, D), row),
            pl.BlockSpec((None, tm, POOL_W), row),
            pl.BlockSpec((None, HALO, POOL_W), lambda b, t: (b, jnp.maximum(t * hpt - 1, 0), 0)),
            pl.BlockSpec((None, HALO, POOL_W), lambda b, t: (b, 0, 0)),
            pl.BlockSpec((None, tm, ATT_W), row),
            pl.BlockSpec((1, LANES), lambda b, t: (0, 0)),
        ] + [full(w) for w in weights],
        out_specs=[
            pl.BlockSpec((None, tm, D), row),
            pl.BlockSpec((None, tm, HALF), row),
            pl.BlockSpec((None, tm, LANES), row),
            pl.BlockSpec((None, tm, LANES), row),
            pl.BlockSpec((None, tm, LANES), row),
            pl.BlockSpec((1, LANES), lambda b, t: (0, 0)),
        ],
        out_shape=[
            jax.ShapeDtypeStruct((B, T, D), F32),
            jax.ShapeDtypeStruct((B, T, HALF), U32),
            lane_out,
            jax.ShapeDtypeStruct((B, T, LANES), F32),
            lane_out,
            jax.ShapeDtypeStruct((1, LANES), F32),
        ],
        scratch_shapes=[pltpu.VMEM((HALO + tm, POOL_W), F32)],
        compiler_params=pltpu.CompilerParams(
            dimension_semantics=("arbitrary", "arbitrary"), vmem_limit_bytes=VMEM_LIMIT),
    )(x, u, u, hist, yatt, cnt0, *weights)


def _row_copy(src_hbm, src_row, dst, dst_row, sem):
    return pltpu.make_async_copy(src_hbm.at[pl.ds(src_row, 1)], dst.at[pl.ds(dst_row, 1)], sem)


def _dispatch_kernel(dest_ref, hp_hbm, xb_in_hbm, xb_hbm, sem, *, tn):
    del xb_in_hbm
    base = pl.program_id(0) * tn

    def start(r, c):
        for k in range(TOP_K):
            _row_copy(hp_hbm, base + r, xb_hbm, dest_ref[r * TOP_K + k], sem).start()
        return c

    def wait(r, c):
        for k in range(TOP_K):
            _row_copy(hp_hbm, base + r, xb_hbm, dest_ref[r * TOP_K + k], sem).wait()
        return c

    lax.fori_loop(0, tn, start, 0)
    lax.fori_loop(0, tn, wait, 0)


def _dispatch(dest_flat, hp, n_rows):
    n_tok = hp.shape[0]
    tn = ROW_TILE
    xb0 = jnp.zeros((n_rows, HALF), U32)
    return pl.pallas_call(
        functools.partial(_dispatch_kernel, tn=tn),
        grid=(n_tok // tn,),
        in_specs=[
            pl.BlockSpec((tn * TOP_K,), lambda i: (i,), memory_space=pltpu.SMEM),
            pl.BlockSpec(memory_space=pl.ANY),
            pl.BlockSpec(memory_space=pl.ANY),
        ],
        out_specs=pl.BlockSpec(memory_space=pl.ANY),
        out_shape=jax.ShapeDtypeStruct((n_rows, HALF), U32),
        scratch_shapes=[pltpu.SemaphoreType.DMA(())],
        input_output_aliases={2: 0},
        compiler_params=pltpu.CompilerParams(dimension_semantics=("arbitrary",)),
    )(dest_flat, hp, xb0)


def _expert_kernel(be_ref, nb_ref, xb_ref, w1_ref, b1_ref, w2_ref, b2_ref, yb_ref):
    del be_ref
    i = pl.program_id(0)

    @pl.when(i < nb_ref[0])
    def _():
        xa, xc = _unpack_halves(xb_ref[...])
        hfull = (jnp.dot(xa.astype(BF16), w1_ref[:HALF, :], preferred_element_type=F32)
                 + jnp.dot(xc.astype(BF16), w1_ref[HALF:, :], preferred_element_type=F32) + b1_ref[...])
        glu = jnp.minimum(hfull[:, :D_FF], SWIGLU_LIMIT)
        lin = jnp.clip(hfull[:, D_FF:], -SWIGLU_LIMIT, SWIGLU_LIMIT)
        a = glu * jax.nn.sigmoid(SWIGLU_ALPHA * glu) * (lin + 1.0)
        y = jnp.dot(a.astype(BF16), w2_ref[...], preferred_element_type=F32) + b2_ref[...]
        yb_ref[...] = _pack_halves(y)

    @pl.when(i >= nb_ref[0])
    def _():
        yb_ref[...] = jnp.zeros_like(yb_ref)


def _experts(block_e, n_used, xb, w1, b1, w2, b2):
    n_rows = xb.shape[0]
    bm = MOE_BLOCK
    return pl.pallas_call(
        _expert_kernel,
        grid_spec=pltpu.PrefetchScalarGridSpec(
            num_scalar_prefetch=2,
            grid=(n_rows // bm,),
            in_specs=[
                pl.BlockSpec((bm, HALF), lambda i, be, nb: (i, 0)),
                pl.BlockSpec((None, D_MODEL, 2 * D_FF), lambda i, be, nb: (be[i], 0, 0)),
                pl.BlockSpec((None, 1, 2 * D_FF), lambda i, be, nb: (be[i], 0, 0)),
                pl.BlockSpec((None, D_FF, D_MODEL), lambda i, be, nb: (be[i], 0, 0)),
                pl.BlockSpec((None, 1, D_MODEL), lambda i, be, nb: (be[i], 0, 0)),
            ],
            out_specs=pl.BlockSpec((bm, HALF), lambda i, be, nb: (i, 0)),
        ),
        out_shape=jax.ShapeDtypeStruct((n_rows, HALF), U32),
        compiler_params=pltpu.CompilerParams(
            dimension_semantics=("arbitrary",), vmem_limit_bytes=VMEM_LIMIT),
    )(block_e, n_used, xb, w1, b1, w2, b2)


def _combine_kernel(dest_ref, h_ref, tw_ref, g_ref, b_ref, yb_hbm, out_ref, gbuf, sem, *, tn):
    def start(r, c):
        for k in range(TOP_K):
            _row_copy(yb_hbm, dest_ref[r * TOP_K + k], gbuf.at[k], r, sem).start()
        return c

    def wait(r, c):
        for k in range(TOP_K):
            _row_copy(yb_hbm, dest_ref[r * TOP_K + k], gbuf.at[k], r, sem).wait()
        return c

    lax.fori_loop(0, tn, start, 0)
    lax.fori_loop(0, tn, wait, 0)
    tw = tw_ref[...]
    acc_hi = None
    acc_lo = None
    for k in range(TOP_K):
        hi, lo = _unpack_halves(gbuf[k])
        w = tw[:, k:k + 1]
        acc_hi = w * hi if acc_hi is None else acc_hi + w * hi
        acc_lo = w * lo if acc_lo is None else acc_lo + w * lo
    moe = jnp.concatenate([acc_hi, acc_lo], axis=1)
    out_ref[...] = _layer_norm(DEEPNORM_ALPHA * h_ref[...] + moe, g_ref[...], b_ref[...])


def _combine(dest_flat, h, tw, ln2g, ln2b, yb):
    n_tok, D = h.shape
    tn = ROW_TILE
    const = lambda i: (0, 0)
    return pl.pallas_call(
        functools.partial(_combine_kernel, tn=tn),
        grid=(n_tok // tn,),
        in_specs=[
            pl.BlockSpec((tn * TOP_K,), lambda i: (i,), memory_space=pltpu.SMEM),
            pl.BlockSpec((tn, D), lambda i: (i, 0)),
            pl.BlockSpec((tn, LANES), lambda i: (i, 0)),
            pl.BlockSpec((1, D), const), pl.BlockSpec((1, D), const),
            pl.BlockSpec(memory_space=pl.ANY),
        ],
        out_specs=pl.BlockSpec((tn, D), lambda i: (i, 0)),
        out_shape=jax.ShapeDtypeStruct((n_tok, D), F32),
        scratch_shapes=[pltpu.VMEM((TOP_K, tn, HALF), U32), pltpu.SemaphoreType.DMA(())],
        compiler_params=pltpu.CompilerParams(
            dimension_semantics=("arbitrary",), vmem_limit_bytes=VMEM_LIMIT),
    )(dest_flat, h, tw, ln2g, ln2b, yb)


def kernel(x_prompt, x_sample, cache_pool, cache_k, cache_v, cache_logf, ln_in_g, ln_in_b, w_in, b_in,
           w_pool, s_pool, w_pool_proj, w_att_proj, w_out, ln1_g, ln1_b, w_router, b_router,
           w1, b1, w2, b2, ln2_g, ln2_b):
    assert w_in.shape[0] == DEPTH
    B, T, D = x_prompt.shape
    Bs, Ts, _ = x_sample.shape
    P = cache_k.shape[2]
    row2 = lambda a: a.reshape(1, -1).astype(F32)

    f_off = MAIN_W
    g_off = MAIN_W + N_HEADS
    wm = w_in[0][:, :f_off].astype(BF16)
    bm = row2(b_in[0][:f_off])
    wf = jnp.pad(w_in[0][:, f_off:g_off], ((0, 0), (0, LANES - N_HEADS))).astype(BF16)
    bf = row2(jnp.pad(b_in[0][f_off:g_off], (0, LANES - N_HEADS)))
    wg = w_in[0][:, g_off:].astype(BF16)
    bg = row2(b_in[0][g_off:])
    lng, lnb = row2(ln_in_g), row2(ln_in_b)
    wr = jnp.pad(w_router[0], ((0, 0), (0, LANES - N_EXPERTS)))
    wr_hi = wr.astype(BF16)
    wr_lo = (wr - wr_hi.astype(F32)).astype(BF16)
    br = row2(jnp.pad(b_router[0], (0, LANES - N_EXPERTS)))
    mixer_weights = (lng, lnb, wg, bg, w_pool[0].astype(BF16), row2(s_pool[0]),
                     w_pool_proj[0].astype(BF16), w_att_proj[0].astype(BF16), w_out[0].astype(BF16),
                     row2(ln1_g[0]), row2(ln1_b[0]), wr_hi, wr_lo, br)

    zeros_f = jnp.zeros((B, 1, LANES), F32)
    u_p, k_p, v_p, logf_p, qa_p, ka_p, va_p = _inproj(x_prompt, zeros_f, lng, lnb, wm, bm, wf, bf)
    tq = min(ATT_TQ, T)
    ya_p = _attention(qa_p, ka_p, va_p, 0, tq, min(ATT_TK, T))

    clf = jnp.pad(cache_logf[0], ((0, 0), (0, 0), (0, LANES - N_HEADS)))
    ka_c, va_c, f_tot = _cache_prep(cache_k[0].reshape(Bs, P, ATT_W), cache_v[0].reshape(Bs, P, ATT_W), clf)
    u_s, k_s, v_s, logf_s, qa_s, ka_s, va_s = _inproj(x_sample, f_tot, lng, lnb, wm, bm, wf, bf)
    tks = LANES
    pad_keys = (-(P + Ts)) % tks
    zpad = jnp.zeros((Bs, N_HEADS, pad_keys, AUG_W), BF16)
    ya_s = _attention(qa_s, jnp.concatenate([ka_c, ka_s, zpad], axis=2),
                      jnp.concatenate([va_c, va_s, zpad], axis=2), P, Ts, tks)

    cnt0 = jnp.zeros((1, LANES), F32)
    hist_p = jnp.zeros((B, HALO, POOL_W), F32)
    h_p, hp_p, ti_p, tw_p, rk_p, cnt_p = _mixer(x_prompt, u_p, hist_p, ya_p, cnt0, 0, mixer_weights)
    u_full_s = jnp.concatenate([cache_pool[0].astype(F32), u_s], axis=1)
    hist_s = jnp.pad(cache_pool[0].astype(F32), ((0, 0), (HALO - POOL_HIST, 0), (0, 0)))
    h_s, hp_s, ti_s, tw_s, rk_s, cnt = _mixer(x_sample, u_s, hist_s, ya_s, cnt_p, P, mixer_weights)

    n_tok = B * T + Bs * Ts
    cat = lambda a, b: jnp.concatenate([a.reshape(B * T, -1), b.reshape(Bs * Ts, -1)], axis=0)
    h_all, hp_all = cat(h_p, h_s), cat(hp_p, hp_s)
    ti, tw, rk = cat(ti_p, ti_s)[:, :TOP_K], cat(tw_p, tw_s), cat(rk_p, rk_s)[:, :TOP_K]

    counts = cnt[0, :N_EXPERTS].astype(I32)
    padded = (counts + MOE_BLOCK - 1) // MOE_BLOCK * MOE_BLOCK
    pad_ends = jnp.cumsum(padded)
    pad_starts = pad_ends - padded
    dest = (pad_starts[ti] + rk).reshape(-1).astype(I32)
    n_blocks = (n_tok * TOP_K + N_EXPERTS * (MOE_BLOCK - 1) + MOE_BLOCK - 1) // MOE_BLOCK
    block_e = jnp.minimum(jnp.searchsorted(pad_ends, jnp.arange(n_blocks, dtype=I32) * MOE_BLOCK, side='right'),
                          N_EXPERTS - 1).astype(I32)
    n_used = (pad_ends[-1:] // MOE_BLOCK).astype(I32)

    xb = _dispatch(dest, hp_all, n_blocks * MOE_BLOCK)
    yb = _experts(block_e, n_used, xb, w1[0].astype(BF16), b1[0][:, None, :], w2[0].astype(BF16), b2[0][:, None, :])
    y_all = _combine(dest, h_all, tw, row2(ln2_g[0]), row2(ln2_b[0]), yb)

    y_prompt = y_all[:B * T].reshape(B, T, D)
    y_sample = y_all[B * T:].reshape(Bs, Ts, D)
    heads = lambda a, b_, t_: a.reshape(1, b_, t_, N_HEADS, HEAD_DIM)
    return (y_prompt, y_sample,
            heads(k_p, B, T), heads(v_p, B, T), logf_p[None], u_p[:, -POOL_HIST:][None],
            heads(k_s, Bs, Ts), heads(v_s, Bs, Ts), logf_s[None], u_full_s[:, -POOL_HIST:][None])
```

```python
import functools

import jax
import jax.numpy as jnp
from jax import lax
from jax.experimental import pallas as pl
from jax.experimental.pallas import tpu as pltpu

F32 = jnp.float32
BF16 = jnp.bfloat16
I32 = jnp.int32
U32 = jnp.uint32

D_MODEL = 1024
N_HEADS = 8
HEAD_DIM = 64
ATT_W = N_HEADS * HEAD_DIM
POOL_WINDOWS = (2, 4, 8, 16)
POOL_GC = 128
POOL_W = len(POOL_WINDOWS) * POOL_GC
POOL_HIST = max(POOL_WINDOWS) - 1
HALO = 16
N_EXPERTS = 32
TOP_K = 4
D_FF = D_MODEL
SWIGLU_ALPHA = 1.702
SWIGLU_LIMIT = 7.0
LN_EPS = 1e-5
DEPTH = 1
DEEPNORM_ALPHA = (2.0 * DEPTH) ** 0.25
ATT_SCALE = HEAD_DIM ** -0.5
LOG2E = 1.4426950408889634
MAIN_W = POOL_W + 3 * ATT_W
LANES = 128
AUG_W = LANES
HALF = D_MODEL // 2
VMEM_LIMIT = 56 * 1024 * 1024

SEQ_TILE = 512
ATT_TQ = 512
ATT_TK = 512
MOE_BLOCK = 256
ROW_TILE = 128


def _layer_norm(x, g, b):
    mu = jnp.mean(x, axis=-1, keepdims=True)
    xc = x - mu
    var = jnp.mean(xc * xc, axis=-1, keepdims=True)
    return xc * lax.rsqrt(var + LN_EPS) * g + b


def _split3(x):
    a = x.astype(BF16)
    r = x - a.astype(F32)
    b = r.astype(BF16)
    c = (r - b.astype(F32)).astype(BF16)
    return a, b, c


def _prefix_rows(x, tm, inclusive):
    kp = max(tm, LANES)
    r = lax.broadcasted_iota(I32, (tm, kp), 0)
    c = lax.broadcasted_iota(I32, (tm, kp), 1)
    tri = ((c <= r) if inclusive else (c < r)).astype(BF16)
    if kp > tm:
        x = jnp.concatenate([x, jnp.zeros((kp - tm, x.shape[1]), x.dtype)], axis=0)
    out = None
    for piece in _split3(x):
        y = jnp.dot(tri, piece, preferred_element_type=F32)
        out = y if out is None else out + y
    return out


def _pack_halves(y):
    hi = pltpu.bitcast(y[:, :HALF].astype(BF16).astype(F32), U32)
    lo = pltpu.bitcast(y[:, HALF:].astype(BF16).astype(F32), U32)
    return hi | (lo >> 16)


def _unpack_halves(w):
    hi = pltpu.bitcast(w & jnp.uint32(0xFFFF0000), F32)
    lo = pltpu.bitcast(w << 16, F32)
    return hi, lo


def _head_slab(p, off, h):
    s = p[:, off + (h // 2) * LANES: off + (h // 2) * LANES + LANES]
    return s if h % 2 == 0 else pltpu.roll(s, HEAD_DIM, 1)


def _augment(qs, ks, vs, f_col, lane):
    fc = jnp.broadcast_to(f_col, lane.shape)
    hi = fc.astype(BF16).astype(F32)
    r1 = fc - hi
    mid = r1.astype(BF16).astype(F32)
    lo = r1 - mid
    one = jnp.ones_like(fc)
    zero = jnp.zeros_like(fc)
    ka = jnp.where(lane < 64, ks, jnp.where(lane < 67, one, jnp.where(
        lane == 67, -hi, jnp.where(lane == 68, -mid, jnp.where(lane == 69, -lo, zero)))))
    va = jnp.where(lane < 64, vs, jnp.where(lane == 64, one, zero))
    if qs is None:
        return None, ka.astype(BF16), va.astype(BF16)
    qa = jnp.where(lane < 64, qs, jnp.where(lane == 64, hi, jnp.where(
        lane == 65, mid, jnp.where(lane == 66, lo, jnp.where(lane < 70, one, zero)))))
    return qa.astype(BF16), ka.astype(BF16), va.astype(BF16)


def _inproj_kernel(x_ref, f0_ref, lng_ref, lnb_ref, wm_ref, bm_ref, wf_ref, bf_ref,
                   u_ref, k_ref, v_ref, logf_ref, qa_ref, ka_ref, va_ref, carry_ref, *, tm):
    t = pl.program_id(1)

    @pl.when(t == 0)
    def _():
        carry_ref[...] = f0_ref[...]

    xn = _layer_norm(x_ref[...], lng_ref[...], lnb_ref[...]).astype(BF16)
    p = jnp.dot(xn, wm_ref[...], preferred_element_type=F32) + bm_ref[...]
    fl = jnp.dot(xn, wf_ref[...], preferred_element_type=F32) + bf_ref[...]
    logf = jnp.minimum(fl, 0.0) - jnp.log(1.0 + jnp.exp(-jnp.abs(fl)))
    f_cum = _prefix_rows(logf, tm, True) + carry_ref[...]
    carry_ref[...] = f_cum[tm - 1:tm, :]

    u_ref[...] = p[:, :POOL_W]
    k_ref[...] = p[:, POOL_W + ATT_W:POOL_W + 2 * ATT_W]
    v_ref[...] = p[:, POOL_W + 2 * ATT_W:]
    logf_ref[...] = logf[:, :N_HEADS]

    lane = lax.broadcasted_iota(I32, (tm, AUG_W), 1)
    for h in range(N_HEADS):
        qs = _head_slab(p, POOL_W, h) * (ATT_SCALE * LOG2E)
        ks = _head_slab(p, POOL_W + ATT_W, h)
        vs = _head_slab(p, POOL_W + 2 * ATT_W, h)
        qa, ka, va = _augment(qs, ks, vs, f_cum[:, h:h + 1] * LOG2E, lane)
        qa_ref[h] = qa
        ka_ref[h] = ka
        va_ref[h] = va


def _inproj(x, f0, lng, lnb, wm, bm, wf, bf):
    B, T, D = x.shape
    tm = min(SEQ_TILE, T)
    grid = (B, T // tm)
    const = lambda b, t: (0, 0)
    row = lambda b, t: (b, t, 0)
    aug = pl.BlockSpec((None, N_HEADS, tm, AUG_W), lambda b, t: (b, 0, t, 0))
    aug_shape = jax.ShapeDtypeStruct((B, N_HEADS, T, AUG_W), BF16)
    return pl.pallas_call(
        functools.partial(_inproj_kernel, tm=tm),
        grid=grid,
        in_specs=[
            pl.BlockSpec((None, tm, D), row),
            pl.BlockSpec((None, 1, LANES), lambda b, t: (b, 0, 0)),
            pl.BlockSpec((1, D), const), pl.BlockSpec((1, D), const),
            pl.BlockSpec((D, MAIN_W), const), pl.BlockSpec((1, MAIN_W), const),
            pl.BlockSpec((D, LANES), const), pl.BlockSpec((1, LANES), const),
        ],
        out_specs=[
            pl.BlockSpec((None, tm, POOL_W), row),
            pl.BlockSpec((None, tm, ATT_W), row),
            pl.BlockSpec((None, tm, ATT_W), row),
            pl.BlockSpec((None, tm, N_HEADS), row),
            aug, aug, aug,
        ],
        out_shape=[
            jax.ShapeDtypeStruct((B, T, POOL_W), F32),
            jax.ShapeDtypeStruct((B, T, ATT_W), F32),
            jax.ShapeDtypeStruct((B, T, ATT_W), F32),
            jax.ShapeDtypeStruct((B, T, N_HEADS), F32),
            aug_shape, aug_shape, aug_shape,
        ],
        scratch_shapes=[pltpu.VMEM((1, LANES), F32)],
        compiler_params=pltpu.CompilerParams(
            dimension_semantics=("arbitrary", "arbitrary"), vmem_limit_bytes=VMEM_LIMIT),
    )(x, f0, lng, lnb, wm, bm, wf, bf)


def _cache_kernel(ck_ref, cv_ref, clf_ref, ka_ref, va_ref, ftot_ref, carry_ref, *, tp):
    t = pl.program_id(1)

    @pl.when(t == 0)
    def _():
        carry_ref[...] = jnp.zeros_like(carry_ref)

    f_cum = _prefix_rows(clf_ref[...], tp, True) + carry_ref[...]
    carry_ref[...] = f_cum[tp - 1:tp, :]
    ftot_ref[...] = f_cum[tp - 1:tp, :]
    ck = ck_ref[...]
    cv = cv_ref[...]
    lane = lax.broadcasted_iota(I32, (tp, AUG_W), 1)
    for h in range(N_HEADS):
        _, ka, va = _augment(None, _head_slab(ck, 0, h), _head_slab(cv, 0, h), f_cum[:, h:h + 1] * LOG2E, lane)
        ka_ref[h] = ka
        va_ref[h] = va


def _cache_prep(ck, cv, clf):
    B, P, _ = ck.shape
    tp = min(SEQ_TILE, P)
    row = lambda b, t: (b, t, 0)
    aug = pl.BlockSpec((None, N_HEADS, tp, AUG_W), lambda b, t: (b, 0, t, 0))
    aug_shape = jax.ShapeDtypeStruct((B, N_HEADS, P, AUG_W), BF16)
    return pl.pallas_call(
        functools.partial(_cache_kernel, tp=tp),
        grid=(B, P // tp),
        in_specs=[pl.BlockSpec((None, tp, ATT_W), row), pl.BlockSpec((None, tp, ATT_W), row),
                  pl.BlockSpec((None, tp, LANES), row)],
        out_specs=[aug, aug, pl.BlockSpec((None, 1, LANES), lambda b, t: (b, 0, 0))],
        out_shape=[aug_shape, aug_shape, jax.ShapeDtypeStruct((B, 1, LANES), F32)],
        scratch_shapes=[pltpu.VMEM((1, LANES), F32)],
        compiler_params=pltpu.CompilerParams(
            dimension_semantics=("arbitrary", "arbitrary"), vmem_limit_bytes=VMEM_LIMIT),
    )(ck, cv, clf)


def _attn_kernel(qa_ref, ka_ref, va_ref, o_ref, *, tq, tk, q_off):
    iq = pl.program_id(2)
    q_lo = q_off + iq * tq
    n_full = (q_lo + 1) // tk
    n_all = (q_lo + tq - 1) // tk + 1
    lane = lax.broadcasted_iota(I32, (tq, AUG_W), 1)
    qs = (qa_ref[0], qa_ref[1])

    def step(j, carry, masked):
        k0 = pl.multiple_of(j * tk, tk)
        new = []
        for hh in range(2):
            m, acc = carry[hh]
            k = ka_ref[hh, pl.ds(k0, tk), :]
            v = va_ref[hh, pl.ds(k0, tk), :]
            s = lax.dot_general(qs[hh], k, (((1,), (1,)), ((), ())), preferred_element_type=F32)
            if masked:
                qpos = q_lo + lax.broadcasted_iota(I32, (tq, tk), 0)
                kpos = k0 + lax.broadcasted_iota(I32, (tq, tk), 1)
                s = jnp.where(qpos >= kpos, s, -jnp.inf)
            m_new = jnp.maximum(m, jnp.max(s, axis=-1, keepdims=True))
            alpha = jnp.exp2(m - m_new)
            p = jnp.exp2(s - m_new)
            acc = acc * alpha + jnp.dot(p.astype(BF16), v, preferred_element_type=F32)
            new.append((m_new, acc))
        return tuple(new)

    init = (jnp.full((tq, 1), -jnp.inf, F32), jnp.zeros((tq, AUG_W), F32))
    carry = lax.fori_loop(0, n_full, functools.partial(step, masked=False), (init, init))
    carry = lax.fori_loop(n_full, n_all, functools.partial(step, masked=True), carry)
    outs = [acc / acc[:, HEAD_DIM:HEAD_DIM + 1] for _, acc in carry]
    o_ref[...] = jnp.where(lane < HEAD_DIM, outs[0], pltpu.roll(outs[1], HEAD_DIM, 1)).astype(BF16)


def _attention(qa, ka, va, q_off, tq, tk):
    B, H, Tq, _ = qa.shape
    Tk = ka.shape[2]
    return pl.pallas_call(
        functools.partial(_attn_kernel, tq=tq, tk=tk, q_off=q_off),
        grid=(B, H // 2, Tq // tq),
        in_specs=[
            pl.BlockSpec((None, 2, tq, AUG_W), lambda b, hp, iq: (b, hp, iq, 0)),
            pl.BlockSpec((None, 2, Tk, AUG_W), lambda b, hp, iq: (b, hp, 0, 0)),
            pl.BlockSpec((None, 2, Tk, AUG_W), lambda b, hp, iq: (b, hp, 0, 0)),
        ],
        out_specs=pl.BlockSpec((None, tq, 2 * HEAD_DIM), lambda b, hp, iq: (b, iq, hp)),
        out_shape=jax.ShapeDtypeStruct((B, Tq, ATT_W), BF16),
        compiler_params=pltpu.CompilerParams(
            dimension_semantics=("arbitrary", "arbitrary", "arbitrary"), vmem_limit_bytes=VMEM_LIMIT),
    )(qa, ka, va)


def _mixer_kernel(x_ref, u_ref, uprev_ref, hist_ref, ya_ref, cnt0_ref,
                  lng_ref, lnb_ref, wg_ref, bg_ref, wpool_ref, spool_ref, wpp_ref, watt_ref, wout_ref,
                  ln1g_ref, ln1b_ref, wrh_ref, wrl_ref, br_ref, *rest, tm, start_pos):
    h_ref, hp_ref, ti_ref, tw_ref, rk_ref, cnt_ref, uext_ref = rest[-7:]
    b = pl.program_id(0)
    t = pl.program_id(1)

    @pl.when((b == 0) & (t == 0))
    def _():
        cnt_ref[...] = cnt0_ref[...]

    xn = _layer_norm(x_ref[...], lng_ref[...], lnb_ref[...])
    g = jnp.dot(xn.astype(BF16), wg_ref[...], preferred_element_type=F32) + bg_ref[...]
    gates = jax.nn.sigmoid(g)

    @pl.when(t == 0)
    def _():
        uext_ref[0:HALO, :] = hist_ref[...]

    @pl.when(t > 0)
    def _():
        uext_ref[0:HALO, :] = uprev_ref[...]

    uext_ref[HALO:HALO + tm, :] = u_ref[...]
    pos = start_pos + t * tm + lax.broadcasted_iota(I32, (tm, 1), 0)
    pooled = None
    for gi, w in enumerate(POOL_WINDOWS):
        sl = slice(gi * POOL_GC, (gi + 1) * POOL_GC)
        cur = uext_ref[HALO:HALO + tm, sl]
        s = cur
        for j in range(1, w):
            s = s + uext_ref[HALO - j:HALO - j + tm, sl]
        count = jnp.minimum(pos + 1, w).astype(F32)
        d = s / count - cur
        yg = jnp.dot(d.astype(BF16), wpool_ref[gi], preferred_element_type=F32) * spool_ref[:, sl]
        contrib = jnp.dot(yg.astype(BF16), wpp_ref[sl, :], preferred_element_type=F32)
        pooled = contrib if pooled is None else pooled + contrib
    att = jnp.dot(ya_ref[...], watt_ref[...], preferred_element_type=F32)
    m = gates[:, :D_MODEL] * pooled + gates[:, D_MODEL:] * att
    mix = jnp.dot(m.astype(BF16), wout_ref[...], preferred_element_type=F32)
    h = _layer_norm(DEEPNORM_ALPHA * xn + mix, ln1g_ref[...], ln1b_ref[...])
    h_ref[...] = h
    hp_ref[...] = _pack_halves(h)

    hh = h.astype(BF16)
    hl = (h - hh.astype(F32)).astype(BF16)
    logits = (jnp.dot(hh, wrh_ref[...], preferred_element_type=F32)
              + jnp.dot(hl, wrh_ref[...], preferred_element_type=F32)
              + jnp.dot(hh, wrl_ref[...], preferred_element_type=F32) + br_ref[...])
    lane = lax.broadcasted_iota(I32, (tm, LANES), 1)
    work = jnp.where(lane < N_EXPERTS, logits, -jnp.inf)
    vals, idxs = [], []
    for _ in range(TOP_K):
        mx = jnp.max(work, axis=-1, keepdims=True)
        idx = jnp.min(jnp.where(work == mx, lane, LANES), axis=-1, keepdims=True)
        vals.append(mx)
        idxs.append(idx)
        work = jnp.where(lane == idx, -jnp.inf, work)
    exps = [jnp.exp(v - vals[0]) for v in vals]
    denom = exps[0] + exps[1] + exps[2] + exps[3]

    onehot = jnp.zeros((tm, LANES), F32)
    for idx in idxs:
        onehot = onehot + (lane == idx).astype(F32)
    base = _prefix_rows(onehot, tm, False) + cnt_ref[...]
    ti = jnp.zeros((tm, LANES), I32)
    tw = jnp.zeros((tm, LANES), F32)
    rk = jnp.zeros((tm, LANES), F32)
    for k in range(TOP_K):
        rank = jnp.sum(jnp.where(lane == idxs[k], base, 0.0), axis=-1, keepdims=True)
        ti = jnp.where(lane == k, idxs[k], ti)
        tw = jnp.where(lane == k, exps[k] / denom, tw)
        rk = jnp.where(lane == k, rank, rk)
    ti_ref[...] = ti
    tw_ref[...] = tw
    rk_ref[...] = rk.astype(I32)
    cnt_ref[...] = cnt_ref[...] + jnp.sum(onehot, axis=0, keepdims=True)


def _mixer(x, u, hist, yatt, cnt0, start_pos, weights, n_tok, row_off, prev):
    B, T, D = x.shape
    tm = min(SEQ_TILE, T)
    nt = T // tm
    row = lambda b, t: (b, t, 0)
    tok = lambda b, t: (row_off // tm + b * nt + t, 0)
    hpt = tm // HALO
    full = lambda a: pl.BlockSpec(a.shape, lambda b, t, _n=a.ndim: (0,) * _n)
    widths = ((D, F32), (HALF, U32), (LANES, I32), (LANES, F32), (LANES, I32))
    n_in = 6 + len(weights)
    return pl.pallas_call(
        functools.partial(_mixer_kernel, tm=tm, start_pos=start_pos),
        grid=(B, nt),
        in_specs=[
            pl.BlockSpec((None, tm, D), row),
            pl.BlockSpec((None, tm, POOL_W), row),
            pl.BlockSpec((None, HALO, POOL_W), lambda b, t: (b, jnp.maximum(t * hpt - 1, 0), 0)),
            pl.BlockSpec((None, HALO, POOL_W), lambda b, t: (b, 0, 0)),
            pl.BlockSpec((None, tm, ATT_W), row),
            pl.BlockSpec((1, LANES), lambda b, t: (0, 0)),
        ] + [full(w) for w in weights] + [pl.BlockSpec(memory_space=pl.ANY) for _ in prev],
        out_specs=[pl.BlockSpec((tm, w), tok) for w, _ in widths] + [pl.BlockSpec((1, LANES), lambda b, t: (0, 0))],
        out_shape=[jax.ShapeDtypeStruct((n_tok, w), dt) for w, dt in widths]
        + [jax.ShapeDtypeStruct((1, LANES), F32)],
        scratch_shapes=[pltpu.VMEM((HALO + tm, POOL_W), F32)],
        input_output_aliases={n_in + i: i for i in range(len(prev))},
        compiler_params=pltpu.CompilerParams(
            dimension_semantics=("arbitrary", "arbitrary"), vmem_limit_bytes=VMEM_LIMIT),
    )(x, u, u, hist, yatt, cnt0, *weights, *prev)


def _row_copy(src, src_row, dst, dst_row, sem):
    return pltpu.make_async_copy(src.at[pl.ds(src_row, 1)], dst.at[pl.ds(dst_row, 1)], sem)


def _dispatch_kernel(dest_ref, hp_ref, xb_in_hbm, xb_hbm, sem, *, tn):
    del xb_in_hbm

    def start(r, c):
        for k in range(TOP_K):
            _row_copy(hp_ref, r, xb_hbm, dest_ref[r * TOP_K + k], sem).start()
        return c

    def wait(r, c):
        for k in range(TOP_K):
            _row_copy(hp_ref, r, xb_hbm, dest_ref[r * TOP_K + k], sem).wait()
        return c

    lax.fori_loop(0, tn, start, 0)
    lax.fori_loop(0, tn, wait, 0)


def _dispatch(dest_flat, hp, n_rows):
    n_tok = hp.shape[0]
    tn = ROW_TILE
    xb0 = jnp.zeros((n_rows, HALF), U32)
    return pl.pallas_call(
        functools.partial(_dispatch_kernel, tn=tn),
        grid=(n_tok // tn,),
        in_specs=[
            pl.BlockSpec((tn * TOP_K,), lambda i: (i,), memory_space=pltpu.SMEM),
            pl.BlockSpec((tn, HALF), lambda i: (i, 0)),
            pl.BlockSpec(memory_space=pl.ANY),
        ],
        out_specs=pl.BlockSpec(memory_space=pl.ANY),
        out_shape=jax.ShapeDtypeStruct((n_rows, HALF), U32),
        scratch_shapes=[pltpu.SemaphoreType.DMA(())],
        input_output_aliases={2: 0},
        compiler_params=pltpu.CompilerParams(dimension_semantics=("arbitrary",)),
    )(dest_flat, hp, xb0)


def _expert_kernel(be_ref, nb_ref, xb_ref, w1_ref, b1_ref, w2_ref, b2_ref, yb_ref):
    del be_ref
    i = pl.program_id(0)

    @pl.when(i < nb_ref[0])
    def _():
        xa, xc = _unpack_halves(xb_ref[...])
        hfull = (jnp.dot(xa.astype(BF16), w1_ref[:HALF, :], preferred_element_type=F32)
                 + jnp.dot(xc.astype(BF16), w1_ref[HALF:, :], preferred_element_type=F32) + b1_ref[...])
        glu = jnp.minimum(hfull[:, :D_FF], SWIGLU_LIMIT)
        lin = jnp.clip(hfull[:, D_FF:], -SWIGLU_LIMIT, SWIGLU_LIMIT)
        a = glu * jax.nn.sigmoid(SWIGLU_ALPHA * glu) * (lin + 1.0)
        y = jnp.dot(a.astype(BF16), w2_ref[...], preferred_element_type=F32) + b2_ref[...]
        yb_ref[...] = _pack_halves(y)

    @pl.when(i >= nb_ref[0])
    def _():
        yb_ref[...] = jnp.zeros_like(yb_ref)


def _experts(block_e, n_used, xb, w1, b1, w2, b2):
    n_rows = xb.shape[0]
    bm = MOE_BLOCK
    return pl.pallas_call(
        _expert_kernel,
        grid_spec=pltpu.PrefetchScalarGridSpec(
            num_scalar_prefetch=2,
            grid=(n_rows // bm,),
            in_specs=[
                pl.BlockSpec((bm, HALF), lambda i, be, nb: (i, 0)),
                pl.BlockSpec((None, D_MODEL, 2 * D_FF), lambda i, be, nb: (be[i], 0, 0)),
                pl.BlockSpec((None, 1, 2 * D_FF), lambda i, be, nb: (be[i], 0, 0)),
                pl.BlockSpec((None, D_FF, D_MODEL), lambda i, be, nb: (be[i], 0, 0)),
                pl.BlockSpec((None, 1, D_MODEL), lambda i, be, nb: (be[i], 0, 0)),
            ],
            out_specs=pl.BlockSpec((bm, HALF), lambda i, be, nb: (i, 0)),
        ),
        out_shape=jax.ShapeDtypeStruct((n_rows, HALF), U32),
        compiler_params=pltpu.CompilerParams(
            dimension_semantics=("arbitrary",), vmem_limit_bytes=VMEM_LIMIT),
    )(block_e, n_used, xb, w1, b1, w2, b2)


def _combine_kernel(dest_ref, h_ref, tw_ref, g_ref, b_ref, yb_hbm, out_ref, gbuf, sem, *, tn):
    def start(r, c):
        for k in range(TOP_K):
            _row_copy(yb_hbm, dest_ref[r * TOP_K + k], gbuf.at[k], r, sem).start()
        return c

    def wait(r, c):
        for k in range(TOP_K):
            _row_copy(yb_hbm, dest_ref[r * TOP_K + k], gbuf.at[k], r, sem).wait()
        return c

    lax.fori_loop(0, tn, start, 0)
    lax.fori_loop(0, tn, wait, 0)
    tw = tw_ref[...]
    acc_hi = None
    acc_lo = None
    for k in range(TOP_K):
        hi, lo = _unpack_halves(gbuf[k])
        w = tw[:, k:k + 1]
        acc_hi = w * hi if acc_hi is None else acc_hi + w * hi
        acc_lo = w * lo if acc_lo is None else acc_lo + w * lo
    moe = jnp.concatenate([acc_hi, acc_lo], axis=1)
    out_ref[...] = _layer_norm(DEEPNORM_ALPHA * h_ref[...] + moe, g_ref[...], b_ref[...])


def _combine(dest_flat, h, tw, ln2g, ln2b, yb, row_off, n_rows):
    D = h.shape[1]
    tn = ROW_TILE
    off = row_off // tn
    const = lambda i: (0, 0)
    return pl.pallas_call(
        functools.partial(_combine_kernel, tn=tn),
        grid=(n_rows // tn,),
        in_specs=[
            pl.BlockSpec((tn * TOP_K,), lambda i: (off + i,), memory_space=pltpu.SMEM),
            pl.BlockSpec((tn, D), lambda i: (off + i, 0)),
            pl.BlockSpec((tn, LANES), lambda i: (off + i, 0)),
            pl.BlockSpec((1, D), const), pl.BlockSpec((1, D), const),
            pl.BlockSpec(memory_space=pl.ANY),
        ],
        out_specs=pl.BlockSpec((tn, D), lambda i: (i, 0)),
        out_shape=jax.ShapeDtypeStruct((n_rows, D), F32),
        scratch_shapes=[pltpu.VMEM((TOP_K, tn, HALF), U32), pltpu.SemaphoreType.DMA(())],
        compiler_params=pltpu.CompilerParams(
            dimension_semantics=("arbitrary",), vmem_limit_bytes=VMEM_LIMIT),
    )(dest_flat, h, tw, ln2g, ln2b, yb)


def kernel(x_prompt, x_sample, cache_pool, cache_k, cache_v, cache_logf, ln_in_g, ln_in_b, w_in, b_in,
           w_pool, s_pool, w_pool_proj, w_att_proj, w_out, ln1_g, ln1_b, w_router, b_router,
           w1, b1, w2, b2, ln2_g, ln2_b):
    assert w_in.shape[0] == DEPTH
    B, T, D = x_prompt.shape
    Bs, Ts, _ = x_sample.shape
    P = cache_k.shape[2]
    row2 = lambda a: a.reshape(1, -1).astype(F32)

    f_off = MAIN_W
    g_off = MAIN_W + N_HEADS
    wm = w_in[0][:, :f_off].astype(BF16)
    bm = row2(b_in[0][:f_off])
    wf = jnp.pad(w_in[0][:, f_off:g_off], ((0, 0), (0, LANES - N_HEADS))).astype(BF16)
    bf = row2(jnp.pad(b_in[0][f_off:g_off], (0, LANES - N_HEADS)))
    wg = w_in[0][:, g_off:].astype(BF16)
    bg = row2(b_in[0][g_off:])
    lng, lnb = row2(ln_in_g), row2(ln_in_b)
    wr = jnp.pad(w_router[0], ((0, 0), (0, LANES - N_EXPERTS)))
    wr_hi = wr.astype(BF16)
    wr_lo = (wr - wr_hi.astype(F32)).astype(BF16)
    br = row2(jnp.pad(b_router[0], (0, LANES - N_EXPERTS)))
    mixer_weights = (lng, lnb, wg, bg, w_pool[0].astype(BF16), row2(s_pool[0]),
                     w_pool_proj[0].astype(BF16), w_att_proj[0].astype(BF16), w_out[0].astype(BF16),
                     row2(ln1_g[0]), row2(ln1_b[0]), wr_hi, wr_lo, br)

    zeros_f = jnp.zeros((B, 1, LANES), F32)
    u_p, k_p, v_p, logf_p, qa_p, ka_p, va_p = _inproj(x_prompt, zeros_f, lng, lnb, wm, bm, wf, bf)
    tq = min(ATT_TQ, T)
    ya_p = _attention(qa_p, ka_p, va_p, 0, tq, min(ATT_TK, T))

    clf = jnp.pad(cache_logf[0], ((0, 0), (0, 0), (0, LANES - N_HEADS)))
    ka_c, va_c, f_tot = _cache_prep(cache_k[0].reshape(Bs, P, ATT_W), cache_v[0].reshape(Bs, P, ATT_W), clf)
    u_s, k_s, v_s, logf_s, qa_s, ka_s, va_s = _inproj(x_sample, f_tot, lng, lnb, wm, bm, wf, bf)
    pad_keys = (-(P + Ts)) % LANES
    tks = P + Ts + pad_keys
    zpad = jnp.zeros((Bs, N_HEADS, pad_keys, AUG_W), BF16)
    ya_s = _attention(qa_s, jnp.concatenate([ka_c, ka_s, zpad], axis=2),
                      jnp.concatenate([va_c, va_s, zpad], axis=2), P, Ts, tks)

    cnt0 = jnp.zeros((1, LANES), F32)
    hist_p = jnp.zeros((B, HALO, POOL_W), F32)
    n_tok = B * T + Bs * Ts
    *bufs, cnt_p = _mixer(x_prompt, u_p, hist_p, ya_p, cnt0, 0, mixer_weights, n_tok, 0, ())
    u_full_s = jnp.concatenate([cache_pool[0].astype(F32), u_s], axis=1)
    hist_s = jnp.pad(cache_pool[0].astype(F32), ((0, 0), (HALO - POOL_HIST, 0), (0, 0)))
    h_all, hp_all, ti, tw, rk, cnt = _mixer(x_sample, u_s, hist_s, ya_s, cnt_p, P, mixer_weights,
                                            n_tok, B * T, tuple(bufs))
    ti, rk = ti[:, :TOP_K], rk[:, :TOP_K]

    counts = cnt[0, :N_EXPERTS].astype(I32)
    padded = (counts + MOE_BLOCK - 1) // MOE_BLOCK * MOE_BLOCK
    pad_ends = jnp.cumsum(padded)
    pad_starts = pad_ends - padded
    dest = (pad_starts[ti] + rk).reshape(-1).astype(I32)
    n_blocks = (n_tok * TOP_K + N_EXPERTS * (MOE_BLOCK - 1) + MOE_BLOCK - 1) // MOE_BLOCK
    block_start = jnp.arange(n_blocks, dtype=I32) * MOE_BLOCK
    block_e = jnp.minimum(jnp.sum((pad_ends[None, :] <= block_start[:, None]).astype(I32), axis=1), N_EXPERTS - 1)
    n_used = (pad_ends[-1:] // MOE_BLOCK).astype(I32)

    xb = _dispatch(dest, hp_all, n_blocks * MOE_BLOCK)
    yb = _experts(block_e, n_used, xb, w1[0].astype(BF16), b1[0][:, None, :], w2[0].astype(BF16), b2[0][:, None, :])
    ln2g, ln2b = row2(ln2_g[0]), row2(ln2_b[0])
    y_prompt = _combine(dest, h_all, tw, ln2g, ln2b, yb, 0, B * T).reshape(B, T, D)
    y_sample = _combine(dest, h_all, tw, ln2g, ln2b, yb, B * T, Bs * Ts).reshape(Bs, Ts, D)
    heads = lambda a, b_, t_: a.reshape(1, b_, t_, N_HEADS, HEAD_DIM)
    return (y_prompt, y_sample,
            heads(k_p, B, T), heads(v_p, B, T), logf_p[None], u_p[:, -POOL_HIST:][None],
            heads(k_s, Bs, Ts), heads(v_s, Bs, Ts), logf_s[None], u_full_s[:, -POOL_HIST:][None])
```

```python
import functools

import jax
import jax.numpy as jnp
from jax import lax
from jax.experimental import pallas as pl
from jax.experimental.pallas import tpu as pltpu
from jax.experimental.pallas import tpu_sc as plsc

F32 = jnp.float32
BF16 = jnp.bfloat16
I32 = jnp.int32
U32 = jnp.uint32

D_MODEL = 1024
N_HEADS = 8
HEAD_DIM = 64
ATT_W = N_HEADS * HEAD_DIM
POOL_WINDOWS = (2, 4, 8, 16)
POOL_GC = 128
POOL_W = len(POOL_WINDOWS) * POOL_GC
POOL_HIST = max(POOL_WINDOWS) - 1
HALO = 16
N_EXPERTS = 32
TOP_K = 4
D_FF = D_MODEL
SWIGLU_ALPHA = 1.702
SWIGLU_LIMIT = 7.0
LN_EPS = 1e-5
DEPTH = 1
DEEPNORM_ALPHA = (2.0 * DEPTH) ** 0.25
ATT_SCALE = HEAD_DIM ** -0.5
LOG2E = 1.4426950408889634
MAIN_W = POOL_W + 3 * ATT_W
LANES = 128
AUG_W = LANES
HALF = D_MODEL // 2
VMEM_LIMIT = 56 * 1024 * 1024

SEQ_TILE = 512
ATT_TQ = 1024
ATT_TK = 1024
MOE_BLOCK = 256
PLANE = HALF // 2
SC_WINDOW = 128


def _layer_norm(x, g, b):
    mu = jnp.mean(x, axis=-1, keepdims=True)
    xc = x - mu
    var = jnp.mean(xc * xc, axis=-1, keepdims=True)
    return xc * lax.rsqrt(var + LN_EPS) * g + b


def _split3(x):
    a = x.astype(BF16)
    r = x - a.astype(F32)
    b = r.astype(BF16)
    c = (r - b.astype(F32)).astype(BF16)
    return a, b, c


def _prefix_rows(x, tm, inclusive):
    kp = max(tm, LANES)
    r = lax.broadcasted_iota(I32, (tm, kp), 0)
    c = lax.broadcasted_iota(I32, (tm, kp), 1)
    tri = ((c <= r) if inclusive else (c < r)).astype(BF16)
    if kp > tm:
        x = jnp.concatenate([x, jnp.zeros((kp - tm, x.shape[1]), x.dtype)], axis=0)
    out = None
    for piece in _split3(x):
        y = jnp.dot(tri, piece, preferred_element_type=F32)
        out = y if out is None else out + y
    return out


def _pack_halves(y):
    hi = pltpu.bitcast(y[:, :HALF].astype(BF16).astype(F32), U32)
    lo = pltpu.bitcast(y[:, HALF:].astype(BF16).astype(F32), U32)
    return hi | (lo >> 16)


def _unpack_halves(w):
    hi = pltpu.bitcast(w & jnp.uint32(0xFFFF0000), F32)
    lo = pltpu.bitcast(w << 16, F32)
    return hi, lo


def _head_slab(p, off, h):
    s = p[:, off + (h // 2) * LANES: off + (h // 2) * LANES + LANES]
    return s if h % 2 == 0 else pltpu.roll(s, HEAD_DIM, 1)


def _augment(qs, ks, vs, f_col, lane):
    fc = jnp.broadcast_to(f_col, lane.shape)
    hi = fc.astype(BF16).astype(F32)
    r1 = fc - hi
    mid = r1.astype(BF16).astype(F32)
    lo = r1 - mid
    one = jnp.ones_like(fc)
    zero = jnp.zeros_like(fc)
    ka = jnp.where(lane < 64, ks, jnp.where(lane < 67, one, jnp.where(
        lane == 67, -hi, jnp.where(lane == 68, -mid, jnp.where(lane == 69, -lo, zero)))))
    va = jnp.where(lane < 64, vs, jnp.where(lane == 64, one, zero))
    if qs is None:
        return None, ka.astype(BF16), va.astype(BF16)
    qa = jnp.where(lane < 64, qs, jnp.where(lane == 64, hi, jnp.where(
        lane == 65, mid, jnp.where(lane == 66, lo, jnp.where(lane < 70, one, zero)))))
    return qa.astype(BF16), ka.astype(BF16), va.astype(BF16)


def _inproj_kernel(x_ref, f0_ref, lng_ref, lnb_ref, wm_ref, bm_ref, wf_ref, bf_ref,
                   u_ref, k_ref, v_ref, logf_ref, qa_ref, ka_ref, va_ref, carry_ref, *, tm):
    t = pl.program_id(1)

    @pl.when(t == 0)
    def _():
        carry_ref[...] = f0_ref[...]

    xn = _layer_norm(x_ref[...], lng_ref[...], lnb_ref[...]).astype(BF16)
    p = jnp.dot(xn, wm_ref[...], preferred_element_type=F32) + bm_ref[...]
    fl = jnp.dot(xn, wf_ref[...], preferred_element_type=F32) + bf_ref[...]
    logf = jnp.minimum(fl, 0.0) - jnp.log(1.0 + jnp.exp(-jnp.abs(fl)))
    f_cum = _prefix_rows(logf, tm, True) + carry_ref[...]
    carry_ref[...] = f_cum[tm - 1:tm, :]

    u_ref[...] = p[:, :POOL_W]
    k_ref[...] = p[:, POOL_W + ATT_W:POOL_W + 2 * ATT_W]
    v_ref[...] = p[:, POOL_W + 2 * ATT_W:]
    logf_ref[...] = logf[:, :N_HEADS]

    lane = lax.broadcasted_iota(I32, (tm, AUG_W), 1)
    for h in range(N_HEADS):
        qs = _head_slab(p, POOL_W, h) * (ATT_SCALE * LOG2E)
        ks = _head_slab(p, POOL_W + ATT_W, h)
        vs = _head_slab(p, POOL_W + 2 * ATT_W, h)
        qa, ka, va = _augment(qs, ks, vs, f_cum[:, h:h + 1] * LOG2E, lane)
        qa_ref[h] = qa
        ka_ref[h] = ka
        va_ref[h] = va


def _inproj(x, f0, lng, lnb, wm, bm, wf, bf):
    B, T, D = x.shape
    tm = min(SEQ_TILE, T)
    grid = (B, T // tm)
    const = lambda b, t: (0, 0)
    row = lambda b, t: (b, t, 0)
    aug = pl.BlockSpec((None, N_HEADS, tm, AUG_W), lambda b, t: (b, 0, t, 0))
    aug_shape = jax.ShapeDtypeStruct((B, N_HEADS, T, AUG_W), BF16)
    return pl.pallas_call(
        functools.partial(_inproj_kernel, tm=tm),
        grid=grid,
        in_specs=[
            pl.BlockSpec((None, tm, D), row),
            pl.BlockSpec((None, 1, LANES), lambda b, t: (b, 0, 0)),
            pl.BlockSpec((1, D), const), pl.BlockSpec((1, D), const),
            pl.BlockSpec((D, MAIN_W), const), pl.BlockSpec((1, MAIN_W), const),
            pl.BlockSpec((D, LANES), const), pl.BlockSpec((1, LANES), const),
        ],
        out_specs=[
            pl.BlockSpec((None, tm, POOL_W), row),
            pl.BlockSpec((None, tm, ATT_W), row),
            pl.BlockSpec((None, tm, ATT_W), row),
            pl.BlockSpec((None, tm, N_HEADS), row),
            aug, aug, aug,
        ],
        out_shape=[
            jax.ShapeDtypeStruct((B, T, POOL_W), F32),
            jax.ShapeDtypeStruct((B, T, ATT_W), F32),
            jax.ShapeDtypeStruct((B, T, ATT_W), F32),
            jax.ShapeDtypeStruct((B, T, N_HEADS), F32),
            aug_shape, aug_shape, aug_shape,
        ],
        scratch_shapes=[pltpu.VMEM((1, LANES), F32)],
        compiler_params=pltpu.CompilerParams(
            dimension_semantics=("arbitrary", "arbitrary"), vmem_limit_bytes=VMEM_LIMIT),
    )(x, f0, lng, lnb, wm, bm, wf, bf)


def _cache_kernel(ck_ref, cv_ref, clf_ref, ka_ref, va_ref, ftot_ref, carry_ref, *, tp):
    t = pl.program_id(1)

    @pl.when(t == 0)
    def _():
        carry_ref[...] = jnp.zeros_like(carry_ref)

    f_cum = _prefix_rows(clf_ref[...], tp, True) + carry_ref[...]
    carry_ref[...] = f_cum[tp - 1:tp, :]
    ftot_ref[...] = f_cum[tp - 1:tp, :]
    ck = ck_ref[...]
    cv = cv_ref[...]
    lane = lax.broadcasted_iota(I32, (tp, AUG_W), 1)
    for h in range(N_HEADS):
        _, ka, va = _augment(None, _head_slab(ck, 0, h), _head_slab(cv, 0, h), f_cum[:, h:h + 1] * LOG2E, lane)
        ka_ref[h] = ka
        va_ref[h] = va


def _cache_prep(ck, cv, clf):
    B, P, _ = ck.shape
    tp = min(SEQ_TILE, P)
    row = lambda b, t: (b, t, 0)
    aug = pl.BlockSpec((None, N_HEADS, tp, AUG_W), lambda b, t: (b, 0, t, 0))
    aug_shape = jax.ShapeDtypeStruct((B, N_HEADS, P, AUG_W), BF16)
    return pl.pallas_call(
        functools.partial(_cache_kernel, tp=tp),
        grid=(B, P // tp),
        in_specs=[pl.BlockSpec((None, tp, ATT_W), row), pl.BlockSpec((None, tp, ATT_W), row),
                  pl.BlockSpec((None, tp, LANES), row)],
        out_specs=[aug, aug, pl.BlockSpec((None, 1, LANES), lambda b, t: (b, 0, 0))],
        out_shape=[aug_shape, aug_shape, jax.ShapeDtypeStruct((B, 1, LANES), F32)],
        scratch_shapes=[pltpu.VMEM((1, LANES), F32)],
        compiler_params=pltpu.CompilerParams(
            dimension_semantics=("arbitrary", "arbitrary"), vmem_limit_bytes=VMEM_LIMIT),
    )(ck, cv, clf)


def _attn_kernel(qa_ref, ka_ref, va_ref, o_ref, *, tq, tk, q_off):
    iq = pl.program_id(2)
    q_lo = q_off + iq * tq
    n_full = (q_lo + 1) // tk
    n_all = (q_lo + tq - 1) // tk + 1
    lane = lax.broadcasted_iota(I32, (tq, AUG_W), 1)
    qs = (qa_ref[0], qa_ref[1])

    def step(j, carry, masked):
        k0 = pl.multiple_of(j * tk, tk)
        new = []
        for hh in range(2):
            m, acc = carry[hh]
            k = ka_ref[hh, pl.ds(k0, tk), :]
            v = va_ref[hh, pl.ds(k0, tk), :]
            s = lax.dot_general(qs[hh], k, (((1,), (1,)), ((), ())), preferred_element_type=F32)
            if masked:
                qpos = q_lo + lax.broadcasted_iota(I32, (tq, tk), 0)
                kpos = k0 + lax.broadcasted_iota(I32, (tq, tk), 1)
                s = jnp.where(qpos >= kpos, s, -jnp.inf)
            m_new = jnp.maximum(m, jnp.max(s, axis=-1, keepdims=True))
            alpha = jnp.exp2(m - m_new)
            p = jnp.exp2(s - m_new)
            acc = acc * alpha + jnp.dot(p.astype(BF16), v, preferred_element_type=F32)
            new.append((m_new, acc))
        return tuple(new)

    init = (jnp.full((tq, 1), -jnp.inf, F32), jnp.zeros((tq, AUG_W), F32))
    carry = lax.fori_loop(0, n_full, functools.partial(step, masked=False), (init, init))
    carry = lax.fori_loop(n_full, n_all, functools.partial(step, masked=True), carry)
    outs = [acc / acc[:, HEAD_DIM:HEAD_DIM + 1] for _, acc in carry]
    o_ref[...] = jnp.where(lane < HEAD_DIM, outs[0], pltpu.roll(outs[1], HEAD_DIM, 1)).astype(BF16)


def _attention(qa, ka, va, q_off, tq, tk):
    B, H, Tq, _ = qa.shape
    Tk = ka.shape[2]
    return pl.pallas_call(
        functools.partial(_attn_kernel, tq=tq, tk=tk, q_off=q_off),
        grid=(B, H // 2, Tq // tq),
        in_specs=[
            pl.BlockSpec((None, 2, tq, AUG_W), lambda b, hp, iq: (b, hp, iq, 0)),
            pl.BlockSpec((None, 2, Tk, AUG_W), lambda b, hp, iq: (b, hp, 0, 0)),
            pl.BlockSpec((None, 2, Tk, AUG_W), lambda b, hp, iq: (b, hp, 0, 0)),
        ],
        out_specs=pl.BlockSpec((None, tq, 2 * HEAD_DIM), lambda b, hp, iq: (b, iq, hp)),
        out_shape=jax.ShapeDtypeStruct((B, Tq, ATT_W), BF16),
        compiler_params=pltpu.CompilerParams(
            dimension_semantics=("arbitrary", "arbitrary", "arbitrary"), vmem_limit_bytes=VMEM_LIMIT),
    )(qa, ka, va)


def _mixer_kernel(x_ref, u_ref, uprev_ref, hist_ref, ya_ref, cnt0_ref,
                  lng_ref, lnb_ref, wg_ref, bg_ref, wpool_ref, spool_ref, wpp_ref, watt_ref, wout_ref,
                  ln1g_ref, ln1b_ref, wrh_ref, wrl_ref, br_ref, *rest, tm, start_pos):
    h_ref, hp_ref, ti_ref, tw_ref, rk_ref, cnt_ref, uext_ref = rest[-7:]
    b = pl.program_id(0)
    t = pl.program_id(1)

    @pl.when((b == 0) & (t == 0))
    def _():
        cnt_ref[...] = cnt0_ref[...]

    xn = _layer_norm(x_ref[...], lng_ref[...], lnb_ref[...])
    g = jnp.dot(xn.astype(BF16), wg_ref[...], preferred_element_type=F32) + bg_ref[...]
    gates = jax.nn.sigmoid(g)

    @pl.when(t == 0)
    def _():
        uext_ref[0:HALO, :] = hist_ref[...]

    @pl.when(t > 0)
    def _():
        uext_ref[0:HALO, :] = uprev_ref[...]

    uext_ref[HALO:HALO + tm, :] = u_ref[...]
    pos = start_pos + t * tm + lax.broadcasted_iota(I32, (tm, 1), 0)
    pooled = None
    for gi, w in enumerate(POOL_WINDOWS):
        sl = slice(gi * POOL_GC, (gi + 1) * POOL_GC)
        cur = uext_ref[HALO:HALO + tm, sl]
        s = cur
        for j in range(1, w):
            s = s + uext_ref[HALO - j:HALO - j + tm, sl]
        count = jnp.minimum(pos + 1, w).astype(F32)
        d = s / count - cur
        yg = jnp.dot(d.astype(BF16), wpool_ref[gi], preferred_element_type=F32) * spool_ref[:, sl]
        contrib = jnp.dot(yg.astype(BF16), wpp_ref[sl, :], preferred_element_type=F32)
        pooled = contrib if pooled is None else pooled + contrib
    att = jnp.dot(ya_ref[...], watt_ref[...], preferred_element_type=F32)
    m = gates[:, :D_MODEL] * pooled + gates[:, D_MODEL:] * att
    mix = jnp.dot(m.astype(BF16), wout_ref[...], preferred_element_type=F32)
    h = _layer_norm(DEEPNORM_ALPHA * xn + mix, ln1g_ref[...], ln1b_ref[...])
    h_ref[...] = h
    packed = _pack_halves(h)
    hp_ref[0] = packed[:, :PLANE]
    hp_ref[1] = packed[:, PLANE:]

    hh = h.astype(BF16)
    hl = (h - hh.astype(F32)).astype(BF16)
    logits = (jnp.dot(hh, wrh_ref[...], preferred_element_type=F32)
              + jnp.dot(hl, wrh_ref[...], preferred_element_type=F32)
              + jnp.dot(hh, wrl_ref[...], preferred_element_type=F32) + br_ref[...])
    lane = lax.broadcasted_iota(I32, (tm, LANES), 1)
    work = jnp.where(lane < N_EXPERTS, logits, -jnp.inf)
    vals, idxs = [], []
    for _ in range(TOP_K):
        mx = jnp.max(work, axis=-1, keepdims=True)
        idx = jnp.min(jnp.where(work == mx, lane, LANES), axis=-1, keepdims=True)
        vals.append(mx)
        idxs.append(idx)
        work = jnp.where(lane == idx, -jnp.inf, work)
    exps = [jnp.exp(v - vals[0]) for v in vals]
    denom = exps[0] + exps[1] + exps[2] + exps[3]

    onehot = jnp.zeros((tm, LANES), F32)
    for idx in idxs:
        onehot = onehot + (lane == idx).astype(F32)
    base = _prefix_rows(onehot, tm, False) + cnt_ref[...]
    ti = jnp.zeros((tm, LANES), I32)
    tw = jnp.zeros((tm, LANES), F32)
    rk = jnp.zeros((tm, LANES), F32)
    for k in range(TOP_K):
        rank = jnp.sum(jnp.where(lane == idxs[k], base, 0.0), axis=-1, keepdims=True)
        ti = jnp.where(lane == k, idxs[k], ti)
        tw = jnp.where(lane == k, exps[k] / denom, tw)
        rk = jnp.where(lane == k, rank, rk)
    ti_ref[...] = ti
    tw_ref[...] = tw
    rk_ref[...] = rk.astype(I32)
    cnt_ref[...] = cnt_ref[...] + jnp.sum(onehot, axis=0, keepdims=True)


def _mixer(x, u, hist, yatt, cnt0, start_pos, weights, n_tok, row_off, prev):
    B, T, D = x.shape
    tm = min(SEQ_TILE, T)
    nt = T // tm
    row = lambda b, t: (b, t, 0)
    tok = lambda b, t: (row_off // tm + b * nt + t, 0)
    hpt = tm // HALO
    full = lambda a: pl.BlockSpec(a.shape, lambda b, t, _n=a.ndim: (0,) * _n)
    tok3 = lambda b, t: (0, row_off // tm + b * nt + t, 0)
    out_specs = [pl.BlockSpec((tm, D), tok), pl.BlockSpec((2, tm, PLANE), tok3)] + [
        pl.BlockSpec((tm, LANES), tok) for _ in range(3)] + [pl.BlockSpec((1, LANES), lambda b, t: (0, 0))]
    out_shape = [jax.ShapeDtypeStruct((n_tok, D), F32), jax.ShapeDtypeStruct((2, n_tok, PLANE), U32),
                 jax.ShapeDtypeStruct((n_tok, LANES), I32), jax.ShapeDtypeStruct((n_tok, LANES), F32),
                 jax.ShapeDtypeStruct((n_tok, LANES), I32), jax.ShapeDtypeStruct((1, LANES), F32)]
    n_in = 6 + len(weights)
    return pl.pallas_call(
        functools.partial(_mixer_kernel, tm=tm, start_pos=start_pos),
        grid=(B, nt),
        in_specs=[
            pl.BlockSpec((None, tm, D), row),
            pl.BlockSpec((None, tm, POOL_W), row),
            pl.BlockSpec((None, HALO, POOL_W), lambda b, t: (b, jnp.maximum(t * hpt - 1, 0), 0)),
            pl.BlockSpec((None, HALO, POOL_W), lambda b, t: (b, 0, 0)),
            pl.BlockSpec((None, tm, ATT_W), row),
            pl.BlockSpec((1, LANES), lambda b, t: (0, 0)),
        ] + [full(w) for w in weights] + [pl.BlockSpec(memory_space=pl.ANY) for _ in prev],
        out_specs=out_specs,
        out_shape=out_shape,
        scratch_shapes=[pltpu.VMEM((HALO + tm, POOL_W), F32)],
        input_output_aliases={n_in + i: i for i in range(len(prev))},
        compiler_params=pltpu.CompilerParams(
            dimension_semantics=("arbitrary", "arbitrary"), vmem_limit_bytes=VMEM_LIMIT),
    )(x, u, u, hist, yatt, cnt0, *weights, *prev)


def _sc_mesh():
    return plsc.VectorSubcoreMesh(core_axis_name="c", subcore_axis_name="s")


def _sc_scatter_rows(x, idx, n_rows):
    n = x.shape[0]

    @pl.kernel(out_type=jax.ShapeDtypeStruct((n_rows, PLANE), x.dtype), mesh=_sc_mesh(), scratch_types=[])
    def scatter(x_hbm, i_hbm, o_hbm):
        def body(x_vmem, i_vmem):
            for k in range(TOP_K):
                pltpu.sync_copy(x_vmem, o_hbm.at[i_vmem.at[k]])

        pltpu.emit_pipeline(
            body,
            grid=(n // SC_WINDOW,),
            in_specs=[pl.BlockSpec((SC_WINDOW, PLANE), index_map=lambda i: (i, 0)),
                      pl.BlockSpec((TOP_K, SC_WINDOW), index_map=lambda i: (0, i))],
            out_specs=[],
            core_axis_name=("c", "s"),
            dimension_semantics=(pltpu.PARALLEL,),
        )(x_hbm, i_hbm)

    return scatter(x, idx)


def _sc_gather_rows(y, idx):
    n = idx.shape[1]

    @pl.kernel(out_type=jax.ShapeDtypeStruct((n, PLANE), y.dtype), mesh=_sc_mesh(), scratch_types=[])
    def gather(y_hbm, i_hbm, o_hbm):
        def body(i_vmem, o_vmem):
            pltpu.sync_copy(y_hbm.at[i_vmem.at[0]], o_vmem)

        pltpu.emit_pipeline(
            body,
            grid=(n // SC_WINDOW,),
            in_specs=[pl.BlockSpec((1, SC_WINDOW), index_map=lambda i: (0, i))],
            out_specs=[pl.BlockSpec((SC_WINDOW, PLANE), index_map=lambda i: (i, 0))],
            core_axis_name=("c", "s"),
            dimension_semantics=(pltpu.PARALLEL,),
        )(i_hbm, o_hbm)

    return gather(y, idx)


def _join_planes(ref):
    return jnp.concatenate([ref[0], ref[1]], axis=1)


def _expert_kernel(be_ref, nb_ref, xb_ref, w1_ref, b1_ref, w2_ref, b2_ref, yb_ref):
    del be_ref
    i = pl.program_id(0)

    @pl.when(i < nb_ref[0])
    def _():
        xa, xc = _unpack_halves(_join_planes(xb_ref))
        hfull = (jnp.dot(xa.astype(BF16), w1_ref[:HALF, :], preferred_element_type=F32)
                 + jnp.dot(xc.astype(BF16), w1_ref[HALF:, :], preferred_element_type=F32) + b1_ref[...])
        glu = jnp.minimum(hfull[:, :D_FF], SWIGLU_LIMIT)
        lin = jnp.clip(hfull[:, D_FF:], -SWIGLU_LIMIT, SWIGLU_LIMIT)
        a = glu * jax.nn.sigmoid(SWIGLU_ALPHA * glu) * (lin + 1.0)
        y = jnp.dot(a.astype(BF16), w2_ref[...], preferred_element_type=F32) + b2_ref[...]
        packed = _pack_halves(y)
        yb_ref[0] = packed[:, :PLANE]
        yb_ref[1] = packed[:, PLANE:]

    @pl.when(i >= nb_ref[0])
    def _():
        yb_ref[...] = jnp.zeros_like(yb_ref)


def _experts(block_e, n_used, xb, w1, b1, w2, b2):
    n_rows = xb.shape[1]
    bm = MOE_BLOCK
    rows = pl.BlockSpec((2, bm, PLANE), lambda i, be, nb: (0, i, 0))
    return pl.pallas_call(
        _expert_kernel,
        grid_spec=pltpu.PrefetchScalarGridSpec(
            num_scalar_prefetch=2,
            grid=(n_rows // bm,),
            in_specs=[
                rows,
                pl.BlockSpec((None, D_MODEL, 2 * D_FF), lambda i, be, nb: (be[i], 0, 0)),
                pl.BlockSpec((None, 1, 2 * D_FF), lambda i, be, nb: (be[i], 0, 0)),
                pl.BlockSpec((None, D_FF, D_MODEL), lambda i, be, nb: (be[i], 0, 0)),
                pl.BlockSpec((None, 1, D_MODEL), lambda i, be, nb: (be[i], 0, 0)),
            ],
            out_specs=rows,
        ),
        out_shape=jax.ShapeDtypeStruct((2, n_rows, PLANE), U32),
        compiler_params=pltpu.CompilerParams(
            dimension_semantics=("arbitrary",), vmem_limit_bytes=VMEM_LIMIT),
    )(block_e, n_used, xb, w1, b1, w2, b2)


def _combine_kernel(h_ref, tw_ref, g_ref, b_ref, y4_ref, out_ref):
    tw = tw_ref[...]
    acc_hi = None
    acc_lo = None
    for k in range(TOP_K):
        hi, lo = _unpack_halves(_join_planes(y4_ref.at[k]))
        w = tw[:, k:k + 1]
        acc_hi = w * hi if acc_hi is None else acc_hi + w * hi
        acc_lo = w * lo if acc_lo is None else acc_lo + w * lo
    moe = jnp.concatenate([acc_hi, acc_lo], axis=1)
    out_ref[...] = _layer_norm(DEEPNORM_ALPHA * h_ref[...] + moe, g_ref[...], b_ref[...])


def _combine(h, tw, ln2g, ln2b, y4, row_off, n_rows):
    D = h.shape[1]
    tn = min(SEQ_TILE, n_rows)
    off = row_off // tn
    const = lambda i: (0, 0)
    return pl.pallas_call(
        _combine_kernel,
        grid=(n_rows // tn,),
        in_specs=[
            pl.BlockSpec((tn, D), lambda i: (off + i, 0)),
            pl.BlockSpec((tn, LANES), lambda i: (off + i, 0)),
            pl.BlockSpec((1, D), const), pl.BlockSpec((1, D), const),
            pl.BlockSpec((TOP_K, 2, tn, PLANE), lambda i: (0, 0, off + i, 0)),
        ],
        out_specs=pl.BlockSpec((tn, D), lambda i: (i, 0)),
        out_shape=jax.ShapeDtypeStruct((n_rows, D), F32),
        compiler_params=pltpu.CompilerParams(
            dimension_semantics=("arbitrary",), vmem_limit_bytes=VMEM_LIMIT),
    )(h, tw, ln2g, ln2b, y4)


def kernel(x_prompt, x_sample, cache_pool, cache_k, cache_v, cache_logf, ln_in_g, ln_in_b, w_in, b_in,
           w_pool, s_pool, w_pool_proj, w_att_proj, w_out, ln1_g, ln1_b, w_router, b_router,
           w1, b1, w2, b2, ln2_g, ln2_b):
    assert w_in.shape[0] == DEPTH
    B, T, D = x_prompt.shape
    Bs, Ts, _ = x_sample.shape
    P = cache_k.shape[2]
    row2 = lambda a: a.reshape(1, -1).astype(F32)

    f_off = MAIN_W
    g_off = MAIN_W + N_HEADS
    wm = w_in[0][:, :f_off].astype(BF16)
    bm = row2(b_in[0][:f_off])
    wf = jnp.pad(w_in[0][:, f_off:g_off], ((0, 0), (0, LANES - N_HEADS))).astype(BF16)
    bf = row2(jnp.pad(b_in[0][f_off:g_off], (0, LANES - N_HEADS)))
    wg = w_in[0][:, g_off:].astype(BF16)
    bg = row2(b_in[0][g_off:])
    lng, lnb = row2(ln_in_g), row2(ln_in_b)
    wr = jnp.pad(w_router[0], ((0, 0), (0, LANES - N_EXPERTS)))
    wr_hi = wr.astype(BF16)
    wr_lo = (wr - wr_hi.astype(F32)).astype(BF16)
    br = row2(jnp.pad(b_router[0], (0, LANES - N_EXPERTS)))
    mixer_weights = (lng, lnb, wg, bg, w_pool[0].astype(BF16), row2(s_pool[0]),
                     w_pool_proj[0].astype(BF16), w_att_proj[0].astype(BF16), w_out[0].astype(BF16),
                     row2(ln1_g[0]), row2(ln1_b[0]), wr_hi, wr_lo, br)

    zeros_f = jnp.zeros((B, 1, LANES), F32)
    u_p, k_p, v_p, logf_p, qa_p, ka_p, va_p = _inproj(x_prompt, zeros_f, lng, lnb, wm, bm, wf, bf)
    tq = min(ATT_TQ, T)
    ya_p = _attention(qa_p, ka_p, va_p, 0, tq, min(ATT_TK, T))

    clf = jnp.pad(cache_logf[0], ((0, 0), (0, 0), (0, LANES - N_HEADS)))
    ka_c, va_c, f_tot = _cache_prep(cache_k[0].reshape(Bs, P, ATT_W), cache_v[0].reshape(Bs, P, ATT_W), clf)
    u_s, k_s, v_s, logf_s, qa_s, ka_s, va_s = _inproj(x_sample, f_tot, lng, lnb, wm, bm, wf, bf)
    pad_keys = (-(P + Ts)) % LANES
    tks = P + Ts + pad_keys
    zpad = jnp.zeros((Bs, N_HEADS, pad_keys, AUG_W), BF16)
    ya_s = _attention(qa_s, jnp.concatenate([ka_c, ka_s, zpad], axis=2),
                      jnp.concatenate([va_c, va_s, zpad], axis=2), P, Ts, tks)

    cnt0 = jnp.zeros((1, LANES), F32)
    hist_p = jnp.zeros((B, HALO, POOL_W), F32)
    n_tok = B * T + Bs * Ts
    *bufs, cnt_p = _mixer(x_prompt, u_p, hist_p, ya_p, cnt0, 0, mixer_weights, n_tok, 0, ())
    u_full_s = jnp.concatenate([cache_pool[0].astype(F32), u_s], axis=1)
    hist_s = jnp.pad(cache_pool[0].astype(F32), ((0, 0), (HALO - POOL_HIST, 0), (0, 0)))
    h_all, hp_all, ti, tw, rk, cnt = _mixer(x_sample, u_s, hist_s, ya_s, cnt_p, P, mixer_weights,
                                            n_tok, B * T, tuple(bufs))
    ti, rk = ti[:, :TOP_K], rk[:, :TOP_K]

    counts = cnt[0, :N_EXPERTS].astype(I32)
    padded = (counts + MOE_BLOCK - 1) // MOE_BLOCK * MOE_BLOCK
    pad_ends = jnp.cumsum(padded)
    pad_starts = pad_ends - padded
    n_blocks = (n_tok * TOP_K + N_EXPERTS * (MOE_BLOCK - 1) + MOE_BLOCK - 1) // MOE_BLOCK
    n_rows = n_blocks * MOE_BLOCK
    block_start = jnp.arange(n_blocks, dtype=I32) * MOE_BLOCK
    block_e = jnp.minimum(jnp.sum((pad_ends[None, :] <= block_start[:, None]).astype(I32), axis=1), N_EXPERTS - 1)
    n_used = (pad_ends[-1:] // MOE_BLOCK).astype(I32)
    dest_t = (pad_starts[ti] + rk).astype(I32).T
    dest_planes = jnp.stack([dest_t, dest_t + n_rows], axis=1)

    xb = _sc_scatter_rows(hp_all.reshape(2 * n_tok, PLANE), dest_planes.reshape(TOP_K, 2 * n_tok), 2 * n_rows)
    yb = _experts(block_e, n_used, xb.reshape(2, n_rows, PLANE),
                  w1[0].astype(BF16), b1[0][:, None, :], w2[0].astype(BF16), b2[0][:, None, :])
    y4 = _sc_gather_rows(yb.reshape(2 * n_rows, PLANE), dest_planes.reshape(1, TOP_K * 2 * n_tok))
    y4 = y4.reshape(TOP_K, 2, n_tok, PLANE)
    ln2g, ln2b = row2(ln2_g[0]), row2(ln2_b[0])
    y_prompt = _combine(h_all, tw, ln2g, ln2b, y4, 0, B * T).reshape(B, T, D)
    y_sample = _combine(h_all, tw, ln2g, ln2b, y4, B * T, Bs * Ts).reshape(Bs, Ts, D)
    heads = lambda a, b_, t_: a.reshape(1, b_, t_, N_HEADS, HEAD_DIM)
    return (y_prompt, y_sample,
            heads(k_p, B, T), heads(v_p, B, T), logf_p[None], u_p[:, -POOL_HIST:][None],
            heads(k_s, Bs, Ts), heads(v_s, Bs, Ts), logf_s[None], u_full_s[:, -POOL_HIST:][None])
```

```python
import functools

import jax
import jax.numpy as jnp
from jax import lax
from jax.experimental import pallas as pl
from jax.experimental.pallas import tpu as pltpu
from jax.experimental.pallas import tpu_sc as plsc

F32 = jnp.float32
BF16 = jnp.bfloat16
I32 = jnp.int32
U32 = jnp.uint32

D_MODEL = 1024
N_HEADS = 8
HEAD_DIM = 64
ATT_W = N_HEADS * HEAD_DIM
POOL_WINDOWS = (2, 4, 8, 16)
POOL_GC = 128
POOL_W = len(POOL_WINDOWS) * POOL_GC
POOL_HIST = max(POOL_WINDOWS) - 1
HALO = 16
N_EXPERTS = 32
TOP_K = 4
D_FF = D_MODEL
SWIGLU_ALPHA = 1.702
SWIGLU_LIMIT = 7.0
LN_EPS = 1e-5
DEPTH = 1
DEEPNORM_ALPHA = (2.0 * DEPTH) ** 0.25
ATT_SCALE = HEAD_DIM ** -0.5
LOG2E = 1.4426950408889634
MAIN_W = POOL_W + 3 * ATT_W
LANES = 128
AUG_W = LANES
HALF = D_MODEL // 2
VMEM_LIMIT = 56 * 1024 * 1024

SEQ_TILE = 512
ATT_TQ = 1024
ATT_TK = 1024
N_STATS = 4
SKIP_LOG2 = 150.0
MOE_BLOCK = 512
PLANE = HALF // 2
SC_WINDOW = 128


def _layer_norm(x, g, b):
    mu = jnp.mean(x, axis=-1, keepdims=True)
    xc = x - mu
    var = jnp.mean(xc * xc, axis=-1, keepdims=True)
    return xc * lax.rsqrt(var + LN_EPS) * g + b


def _split3(x):
    a = x.astype(BF16)
    r = x - a.astype(F32)
    b = r.astype(BF16)
    c = (r - b.astype(F32)).astype(BF16)
    return a, b, c


def _prefix_rows(x, tm, inclusive):
    kp = max(tm, LANES)
    r = lax.broadcasted_iota(I32, (tm, kp), 0)
    c = lax.broadcasted_iota(I32, (tm, kp), 1)
    tri = ((c <= r) if inclusive else (c < r)).astype(BF16)
    if kp > tm:
        x = jnp.concatenate([x, jnp.zeros((kp - tm, x.shape[1]), x.dtype)], axis=0)
    out = None
    for piece in _split3(x):
        y = jnp.dot(tri, piece, preferred_element_type=F32)
        out = y if out is None else out + y
    return out


def _pack_halves(y):
    hi = pltpu.bitcast(y[:, :HALF].astype(BF16).astype(F32), U32)
    lo = pltpu.bitcast(y[:, HALF:].astype(BF16).astype(F32), U32)
    return hi | (lo >> 16)


def _unpack_halves(w):
    hi = pltpu.bitcast(w & jnp.uint32(0xFFFF0000), F32)
    lo = pltpu.bitcast(w << 16, F32)
    return hi, lo


def _head_slab(p, off, h):
    s = p[:, off + (h // 2) * LANES: off + (h // 2) * LANES + LANES]
    return s if h % 2 == 0 else pltpu.roll(s, HEAD_DIM, 1)


def _augment(qs, ks, vs, f_col, lane):
    fc = jnp.broadcast_to(f_col, lane.shape)
    hi = fc.astype(BF16).astype(F32)
    r1 = fc - hi
    mid = r1.astype(BF16).astype(F32)
    lo = r1 - mid
    one = jnp.ones_like(fc)
    zero = jnp.zeros_like(fc)
    ka = jnp.where(lane < 64, ks, jnp.where(lane < 67, one, jnp.where(
        lane == 67, -hi, jnp.where(lane == 68, -mid, jnp.where(lane == 69, -lo, zero)))))
    va = jnp.where(lane < 64, vs, jnp.where(lane == 64, one, zero))
    if qs is None:
        return None, ka.astype(BF16), va.astype(BF16)
    qa = jnp.where(lane < 64, qs, jnp.where(lane == 64, hi, jnp.where(
        lane == 65, mid, jnp.where(lane == 66, lo, jnp.where(lane < 70, one, zero)))))
    return qa.astype(BF16), ka.astype(BF16), va.astype(BF16)


def _inproj_kernel(x_ref, f0_ref, lng_ref, lnb_ref, wm_ref, bm_ref, wf_ref, bf_ref,
                   u_ref, k_ref, v_ref, logf_ref, qa_ref, ka_ref, va_ref, st_ref, carry_ref, *, tm):
    t = pl.program_id(1)

    @pl.when(t == 0)
    def _():
        carry_ref[...] = f0_ref[...]

    xn = _layer_norm(x_ref[...], lng_ref[...], lnb_ref[...]).astype(BF16)
    p = jnp.dot(xn, wm_ref[...], preferred_element_type=F32) + bm_ref[...]
    fl = jnp.dot(xn, wf_ref[...], preferred_element_type=F32) + bf_ref[...]
    logf = jnp.minimum(fl, 0.0) - jnp.log(1.0 + jnp.exp(-jnp.abs(fl)))
    f_cum = _prefix_rows(logf, tm, True) + carry_ref[...]
    carry_ref[...] = f_cum[tm - 1:tm, :]

    u_ref[...] = p[:, :POOL_W]
    k_ref[...] = p[:, POOL_W + ATT_W:POOL_W + 2 * ATT_W]
    v_ref[...] = p[:, POOL_W + 2 * ATT_W:]
    logf_ref[...] = logf[:, :N_HEADS]

    lane = lax.broadcasted_iota(I32, (tm, AUG_W), 1)
    for h in range(N_HEADS):
        qs = _head_slab(p, POOL_W, h) * (ATT_SCALE * LOG2E)
        ks = _head_slab(p, POOL_W + ATT_W, h)
        vs = _head_slab(p, POOL_W + 2 * ATT_W, h)
        qa, ka, va = _augment(qs, ks, vs, f_cum[:, h:h + 1] * LOG2E, lane)
        qa_ref[h] = qa
        ka_ref[h] = ka
        va_ref[h] = va

    col = lax.broadcasted_iota(I32, (ATT_W, LANES), 0) // HEAD_DIM
    head_sum = (col == lax.broadcasted_iota(I32, (ATT_W, LANES), 1)).astype(BF16)
    qsec = (p[:, POOL_W:POOL_W + ATT_W] * (ATT_SCALE * LOG2E)).astype(BF16).astype(F32)
    ksec = p[:, POOL_W + ATT_W:POOL_W + 2 * ATT_W].astype(BF16).astype(F32)
    for r, sec in enumerate((qsec, ksec)):
        sq = jnp.dot((sec * sec).astype(BF16), head_sum, preferred_element_type=F32)
        st_ref[r:r + 1, :] = jnp.max(sq, axis=0, keepdims=True)
    st_ref[2:3, :] = f_cum[0:1, :] * LOG2E
    st_ref[3:4, :] = f_cum[tm - 1:tm, :] * LOG2E


def _inproj(x, f0, lng, lnb, wm, bm, wf, bf):
    B, T, D = x.shape
    tm = min(SEQ_TILE, T)
    grid = (B, T // tm)
    const = lambda b, t: (0, 0)
    row = lambda b, t: (b, t, 0)
    aug = pl.BlockSpec((None, N_HEADS, tm, AUG_W), lambda b, t: (b, 0, t, 0))
    aug_shape = jax.ShapeDtypeStruct((B, N_HEADS, T, AUG_W), BF16)
    return pl.pallas_call(
        functools.partial(_inproj_kernel, tm=tm),
        grid=grid,
        in_specs=[
            pl.BlockSpec((None, tm, D), row),
            pl.BlockSpec((None, 1, LANES), lambda b, t: (b, 0, 0)),
            pl.BlockSpec((1, D), const), pl.BlockSpec((1, D), const),
            pl.BlockSpec((D, MAIN_W), const), pl.BlockSpec((1, MAIN_W), const),
            pl.BlockSpec((D, LANES), const), pl.BlockSpec((1, LANES), const),
        ],
        out_specs=[
            pl.BlockSpec((None, tm, POOL_W), row),
            pl.BlockSpec((None, tm, ATT_W), row),
            pl.BlockSpec((None, tm, ATT_W), row),
            pl.BlockSpec((None, tm, N_HEADS), row),
            aug, aug, aug,
            pl.BlockSpec((None, None, N_STATS, LANES), lambda b, t: (b, t, 0, 0)),
        ],
        out_shape=[
            jax.ShapeDtypeStruct((B, T, POOL_W), F32),
            jax.ShapeDtypeStruct((B, T, ATT_W), F32),
            jax.ShapeDtypeStruct((B, T, ATT_W), F32),
            jax.ShapeDtypeStruct((B, T, N_HEADS), F32),
            aug_shape, aug_shape, aug_shape,
            jax.ShapeDtypeStruct((B, T // tm, N_STATS, LANES), F32),
        ],
        scratch_shapes=[pltpu.VMEM((1, LANES), F32)],
        compiler_params=pltpu.CompilerParams(
            dimension_semantics=("arbitrary", "arbitrary"), vmem_limit_bytes=VMEM_LIMIT),
    )(x, f0, lng, lnb, wm, bm, wf, bf)


def _cache_kernel(ck_ref, cv_ref, clf_ref, ka_ref, va_ref, ftot_ref, carry_ref, *, tp):
    t = pl.program_id(1)

    @pl.when(t == 0)
    def _():
        carry_ref[...] = jnp.zeros_like(carry_ref)

    f_cum = _prefix_rows(clf_ref[...], tp, True) + carry_ref[...]
    carry_ref[...] = f_cum[tp - 1:tp, :]
    ftot_ref[...] = f_cum[tp - 1:tp, :]
    ck = ck_ref[...]
    cv = cv_ref[...]
    lane = lax.broadcasted_iota(I32, (tp, AUG_W), 1)
    for h in range(N_HEADS):
        _, ka, va = _augment(None, _head_slab(ck, 0, h), _head_slab(cv, 0, h), f_cum[:, h:h + 1] * LOG2E, lane)
        ka_ref[h] = ka
        va_ref[h] = va


def _cache_prep(ck, cv, clf):
    B, P, _ = ck.shape
    tp = min(SEQ_TILE, P)
    row = lambda b, t: (b, t, 0)
    aug = pl.BlockSpec((None, N_HEADS, tp, AUG_W), lambda b, t: (b, 0, t, 0))
    aug_shape = jax.ShapeDtypeStruct((B, N_HEADS, P, AUG_W), BF16)
    return pl.pallas_call(
        functools.partial(_cache_kernel, tp=tp),
        grid=(B, P // tp),
        in_specs=[pl.BlockSpec((None, tp, ATT_W), row), pl.BlockSpec((None, tp, ATT_W), row),
                  pl.BlockSpec((None, tp, LANES), row)],
        out_specs=[aug, aug, pl.BlockSpec((None, 1, LANES), lambda b, t: (b, 0, 0))],
        out_shape=[aug_shape, aug_shape, jax.ShapeDtypeStruct((B, 1, LANES), F32)],
        scratch_shapes=[pltpu.VMEM((1, LANES), F32)],
        compiler_params=pltpu.CompilerParams(
            dimension_semantics=("arbitrary", "arbitrary"), vmem_limit_bytes=VMEM_LIMIT),
    )(ck, cv, clf)


def _attn_kernel(jmin_ref, qa_ref, ka_ref, va_ref, o_ref, *, tq, tk, q_off):
    b, hp, iq = pl.program_id(0), pl.program_id(1), pl.program_id(2)
    q_lo = q_off + iq * tq
    n_full = (q_lo + 1) // tk
    n_all = (q_lo + tq - 1) // tk + 1
    j_first = jmin_ref[(b * pl.num_programs(1) + hp) * pl.num_programs(2) + iq]
    lane = lax.broadcasted_iota(I32, (tq, AUG_W), 1)
    qs = (qa_ref[0], qa_ref[1])

    def step(j, carry, masked):
        k0 = pl.multiple_of(j * tk, tk)
        new = []
        for hh in range(2):
            m, acc = carry[hh]
            k = ka_ref[hh, pl.ds(k0, tk), :]
            v = va_ref[hh, pl.ds(k0, tk), :]
            s = lax.dot_general(qs[hh], k, (((1,), (1,)), ((), ())), preferred_element_type=F32)
            if masked:
                qpos = q_lo + lax.broadcasted_iota(I32, (tq, tk), 0)
                kpos = k0 + lax.broadcasted_iota(I32, (tq, tk), 1)
                s = jnp.where(qpos >= kpos, s, -jnp.inf)
            m_new = jnp.maximum(m, jnp.max(s, axis=-1, keepdims=True))
            alpha = jnp.exp2(m - m_new)
            p = jnp.exp2(s - m_new)
            acc = acc * alpha + jnp.dot(p.astype(BF16), v, preferred_element_type=F32)
            new.append((m_new, acc))
        return tuple(new)

    init = (jnp.full((tq, 1), -jnp.inf, F32), jnp.zeros((tq, AUG_W), F32))
    carry = lax.fori_loop(j_first, n_full, functools.partial(step, masked=False), (init, init))
    carry = lax.fori_loop(n_full, n_all, functools.partial(step, masked=True), carry)
    outs = [acc / acc[:, HEAD_DIM:HEAD_DIM + 1] for _, acc in carry]
    o_ref[...] = jnp.where(lane < HEAD_DIM, outs[0], pltpu.roll(outs[1], HEAD_DIM, 1)).astype(BF16)


def _attention(jmin, qa, ka, va, q_off, tq, tk):
    B, H, Tq, _ = qa.shape
    Tk = ka.shape[2]
    return pl.pallas_call(
        functools.partial(_attn_kernel, tq=tq, tk=tk, q_off=q_off),
        grid_spec=pltpu.PrefetchScalarGridSpec(
            num_scalar_prefetch=1,
            grid=(B, H // 2, Tq // tq),
            in_specs=[
                pl.BlockSpec((None, 2, tq, AUG_W), lambda b, hp, iq, jm: (b, hp, iq, 0)),
                pl.BlockSpec((None, 2, Tk, AUG_W), lambda b, hp, iq, jm: (b, hp, 0, 0)),
                pl.BlockSpec((None, 2, Tk, AUG_W), lambda b, hp, iq, jm: (b, hp, 0, 0)),
            ],
            out_specs=pl.BlockSpec((None, tq, 2 * HEAD_DIM), lambda b, hp, iq, jm: (b, iq, hp)),
        ),
        out_shape=jax.ShapeDtypeStruct((B, Tq, ATT_W), BF16),
        compiler_params=pltpu.CompilerParams(
            dimension_semantics=("arbitrary", "arbitrary", "arbitrary"), vmem_limit_bytes=VMEM_LIMIT),
    )(jmin, qa, ka, va)


def _attn_skip_plan(stats, tq, tk, tm):
    B, nt = stats.shape[:2]
    st = stats[..., :N_HEADS]
    per = lambda row, r: st[:, :, row].reshape(B, nt // r, r, N_HEADS)
    rq, rk = tq // tm, tk // tm
    nq, nk = nt // rq, nt // rk
    qn2, kn2, kn2_own = per(0, rq).max(2), per(1, rk).max(2), per(1, rq).max(2)
    f_first, f_last = per(2, rq)[:, :, 0], per(3, rk)[:, :, -1]
    slack = 1.01
    upper = jnp.sqrt(qn2[:, :, None] * kn2[:, None, :]) * slack + (f_first[:, :, None] - f_last[:, None, :])
    lower = -jnp.sqrt(qn2 * kn2_own) * slack
    weightless = upper - lower[:, :, None] <= -(SKIP_LOG2 + 2.0)
    j = jnp.arange(nk, dtype=I32)[None, None, :, None]
    n_full = ((jnp.arange(nq, dtype=I32) * tq + 1) // tk)[None, :, None, None]
    first = jnp.min(jnp.where(weightless | (j >= n_full), n_full, j), axis=2)
    first = jnp.min(first.reshape(B, nq, N_HEADS // 2, 2), axis=3)
    return jnp.swapaxes(first, 1, 2).reshape(-1).astype(I32)


def _mixer_kernel(x_ref, u_ref, uprev_ref, hist_ref, ya_ref, cnt0_ref,
                  lng_ref, lnb_ref, wg_ref, bg_ref, wpool_ref, spool_ref, wpp_ref, watt_ref, wout_ref,
                  ln1g_ref, ln1b_ref, wrh_ref, wrl_ref, br_ref, *rest, tm, start_pos):
    h_ref, hp_ref, ti_ref, tw_ref, rk_ref, cnt_ref, uext_ref = rest[-7:]
    b = pl.program_id(0)
    t = pl.program_id(1)

    @pl.when((b == 0) & (t == 0))
    def _():
        cnt_ref[...] = cnt0_ref[...]

    xn = _layer_norm(x_ref[...], lng_ref[...], lnb_ref[...])
    g = jnp.dot(xn.astype(BF16), wg_ref[...], preferred_element_type=F32) + bg_ref[...]
    gates = jax.nn.sigmoid(g)

    @pl.when(t == 0)
    def _():
        uext_ref[0:HALO, :] = hist_ref[...]

    @pl.when(t > 0)
    def _():
        uext_ref[0:HALO, :] = uprev_ref[...]

    uext_ref[HALO:HALO + tm, :] = u_ref[...]
    pos = start_pos + t * tm + lax.broadcasted_iota(I32, (tm, 1), 0)
    pooled = None
    for gi, w in enumerate(POOL_WINDOWS):
        sl = slice(gi * POOL_GC, (gi + 1) * POOL_GC)
        cur = uext_ref[HALO:HALO + tm, sl]
        s = cur
        for j in range(1, w):
            s = s + uext_ref[HALO - j:HALO - j + tm, sl]
        count = jnp.minimum(pos + 1, w).astype(F32)
        d = s / count - cur
        yg = jnp.dot(d.astype(BF16), wpool_ref[gi], preferred_element_type=F32) * spool_ref[:, sl]
        contrib = jnp.dot(yg.astype(BF16), wpp_ref[sl, :], preferred_element_type=F32)
        pooled = contrib if pooled is None else pooled + contrib
    att = jnp.dot(ya_ref[...], watt_ref[...], preferred_element_type=F32)
    m = gates[:, :D_MODEL] * pooled + gates[:, D_MODEL:] * att
    mix = jnp.dot(m.astype(BF16), wout_ref[...], preferred_element_type=F32)
    h = _layer_norm(DEEPNORM_ALPHA * xn + mix, ln1g_ref[...], ln1b_ref[...])
    h_ref[...] = h
    packed = _pack_halves(h)
    hp_ref[0] = packed[:, :PLANE]
    hp_ref[1] = packed[:, PLANE:]

    hh = h.astype(BF16)
    hl = (h - hh.astype(F32)).astype(BF16)
    logits = (jnp.dot(hh, wrh_ref[...], preferred_element_type=F32)
              + jnp.dot(hl, wrh_ref[...], preferred_element_type=F32)
              + jnp.dot(hh, wrl_ref[...], preferred_element_type=F32) + br_ref[...])
    lane = lax.broadcasted_iota(I32, (tm, LANES), 1)
    work = jnp.where(lane < N_EXPERTS, logits, -jnp.inf)
    vals, idxs = [], []
    for _ in range(TOP_K):
        mx = jnp.max(work, axis=-1, keepdims=True)
        idx = jnp.min(jnp.where(work == mx, lane, LANES), axis=-1, keepdims=True)
        vals.append(mx)
        idxs.append(idx)
        work = jnp.where(lane == idx, -jnp.inf, work)
    exps = [jnp.exp(v - vals[0]) for v in vals]
    denom = exps[0] + exps[1] + exps[2] + exps[3]

    onehot = jnp.zeros((tm, LANES), F32)
    for idx in idxs:
        onehot = onehot + (lane == idx).astype(F32)
    base = _prefix_rows(onehot, tm, False) + cnt_ref[...]
    ti = jnp.zeros((tm, LANES), I32)
    tw = jnp.zeros((tm, LANES), F32)
    rk = jnp.zeros((tm, LANES), F32)
    for k in range(TOP_K):
        rank = jnp.sum(jnp.where(lane == idxs[k], base, 0.0), axis=-1, keepdims=True)
        ti = jnp.where(lane == k, idxs[k], ti)
        tw = jnp.where(lane == k, exps[k] / denom, tw)
        rk = jnp.where(lane == k, rank, rk)
    ti_ref[...] = ti
    tw_ref[...] = tw
    rk_ref[...] = rk.astype(I32)
    cnt_ref[...] = cnt_ref[...] + jnp.sum(onehot, axis=0, keepdims=True)


def _mixer(x, u, hist, yatt, cnt0, start_pos, weights, n_tok, row_off, prev):
    B, T, D = x.shape
    tm = min(SEQ_TILE, T)
    nt = T // tm
    row = lambda b, t: (b, t, 0)
    tok = lambda b, t: (row_off // tm + b * nt + t, 0)
    hpt = tm // HALO
    full = lambda a: pl.BlockSpec(a.shape, lambda b, t, _n=a.ndim: (0,) * _n)
    tok3 = lambda b, t: (0, row_off // tm + b * nt + t, 0)
    out_specs = [pl.BlockSpec((tm, D), tok), pl.BlockSpec((2, tm, PLANE), tok3)] + [
        pl.BlockSpec((tm, LANES), tok) for _ in range(3)] + [pl.BlockSpec((1, LANES), lambda b, t: (0, 0))]
    out_shape = [jax.ShapeDtypeStruct((n_tok, D), F32), jax.ShapeDtypeStruct((2, n_tok, PLANE), U32),
                 jax.ShapeDtypeStruct((n_tok, LANES), I32), jax.ShapeDtypeStruct((n_tok, LANES), F32),
                 jax.ShapeDtypeStruct((n_tok, LANES), I32), jax.ShapeDtypeStruct((1, LANES), F32)]
    n_in = 6 + len(weights)
    return pl.pallas_call(
        functools.partial(_mixer_kernel, tm=tm, start_pos=start_pos),
        grid=(B, nt),
        in_specs=[
            pl.BlockSpec((None, tm, D), row),
            pl.BlockSpec((None, tm, POOL_W), row),
            pl.BlockSpec((None, HALO, POOL_W), lambda b, t: (b, jnp.maximum(t * hpt - 1, 0), 0)),
            pl.BlockSpec((None, HALO, POOL_W), lambda b, t: (b, 0, 0)),
            pl.BlockSpec((None, tm, ATT_W), row),
            pl.BlockSpec((1, LANES), lambda b, t: (0, 0)),
        ] + [full(w) for w in weights] + [pl.BlockSpec(memory_space=pl.ANY) for _ in prev],
        out_specs=out_specs,
        out_shape=out_shape,
        scratch_shapes=[pltpu.VMEM((HALO + tm, POOL_W), F32)],
        input_output_aliases={n_in + i: i for i in range(len(prev))},
        compiler_params=pltpu.CompilerParams(
            dimension_semantics=("arbitrary", "arbitrary"), vmem_limit_bytes=VMEM_LIMIT),
    )(x, u, u, hist, yatt, cnt0, *weights, *prev)


def _sc_mesh():
    return plsc.VectorSubcoreMesh(core_axis_name="c", subcore_axis_name="s")


def _sc_scatter_rows(x, idx, n_rows):
    n = x.shape[0]

    @pl.kernel(out_type=jax.ShapeDtypeStruct((n_rows, PLANE), x.dtype), mesh=_sc_mesh(), scratch_types=[])
    def scatter(x_hbm, i_hbm, o_hbm):
        def body(x_vmem, i_vmem):
            for k in range(TOP_K):
                pltpu.sync_copy(x_vmem, o_hbm.at[i_vmem.at[k]])

        pltpu.emit_pipeline(
            body,
            grid=(n // SC_WINDOW,),
            in_specs=[pl.BlockSpec((SC_WINDOW, PLANE), index_map=lambda i: (i, 0)),
                      pl.BlockSpec((TOP_K, SC_WINDOW), index_map=lambda i: (0, i))],
            out_specs=[],
            core_axis_name=("c", "s"),
            dimension_semantics=(pltpu.PARALLEL,),
        )(x_hbm, i_hbm)

    return scatter(x, idx)


def _sc_gather_rows(y, idx):
    n = idx.shape[1]

    @pl.kernel(out_type=jax.ShapeDtypeStruct((n, PLANE), y.dtype), mesh=_sc_mesh(), scratch_types=[])
    def gather(y_hbm, i_hbm, o_hbm):
        def body(i_vmem, o_vmem):
            pltpu.sync_copy(y_hbm.at[i_vmem.at[0]], o_vmem)

        pltpu.emit_pipeline(
            body,
            grid=(n // SC_WINDOW,),
            in_specs=[pl.BlockSpec((1, SC_WINDOW), index_map=lambda i: (0, i))],
            out_specs=[pl.BlockSpec((SC_WINDOW, PLANE), index_map=lambda i: (i, 0))],
            core_axis_name=("c", "s"),
            dimension_semantics=(pltpu.PARALLEL,),
        )(i_hbm, o_hbm)

    return gather(y, idx)


def _join_planes(ref):
    return jnp.concatenate([ref[0], ref[1]], axis=1)


def _expert_kernel(be_ref, nb_ref, first_ref, xb_ref, w1_ref, b1_ref, w2_ref, b2_ref, yb_ref, w1b_ref, w2b_ref):
    del be_ref
    i = pl.program_id(0)

    @pl.when(first_ref[i] == 1)
    def _():
        w1b_ref[...] = w1_ref[...].astype(BF16)
        w2b_ref[...] = w2_ref[...].astype(BF16)

    @pl.when(i < nb_ref[0])
    def _():
        xa, xc = _unpack_halves(_join_planes(xb_ref))
        hfull = (jnp.dot(xa.astype(BF16), w1b_ref[:HALF, :], preferred_element_type=F32)
                 + jnp.dot(xc.astype(BF16), w1b_ref[HALF:, :], preferred_element_type=F32) + b1_ref[...])
        glu = jnp.minimum(hfull[:, :D_FF], SWIGLU_LIMIT)
        lin = jnp.clip(hfull[:, D_FF:], -SWIGLU_LIMIT, SWIGLU_LIMIT)
        a = glu * jax.nn.sigmoid(SWIGLU_ALPHA * glu) * (lin + 1.0)
        y = jnp.dot(a.astype(BF16), w2b_ref[...], preferred_element_type=F32) + b2_ref[...]
        packed = _pack_halves(y)
        yb_ref[0] = packed[:, :PLANE]
        yb_ref[1] = packed[:, PLANE:]

    @pl.when(i >= nb_ref[0])
    def _():
        yb_ref[...] = jnp.zeros_like(yb_ref)


def _experts(block_e, n_used, first, xb, w1, b1, w2, b2):
    n_rows = xb.shape[1]
    bm = MOE_BLOCK
    rows = pl.BlockSpec((2, bm, PLANE), lambda i, be, nb, fi: (0, i, 0))
    per_expert = lambda r, c: pl.BlockSpec((None, r, c), lambda i, be, nb, fi: (be[i], 0, 0))
    return pl.pallas_call(
        _expert_kernel,
        grid_spec=pltpu.PrefetchScalarGridSpec(
            num_scalar_prefetch=3,
            grid=(n_rows // bm,),
            in_specs=[rows, per_expert(D_MODEL, 2 * D_FF), per_expert(1, 2 * D_FF),
                      per_expert(D_FF, D_MODEL), per_expert(1, D_MODEL)],
            out_specs=rows,
            scratch_shapes=[pltpu.VMEM((D_MODEL, 2 * D_FF), BF16), pltpu.VMEM((D_FF, D_MODEL), BF16)],
        ),
        out_shape=jax.ShapeDtypeStruct((2, n_rows, PLANE), U32),
        compiler_params=pltpu.CompilerParams(
            dimension_semantics=("arbitrary",), vmem_limit_bytes=VMEM_LIMIT),
    )(block_e, n_used, first, xb, w1, b1, w2, b2)


def _combine_kernel(h_ref, tw_ref, g_ref, b_ref, y4_ref, out_ref):
    tw = tw_ref[...]
    acc_hi = None
    acc_lo = None
    for k in range(TOP_K):
        hi, lo = _unpack_halves(_join_planes(y4_ref.at[k]))
        w = tw[:, k:k + 1]
        acc_hi = w * hi if acc_hi is None else acc_hi + w * hi
        acc_lo = w * lo if acc_lo is None else acc_lo + w * lo
    moe = jnp.concatenate([acc_hi, acc_lo], axis=1)
    out_ref[...] = _layer_norm(DEEPNORM_ALPHA * h_ref[...] + moe, g_ref[...], b_ref[...])


def _combine(h, tw, ln2g, ln2b, y4, row_off, n_rows):
    D = h.shape[1]
    tn = min(SEQ_TILE, n_rows)
    off = row_off // tn
    const = lambda i: (0, 0)
    return pl.pallas_call(
        _combine_kernel,
        grid=(n_rows // tn,),
        in_specs=[
            pl.BlockSpec((tn, D), lambda i: (off + i, 0)),
            pl.BlockSpec((tn, LANES), lambda i: (off + i, 0)),
            pl.BlockSpec((1, D), const), pl.BlockSpec((1, D), const),
            pl.BlockSpec((TOP_K, 2, tn, PLANE), lambda i: (0, 0, off + i, 0)),
        ],
        out_specs=pl.BlockSpec((tn, D), lambda i: (i, 0)),
        out_shape=jax.ShapeDtypeStruct((n_rows, D), F32),
        compiler_params=pltpu.CompilerParams(
            dimension_semantics=("arbitrary",), vmem_limit_bytes=VMEM_LIMIT),
    )(h, tw, ln2g, ln2b, y4)


def kernel(x_prompt, x_sample, cache_pool, cache_k, cache_v, cache_logf, ln_in_g, ln_in_b, w_in, b_in,
           w_pool, s_pool, w_pool_proj, w_att_proj, w_out, ln1_g, ln1_b, w_router, b_router,
           w1, b1, w2, b2, ln2_g, ln2_b):
    assert w_in.shape[0] == DEPTH
    B, T, D = x_prompt.shape
    Bs, Ts, _ = x_sample.shape
    P = cache_k.shape[2]
    row2 = lambda a: a.reshape(1, -1).astype(F32)

    f_off = MAIN_W
    g_off = MAIN_W + N_HEADS
    wm = w_in[0][:, :f_off].astype(BF16)
    bm = row2(b_in[0][:f_off])
    wf = jnp.pad(w_in[0][:, f_off:g_off], ((0, 0), (0, LANES - N_HEADS))).astype(BF16)
    bf = row2(jnp.pad(b_in[0][f_off:g_off], (0, LANES - N_HEADS)))
    wg = w_in[0][:, g_off:].astype(BF16)
    bg = row2(b_in[0][g_off:])
    lng, lnb = row2(ln_in_g), row2(ln_in_b)
    wr = jnp.pad(w_router[0], ((0, 0), (0, LANES - N_EXPERTS)))
    wr_hi = wr.astype(BF16)
    wr_lo = (wr - wr_hi.astype(F32)).astype(BF16)
    br = row2(jnp.pad(b_router[0], (0, LANES - N_EXPERTS)))
    mixer_weights = (lng, lnb, wg, bg, w_pool[0].astype(BF16), row2(s_pool[0]),
                     w_pool_proj[0].astype(BF16), w_att_proj[0].astype(BF16), w_out[0].astype(BF16),
                     row2(ln1_g[0]), row2(ln1_b[0]), wr_hi, wr_lo, br)

    zeros_f = jnp.zeros((B, 1, LANES), F32)
    u_p, k_p, v_p, logf_p, qa_p, ka_p, va_p, stats = _inproj(x_prompt, zeros_f, lng, lnb, wm, bm, wf, bf)
    tq, tk = min(ATT_TQ, T), min(ATT_TK, T)
    ya_p = _attention(_attn_skip_plan(stats, tq, tk, min(SEQ_TILE, T)), qa_p, ka_p, va_p, 0, tq, tk)

    clf = jnp.pad(cache_logf[0], ((0, 0), (0, 0), (0, LANES - N_HEADS)))
    ka_c, va_c, f_tot = _cache_prep(cache_k[0].reshape(Bs, P, ATT_W), cache_v[0].reshape(Bs, P, ATT_W), clf)
    u_s, k_s, v_s, logf_s, qa_s, ka_s, va_s, _ = _inproj(x_sample, f_tot, lng, lnb, wm, bm, wf, bf)
    pad_keys = (-(P + Ts)) % LANES
    tks = P + Ts + pad_keys
    zpad = jnp.zeros((Bs, N_HEADS, pad_keys, AUG_W), BF16)
    visit_all = jnp.zeros((Bs * (N_HEADS // 2),), I32)
    ya_s = _attention(visit_all, qa_s, jnp.concatenate([ka_c, ka_s, zpad], axis=2),
                      jnp.concatenate([va_c, va_s, zpad], axis=2), P, Ts, tks)

    cnt0 = jnp.zeros((1, LANES), F32)
    hist_p = jnp.zeros((B, HALO, POOL_W), F32)
    n_tok = B * T + Bs * Ts
    *bufs, cnt_p = _mixer(x_prompt, u_p, hist_p, ya_p, cnt0, 0, mixer_weights, n_tok, 0, ())
    u_full_s = jnp.concatenate([cache_pool[0].astype(F32), u_s], axis=1)
    hist_s = jnp.pad(cache_pool[0].astype(F32), ((0, 0), (HALO - POOL_HIST, 0), (0, 0)))
    h_all, hp_all, ti, tw, rk, cnt = _mixer(x_sample, u_s, hist_s, ya_s, cnt_p, P, mixer_weights,
                                            n_tok, B * T, tuple(bufs))
    ti, rk = ti[:, :TOP_K], rk[:, :TOP_K]

    counts = cnt[0, :N_EXPERTS].astype(I32)
    padded = (counts + MOE_BLOCK - 1) // MOE_BLOCK * MOE_BLOCK
    pad_ends = jnp.cumsum(padded)
    pad_starts = pad_ends - padded
    n_blocks = (n_tok * TOP_K + N_EXPERTS * (MOE_BLOCK - 1) + MOE_BLOCK - 1) // MOE_BLOCK
    n_rows = n_blocks * MOE_BLOCK
    block_start = jnp.arange(n_blocks, dtype=I32) * MOE_BLOCK
    block_e = jnp.minimum(jnp.sum((pad_ends[None, :] <= block_start[:, None]).astype(I32), axis=1), N_EXPERTS - 1)
    n_used = (pad_ends[-1:] // MOE_BLOCK).astype(I32)
    dest_t = (pad_starts[ti] + rk).astype(I32).T
    dest_planes = jnp.stack([dest_t, dest_t + n_rows], axis=1)

    xb = _sc_scatter_rows(hp_all.reshape(2 * n_tok, PLANE), dest_planes.reshape(TOP_K, 2 * n_tok), 2 * n_rows)
    first = jnp.concatenate([jnp.ones((1,), I32), (block_e[1:] != block_e[:-1]).astype(I32)])
    yb = _experts(block_e, n_used, first, xb.reshape(2, n_rows, PLANE),
                  w1[0], b1[0][:, None, :], w2[0], b2[0][:, None, :])
    y4 = _sc_gather_rows(yb.reshape(2 * n_rows, PLANE), dest_planes.reshape(1, TOP_K * 2 * n_tok))
    y4 = y4.reshape(TOP_K, 2, n_tok, PLANE)
    ln2g, ln2b = row2(ln2_g[0]), row2(ln2_b[0])
    y_prompt = _combine(h_all, tw, ln2g, ln2b, y4, 0, B * T).reshape(B, T, D)
    y_sample = _combine(h_all, tw, ln2g, ln2b, y4, B * T, Bs * Ts).reshape(Bs, Ts, D)
    heads = lambda a, b_, t_: a.reshape(1, b_, t_, N_HEADS, HEAD_DIM)
    return (y_prompt, y_sample,
            heads(k_p, B, T), heads(v_p, B, T), logf_p[None], u_p[:, -POOL_HIST:][None],
            heads(k_s, Bs, Ts), heads(v_s, Bs, Ts), logf_s[None], u_full_s[:, -POOL_HIST:][None])
```

```python
import functools

import jax
import jax.numpy as jnp
from jax import lax
from jax.experimental import pallas as pl
from jax.experimental.pallas import tpu as pltpu
from jax.experimental.pallas import tpu_sc as plsc

F32 = jnp.float32
BF16 = jnp.bfloat16
I32 = jnp.int32
U32 = jnp.uint32

D_MODEL = 1024
N_HEADS = 8
HEAD_DIM = 64
ATT_W = N_HEADS * HEAD_DIM
POOL_WINDOWS = (2, 4, 8, 16)
POOL_GC = 128
POOL_W = len(POOL_WINDOWS) * POOL_GC
POOL_HIST = max(POOL_WINDOWS) - 1
HALO = 16
N_EXPERTS = 32
TOP_K = 4
D_FF = D_MODEL
SWIGLU_ALPHA = 1.702
SWIGLU_LIMIT = 7.0
LN_EPS = 1e-5
DEPTH = 1
DEEPNORM_ALPHA = (2.0 * DEPTH) ** 0.25
ATT_SCALE = HEAD_DIM ** -0.5
LOG2E = 1.4426950408889634
MAIN_W = POOL_W + 3 * ATT_W
LANES = 128
AUG_W = LANES
HALF = D_MODEL // 2
VMEM_LIMIT = 56 * 1024 * 1024

SEQ_TILE = 512
ATT_TQ = 1024
ATT_TK = 1024
N_STATS = 4
SKIP_LOG2 = 150.0
MOE_BLOCK = 512
COMBINE_CHUNKS = 4
RANK_SPAN = 1 << 20
CODE_ROWS = 8
PLANE = HALF // 2
SC_WINDOW = 128


def _layer_norm(x, g, b):
    mu = jnp.mean(x, axis=-1, keepdims=True)
    xc = x - mu
    var = jnp.mean(xc * xc, axis=-1, keepdims=True)
    return xc * lax.rsqrt(var + LN_EPS) * g + b


def _split3(x):
    a = x.astype(BF16)
    r = x - a.astype(F32)
    b = r.astype(BF16)
    c = (r - b.astype(F32)).astype(BF16)
    return a, b, c


def _prefix_rows(x, tm, inclusive):
    kp = max(tm, LANES)
    r = lax.broadcasted_iota(I32, (tm, kp), 0)
    c = lax.broadcasted_iota(I32, (tm, kp), 1)
    tri = ((c <= r) if inclusive else (c < r)).astype(BF16)
    if kp > tm:
        x = jnp.concatenate([x, jnp.zeros((kp - tm, x.shape[1]), x.dtype)], axis=0)
    out = None
    for piece in _split3(x):
        y = jnp.dot(tri, piece, preferred_element_type=F32)
        out = y if out is None else out + y
    return out


def _pack_halves(y):
    hi = pltpu.bitcast(y[:, :HALF].astype(BF16).astype(F32), U32)
    lo = pltpu.bitcast(y[:, HALF:].astype(BF16).astype(F32), U32)
    return hi | (lo >> 16)


def _unpack_halves(w):
    hi = pltpu.bitcast(w & jnp.uint32(0xFFFF0000), F32)
    lo = pltpu.bitcast(w << 16, F32)
    return hi, lo


def _head_slab(p, off, h):
    s = p[:, off + (h // 2) * LANES: off + (h // 2) * LANES + LANES]
    return s if h % 2 == 0 else pltpu.roll(s, HEAD_DIM, 1)


def _augment(qs, ks, vs, f_col, lane):
    fc = jnp.broadcast_to(f_col, lane.shape)
    hi = fc.astype(BF16).astype(F32)
    r1 = fc - hi
    mid = r1.astype(BF16).astype(F32)
    lo = r1 - mid
    one = jnp.ones_like(fc)
    zero = jnp.zeros_like(fc)
    ka = jnp.where(lane < 64, ks, jnp.where(lane < 67, one, jnp.where(
        lane == 67, -hi, jnp.where(lane == 68, -mid, jnp.where(lane == 69, -lo, zero)))))
    va = jnp.where(lane < 64, vs, jnp.where(lane == 64, one, zero))
    if qs is None:
        return None, ka.astype(BF16), va.astype(BF16)
    qa = jnp.where(lane < 64, qs, jnp.where(lane == 64, hi, jnp.where(
        lane == 65, mid, jnp.where(lane == 66, lo, jnp.where(lane < 70, one, zero)))))
    return qa.astype(BF16), ka.astype(BF16), va.astype(BF16)


def _inproj_kernel(x_ref, f0_ref, lng_ref, lnb_ref, wm_ref, bm_ref, wf_ref, bf_ref,
                   u_ref, k_ref, v_ref, logf_ref, qa_ref, ka_ref, va_ref, st_ref, carry_ref, *, tm):
    t = pl.program_id(1)

    @pl.when(t == 0)
    def _():
        carry_ref[...] = f0_ref[...]

    xn = _layer_norm(x_ref[...], lng_ref[...], lnb_ref[...]).astype(BF16)
    p = jnp.dot(xn, wm_ref[...], preferred_element_type=F32) + bm_ref[...]
    fl = jnp.dot(xn, wf_ref[...], preferred_element_type=F32) + bf_ref[...]
    logf = jnp.minimum(fl, 0.0) - jnp.log(1.0 + jnp.exp(-jnp.abs(fl)))
    f_cum = _prefix_rows(logf, tm, True) + carry_ref[...]
    carry_ref[...] = f_cum[tm - 1:tm, :]

    u_ref[...] = p[:, :POOL_W]
    k_ref[...] = p[:, POOL_W + ATT_W:POOL_W + 2 * ATT_W]
    v_ref[...] = p[:, POOL_W + 2 * ATT_W:]
    logf_ref[...] = logf[:, :N_HEADS]

    lane = lax.broadcasted_iota(I32, (tm, AUG_W), 1)
    for h in range(N_HEADS):
        qs = _head_slab(p, POOL_W, h) * (ATT_SCALE * LOG2E)
        ks = _head_slab(p, POOL_W + ATT_W, h)
        vs = _head_slab(p, POOL_W + 2 * ATT_W, h)
        qa, ka, va = _augment(qs, ks, vs, f_cum[:, h:h + 1] * LOG2E, lane)
        qa_ref[h] = qa
        ka_ref[h] = ka
        va_ref[h] = va

    col = lax.broadcasted_iota(I32, (ATT_W, LANES), 0) // HEAD_DIM
    head_sum = (col == lax.broadcasted_iota(I32, (ATT_W, LANES), 1)).astype(BF16)
    qsec = (p[:, POOL_W:POOL_W + ATT_W] * (ATT_SCALE * LOG2E)).astype(BF16).astype(F32)
    ksec = p[:, POOL_W + ATT_W:POOL_W + 2 * ATT_W].astype(BF16).astype(F32)
    for r, sec in enumerate((qsec, ksec)):
        sq = jnp.dot((sec * sec).astype(BF16), head_sum, preferred_element_type=F32)
        st_ref[r:r + 1, :] = jnp.max(sq, axis=0, keepdims=True)
    st_ref[2:3, :] = f_cum[0:1, :] * LOG2E
    st_ref[3:4, :] = f_cum[tm - 1:tm, :] * LOG2E


def _inproj(x, f0, lng, lnb, wm, bm, wf, bf):
    B, T, D = x.shape
    tm = min(SEQ_TILE, T)
    grid = (B, T // tm)
    const = lambda b, t: (0, 0)
    row = lambda b, t: (b, t, 0)
    aug = pl.BlockSpec((None, N_HEADS, tm, AUG_W), lambda b, t: (b, 0, t, 0))
    aug_shape = jax.ShapeDtypeStruct((B, N_HEADS, T, AUG_W), BF16)
    return pl.pallas_call(
        functools.partial(_inproj_kernel, tm=tm),
        grid=grid,
        in_specs=[
            pl.BlockSpec((None, tm, D), row),
            pl.BlockSpec((None, 1, LANES), lambda b, t: (b, 0, 0)),
            pl.BlockSpec((1, D), const), pl.BlockSpec((1, D), const),
            pl.BlockSpec((D, MAIN_W), const), pl.BlockSpec((1, MAIN_W), const),
            pl.BlockSpec((D, LANES), const), pl.BlockSpec((1, LANES), const),
        ],
        out_specs=[
            pl.BlockSpec((None, tm, POOL_W), row),
            pl.BlockSpec((None, tm, ATT_W), row),
            pl.BlockSpec((None, tm, ATT_W), row),
            pl.BlockSpec((None, tm, N_HEADS), row),
            aug, aug, aug,
            pl.BlockSpec((None, None, N_STATS, LANES), lambda b, t: (b, t, 0, 0)),
        ],
        out_shape=[
            jax.ShapeDtypeStruct((B, T, POOL_W), F32),
            jax.ShapeDtypeStruct((B, T, ATT_W), F32),
            jax.ShapeDtypeStruct((B, T, ATT_W), F32),
            jax.ShapeDtypeStruct((B, T, N_HEADS), F32),
            aug_shape, aug_shape, aug_shape,
            jax.ShapeDtypeStruct((B, T // tm, N_STATS, LANES), F32),
        ],
        scratch_shapes=[pltpu.VMEM((1, LANES), F32)],
        compiler_params=pltpu.CompilerParams(
            dimension_semantics=("arbitrary", "arbitrary"), vmem_limit_bytes=VMEM_LIMIT),
    )(x, f0, lng, lnb, wm, bm, wf, bf)


def _cache_kernel(ck_ref, cv_ref, clf_ref, ka_ref, va_ref, ftot_ref, carry_ref, *, tp):
    t = pl.program_id(1)

    @pl.when(t == 0)
    def _():
        carry_ref[...] = jnp.zeros_like(carry_ref)

    f_cum = _prefix_rows(clf_ref[...], tp, True) + carry_ref[...]
    carry_ref[...] = f_cum[tp - 1:tp, :]
    ftot_ref[...] = f_cum[tp - 1:tp, :]
    ck = ck_ref[...]
    cv = cv_ref[...]
    lane = lax.broadcasted_iota(I32, (tp, AUG_W), 1)
    for h in range(N_HEADS):
        _, ka, va = _augment(None, _head_slab(ck, 0, h), _head_slab(cv, 0, h), f_cum[:, h:h + 1] * LOG2E, lane)
        ka_ref[h] = ka
        va_ref[h] = va


def _cache_prep(ck, cv, clf):
    B, P, _ = ck.shape
    tp = min(SEQ_TILE, P)
    row = lambda b, t: (b, t, 0)
    aug = pl.BlockSpec((None, N_HEADS, tp, AUG_W), lambda b, t: (b, 0, t, 0))
    aug_shape = jax.ShapeDtypeStruct((B, N_HEADS, P, AUG_W), BF16)
    return pl.pallas_call(
        functools.partial(_cache_kernel, tp=tp),
        grid=(B, P // tp),
        in_specs=[pl.BlockSpec((None, tp, ATT_W), row), pl.BlockSpec((None, tp, ATT_W), row),
                  pl.BlockSpec((None, tp, LANES), row)],
        out_specs=[aug, aug, pl.BlockSpec((None, 1, LANES), lambda b, t: (b, 0, 0))],
        out_shape=[aug_shape, aug_shape, jax.ShapeDtypeStruct((B, 1, LANES), F32)],
        scratch_shapes=[pltpu.VMEM((1, LANES), F32)],
        compiler_params=pltpu.CompilerParams(
            dimension_semantics=("arbitrary", "arbitrary"), vmem_limit_bytes=VMEM_LIMIT),
    )(ck, cv, clf)


def _attn_kernel(jmin_ref, qa_ref, ka_ref, va_ref, o_ref, *, tq, tk, q_off):
    b, hp, iq = pl.program_id(0), pl.program_id(1), pl.program_id(2)
    q_lo = q_off + iq * tq
    n_full = (q_lo + 1) // tk
    n_all = (q_lo + tq - 1) // tk + 1
    j_first = jmin_ref[(b * pl.num_programs(1) + hp) * pl.num_programs(2) + iq]
    lane = lax.broadcasted_iota(I32, (tq, AUG_W), 1)
    qs = (qa_ref[0], qa_ref[1])

    def step(j, carry, masked):
        k0 = pl.multiple_of(j * tk, tk)
        new = []
        for hh in range(2):
            m, acc = carry[hh]
            k = ka_ref[hh, pl.ds(k0, tk), :]
            v = va_ref[hh, pl.ds(k0, tk), :]
            s = lax.dot_general(qs[hh], k, (((1,), (1,)), ((), ())), preferred_element_type=F32)
            if masked:
                qpos = q_lo + lax.broadcasted_iota(I32, (tq, tk), 0)
                kpos = k0 + lax.broadcasted_iota(I32, (tq, tk), 1)
                s = jnp.where(qpos >= kpos, s, -jnp.inf)
            m_new = jnp.maximum(m, jnp.max(s, axis=-1, keepdims=True))
            alpha = jnp.exp2(m - m_new)
            p = jnp.exp2(s - m_new)
            acc = acc * alpha + jnp.dot(p.astype(BF16), v, preferred_element_type=F32)
            new.append((m_new, acc))
        return tuple(new)

    init = (jnp.full((tq, 1), -jnp.inf, F32), jnp.zeros((tq, AUG_W), F32))
    carry = lax.fori_loop(j_first, n_full, functools.partial(step, masked=False), (init, init))
    carry = lax.fori_loop(n_full, n_all, functools.partial(step, masked=True), carry)
    outs = [acc / acc[:, HEAD_DIM:HEAD_DIM + 1] for _, acc in carry]
    o_ref[...] = jnp.where(lane < HEAD_DIM, outs[0], pltpu.roll(outs[1], HEAD_DIM, 1)).astype(BF16)


def _attention(jmin, qa, ka, va, q_off, tq, tk):
    B, H, Tq, _ = qa.shape
    Tk = ka.shape[2]
    return pl.pallas_call(
        functools.partial(_attn_kernel, tq=tq, tk=tk, q_off=q_off),
        grid_spec=pltpu.PrefetchScalarGridSpec(
            num_scalar_prefetch=1,
            grid=(B, H // 2, Tq // tq),
            in_specs=[
                pl.BlockSpec((None, 2, tq, AUG_W), lambda b, hp, iq, jm: (b, hp, iq, 0)),
                pl.BlockSpec((None, 2, Tk, AUG_W), lambda b, hp, iq, jm: (b, hp, 0, 0)),
                pl.BlockSpec((None, 2, Tk, AUG_W), lambda b, hp, iq, jm: (b, hp, 0, 0)),
            ],
            out_specs=pl.BlockSpec((None, tq, 2 * HEAD_DIM), lambda b, hp, iq, jm: (b, iq, hp)),
        ),
        out_shape=jax.ShapeDtypeStruct((B, Tq, ATT_W), BF16),
        compiler_params=pltpu.CompilerParams(
            dimension_semantics=("arbitrary", "arbitrary", "arbitrary"), vmem_limit_bytes=VMEM_LIMIT),
    )(jmin, qa, ka, va)


def _attn_skip_plan(stats, tq, tk, tm):
    B, nt = stats.shape[:2]
    st = stats[..., :N_HEADS]
    per = lambda row, r: st[:, :, row].reshape(B, nt // r, r, N_HEADS)
    rq, rk = tq // tm, tk // tm
    nq, nk = nt // rq, nt // rk
    qn2, kn2, kn2_own = per(0, rq).max(2), per(1, rk).max(2), per(1, rq).max(2)
    f_first, f_last = per(2, rq)[:, :, 0], per(3, rk)[:, :, -1]
    slack = 1.01
    upper = jnp.sqrt(qn2[:, :, None] * kn2[:, None, :]) * slack + (f_first[:, :, None] - f_last[:, None, :])
    lower = -jnp.sqrt(qn2 * kn2_own) * slack
    weightless = upper - lower[:, :, None] <= -(SKIP_LOG2 + 2.0)
    j = jnp.arange(nk, dtype=I32)[None, None, :, None]
    n_full = ((jnp.arange(nq, dtype=I32) * tq + 1) // tk)[None, :, None, None]
    first = jnp.min(jnp.where(weightless | (j >= n_full), n_full, j), axis=2)
    first = jnp.min(first.reshape(B, nq, N_HEADS // 2, 2), axis=3)
    return jnp.swapaxes(first, 1, 2).reshape(-1).astype(I32)


def _mixer_kernel(x_ref, u_ref, uprev_ref, hist_ref, ya_ref, cnt0_ref,
                  lng_ref, lnb_ref, wg_ref, bg_ref, wpool_ref, spool_ref, wpp_ref, watt_ref, wout_ref,
                  ln1g_ref, ln1b_ref, wrh_ref, wrl_ref, br_ref, *rest, tm, start_pos):
    h_ref, hp_ref, tw_ref, code_ref, cnt_ref, uext_ref = rest[-6:]
    b = pl.program_id(0)
    t = pl.program_id(1)

    @pl.when((b == 0) & (t == 0))
    def _():
        cnt_ref[...] = cnt0_ref[...]

    @pl.when(t == 0)
    def _():
        uext_ref[0:HALO, :] = hist_ref[...]

    @pl.when(t > 0)
    def _():
        uext_ref[0:HALO, :] = uprev_ref[...]

    uext_ref[HALO:HALO + tm, :] = u_ref[...]
    _mixer_rows(0, tm, start_pos + t * tm, x_ref, ya_ref, lng_ref, lnb_ref, wg_ref, bg_ref, wpool_ref,
                spool_ref, wpp_ref, watt_ref, wout_ref, ln1g_ref, ln1b_ref, wrh_ref, wrl_ref, br_ref,
                h_ref, hp_ref, code_ref, tw_ref, cnt_ref, uext_ref)


def _mixer_rows(r0, rows, pos0, x_ref, ya_ref, lng_ref, lnb_ref, wg_ref, bg_ref, wpool_ref, spool_ref, wpp_ref,
                watt_ref, wout_ref, ln1g_ref, ln1b_ref, wrh_ref, wrl_ref, br_ref,
                h_ref, hp_ref, code_ref, tw_ref, cnt_ref, uext_ref):
    rs = slice(r0, r0 + rows)
    xn = _layer_norm(x_ref[rs, :], lng_ref[...], lnb_ref[...])
    g = jnp.dot(xn.astype(BF16), wg_ref[...], preferred_element_type=F32) + bg_ref[...]
    gates = jax.nn.sigmoid(g)

    pos = pos0 + r0 + lax.broadcasted_iota(I32, (rows, 1), 0)
    pooled = None
    for gi, w in enumerate(POOL_WINDOWS):
        sl = slice(gi * POOL_GC, (gi + 1) * POOL_GC)
        cur = uext_ref[HALO + r0:HALO + r0 + rows, sl]
        s = cur
        for j in range(1, w):
            s = s + uext_ref[HALO + r0 - j:HALO + r0 - j + rows, sl]
        count = jnp.minimum(pos + 1, w).astype(F32)
        d = s / count - cur
        yg = jnp.dot(d.astype(BF16), wpool_ref[gi], preferred_element_type=F32) * spool_ref[:, sl]
        contrib = jnp.dot(yg.astype(BF16), wpp_ref[sl, :], preferred_element_type=F32)
        pooled = contrib if pooled is None else pooled + contrib
    att = jnp.dot(ya_ref[rs, :], watt_ref[...], preferred_element_type=F32)
    m = gates[:, :D_MODEL] * pooled + gates[:, D_MODEL:] * att
    mix = jnp.dot(m.astype(BF16), wout_ref[...], preferred_element_type=F32)
    h = _layer_norm(DEEPNORM_ALPHA * xn + mix, ln1g_ref[...], ln1b_ref[...])
    h_ref[rs, :] = h
    packed = _pack_halves(h)
    hp_ref[0, rs, :] = packed[:, :PLANE]
    hp_ref[1, rs, :] = packed[:, PLANE:]

    hh = h.astype(BF16)
    hl = (h - hh.astype(F32)).astype(BF16)
    logits = (jnp.dot(hh, wrh_ref[...], preferred_element_type=F32)
              + jnp.dot(hl, wrh_ref[...], preferred_element_type=F32)
              + jnp.dot(hh, wrl_ref[...], preferred_element_type=F32) + br_ref[...])
    lane = lax.broadcasted_iota(I32, (rows, LANES), 1)
    work = jnp.where(lane < N_EXPERTS, logits, -jnp.inf)
    vals, idxs = [], []
    for _ in range(TOP_K):
        mx = jnp.max(work, axis=-1, keepdims=True)
        idx = jnp.min(jnp.where(work == mx, lane, LANES), axis=-1, keepdims=True)
        vals.append(mx)
        idxs.append(idx)
        work = jnp.where(lane == idx, -jnp.inf, work)
    exps = [jnp.exp(v - vals[0]) for v in vals]
    denom = exps[0] + exps[1] + exps[2] + exps[3]

    onehot = jnp.zeros((rows, LANES), F32)
    for idx in idxs:
        onehot = onehot + (lane == idx).astype(F32)
    base = _prefix_rows(onehot, rows, False) + cnt_ref[...]
    code = jnp.zeros((rows, LANES), I32)
    tw = jnp.zeros((rows, LANES), F32)
    for k in range(TOP_K):
        rank = jnp.sum(jnp.where(lane == idxs[k], base, 0.0), axis=-1, keepdims=True)
        code = jnp.where(lane == k, idxs[k] * RANK_SPAN + rank.astype(I32), code)
        tw = jnp.where(lane == k, exps[k] / denom, tw)
    tw_ref[rs, :] = tw
    if code_ref.shape[0] == CODE_ROWS:
        code_ref[:, rs] = jnp.transpose(code)[:CODE_ROWS, :]
    else:
        code_ref[rs, :] = code
    cnt_ref[...] = cnt_ref[...] + jnp.sum(onehot, axis=0, keepdims=True)


def _mixer(x, u, hist, yatt, cnt0, start_pos, weights, n_tok, row_off, prev):
    B, T, D = x.shape
    tm = min(SEQ_TILE, T)
    nt = T // tm
    row = lambda b, t: (b, t, 0)
    tok = lambda b, t: (row_off // tm + b * nt + t, 0)
    hpt = tm // HALO
    full = lambda a: pl.BlockSpec(a.shape, lambda b, t, _n=a.ndim: (0,) * _n)
    tok3 = lambda b, t: (0, row_off // tm + b * nt + t, 0)
    assert n_tok <= RANK_SPAN
    if tm % LANES == 0:
        code_spec = pl.BlockSpec((CODE_ROWS, tm), lambda b, t: (0, b * nt + t))
        code_shape = jax.ShapeDtypeStruct((CODE_ROWS, B * T), I32)
    else:
        code_spec = pl.BlockSpec((tm, LANES), lambda b, t: (b * nt + t, 0))
        code_shape = jax.ShapeDtypeStruct((B * T, LANES), I32)
    out_specs = [pl.BlockSpec((tm, D), tok), pl.BlockSpec((2, tm, PLANE), tok3), pl.BlockSpec((tm, LANES), tok),
                 code_spec, pl.BlockSpec((1, LANES), lambda b, t: (0, 0))]
    out_shape = [jax.ShapeDtypeStruct((n_tok, D), F32), jax.ShapeDtypeStruct((2, n_tok, PLANE), U32),
                 jax.ShapeDtypeStruct((n_tok, LANES), F32), code_shape, jax.ShapeDtypeStruct((1, LANES), F32)]
    n_in = 6 + len(weights)
    return pl.pallas_call(
        functools.partial(_mixer_kernel, tm=tm, start_pos=start_pos),
        grid=(B, nt),
        in_specs=[
            pl.BlockSpec((None, tm, D), row),
            pl.BlockSpec((None, tm, POOL_W), row),
            pl.BlockSpec((None, HALO, POOL_W), lambda b, t: (b, jnp.maximum(t * hpt - 1, 0), 0)),
            pl.BlockSpec((None, HALO, POOL_W), lambda b, t: (b, 0, 0)),
            pl.BlockSpec((None, tm, ATT_W), row),
            pl.BlockSpec((1, LANES), lambda b, t: (0, 0)),
        ] + [full(w) for w in weights] + [pl.BlockSpec(memory_space=pl.ANY) for _ in prev],
        out_specs=out_specs,
        out_shape=out_shape,
        scratch_shapes=[pltpu.VMEM((HALO + tm, POOL_W), F32)],
        input_output_aliases={n_in + i: i for i in range(len(prev))},
        compiler_params=pltpu.CompilerParams(
            dimension_semantics=("arbitrary", "arbitrary"), vmem_limit_bytes=VMEM_LIMIT),
    )(x, u, u, hist, yatt, cnt0, *weights, *prev)


def _sc_mesh():
    return plsc.VectorSubcoreMesh(core_axis_name="c", subcore_axis_name="s")


def _sc_scatter_rows(x, idx, n_rows):
    n = x.shape[0]

    @pl.kernel(out_type=jax.ShapeDtypeStruct((n_rows, PLANE), x.dtype), mesh=_sc_mesh(), scratch_types=[])
    def scatter(x_hbm, i_hbm, o_hbm):
        def body(x_vmem, i_vmem):
            for k in range(TOP_K):
                pltpu.sync_copy(x_vmem, o_hbm.at[i_vmem.at[k]])

        pltpu.emit_pipeline(
            body,
            grid=(n // SC_WINDOW,),
            in_specs=[pl.BlockSpec((SC_WINDOW, PLANE), index_map=lambda i: (i, 0)),
                      pl.BlockSpec((TOP_K, SC_WINDOW), index_map=lambda i: (0, i))],
            out_specs=[],
            core_axis_name=("c", "s"),
            dimension_semantics=(pltpu.PARALLEL,),
        )(x_hbm, i_hbm)

    return scatter(x, idx)


def _sc_gather_rows(y, idx):
    n = idx.shape[1]

    @pl.kernel(out_type=jax.ShapeDtypeStruct((n, PLANE), y.dtype), mesh=_sc_mesh(), scratch_types=[])
    def gather(y_hbm, i_hbm, o_hbm):
        def body(i_vmem, o_vmem):
            pltpu.sync_copy(y_hbm.at[i_vmem.at[0]], o_vmem)

        pltpu.emit_pipeline(
            body,
            grid=(n // SC_WINDOW,),
            in_specs=[pl.BlockSpec((1, SC_WINDOW), index_map=lambda i: (0, i))],
            out_specs=[pl.BlockSpec((SC_WINDOW, PLANE), index_map=lambda i: (i, 0))],
            core_axis_name=("c", "s"),
            dimension_semantics=(pltpu.PARALLEL,),
        )(i_hbm, o_hbm)

    return gather(y, idx)


def _join_planes(ref):
    return jnp.concatenate([ref[0], ref[1]], axis=1)


def _expert_kernel(be_ref, nb_ref, first_ref, xb_ref, w1_ref, b1_ref, w2_ref, b2_ref, yb_ref, w1b_ref, w2b_ref):
    del be_ref
    i = pl.program_id(0)

    @pl.when(first_ref[i] == 1)
    def _():
        w1b_ref[...] = w1_ref[...].astype(BF16)
        w2b_ref[...] = w2_ref[...].astype(BF16)

    @pl.when(i < nb_ref[0])
    def _():
        xa, xc = _unpack_halves(_join_planes(xb_ref))
        hfull = (jnp.dot(xa.astype(BF16), w1b_ref[:HALF, :], preferred_element_type=F32)
                 + jnp.dot(xc.astype(BF16), w1b_ref[HALF:, :], preferred_element_type=F32) + b1_ref[...])
        glu = jnp.minimum(hfull[:, :D_FF], SWIGLU_LIMIT)
        lin = jnp.clip(hfull[:, D_FF:], -SWIGLU_LIMIT, SWIGLU_LIMIT)
        a = glu * jax.nn.sigmoid(SWIGLU_ALPHA * glu) * (lin + 1.0)
        y = jnp.dot(a.astype(BF16), w2b_ref[...], preferred_element_type=F32) + b2_ref[...]
        packed = _pack_halves(y)
        yb_ref[0] = packed[:, :PLANE]
        yb_ref[1] = packed[:, PLANE:]

    @pl.when(i >= nb_ref[0])
    def _():
        yb_ref[...] = jnp.zeros_like(yb_ref)


def _experts(block_e, n_used, first, xb, w1, b1, w2, b2):
    n_rows = xb.shape[1]
    bm = MOE_BLOCK
    rows = pl.BlockSpec((2, bm, PLANE), lambda i, be, nb, fi: (0, i, 0))
    per_expert = lambda r, c: pl.BlockSpec((None, r, c), lambda i, be, nb, fi: (be[i], 0, 0))
    return pl.pallas_call(
        _expert_kernel,
        grid_spec=pltpu.PrefetchScalarGridSpec(
            num_scalar_prefetch=3,
            grid=(n_rows // bm,),
            in_specs=[rows, per_expert(D_MODEL, 2 * D_FF), per_expert(1, 2 * D_FF),
                      per_expert(D_FF, D_MODEL), per_expert(1, D_MODEL)],
            out_specs=rows,
            scratch_shapes=[pltpu.VMEM((D_MODEL, 2 * D_FF), BF16), pltpu.VMEM((D_FF, D_MODEL), BF16)],
        ),
        out_shape=jax.ShapeDtypeStruct((2, n_rows, PLANE), U32),
        compiler_params=pltpu.CompilerParams(
            dimension_semantics=("arbitrary",), vmem_limit_bytes=VMEM_LIMIT),
    )(block_e, n_used, first, xb, w1, b1, w2, b2)


def _combine_kernel(h_ref, tw_ref, g_ref, b_ref, y4_ref, *rest):
    out_ref = rest[-1]
    tw = tw_ref[...]
    acc_hi = None
    acc_lo = None
    for k in range(TOP_K):
        hi, lo = _unpack_halves(_join_planes(y4_ref.at[k]))
        w = tw[:, k:k + 1]
        acc_hi = w * hi if acc_hi is None else acc_hi + w * hi
        acc_lo = w * lo if acc_lo is None else acc_lo + w * lo
    moe = jnp.concatenate([acc_hi, acc_lo], axis=1)
    out_ref[...] = _layer_norm(DEEPNORM_ALPHA * h_ref[...] + moe, g_ref[...], b_ref[...])


def _combine(h, tw, ln2g, ln2b, y4, tok_off, y_off, out_off, n_rows, out_rows, prev):
    D = h.shape[1]
    tn = min(SEQ_TILE, n_rows)
    const = lambda i: (0, 0)
    return pl.pallas_call(
        _combine_kernel,
        grid=(n_rows // tn,),
        in_specs=[
            pl.BlockSpec((tn, D), lambda i: (tok_off // tn + i, 0)),
            pl.BlockSpec((tn, LANES), lambda i: (tok_off // tn + i, 0)),
            pl.BlockSpec((1, D), const), pl.BlockSpec((1, D), const),
            pl.BlockSpec((TOP_K, 2, tn, PLANE), lambda i: (0, 0, y_off // tn + i, 0)),
        ] + [pl.BlockSpec(memory_space=pl.ANY) for _ in prev],
        out_specs=pl.BlockSpec((tn, D), lambda i: (out_off // tn + i, 0)),
        out_shape=jax.ShapeDtypeStruct((out_rows, D), F32),
        input_output_aliases={5 + i: i for i in range(len(prev))},
        compiler_params=pltpu.CompilerParams(
            dimension_semantics=("arbitrary",), vmem_limit_bytes=VMEM_LIMIT),
    )(h, tw, ln2g, ln2b, y4, *prev)


def kernel(x_prompt, x_sample, cache_pool, cache_k, cache_v, cache_logf, ln_in_g, ln_in_b, w_in, b_in,
           w_pool, s_pool, w_pool_proj, w_att_proj, w_out, ln1_g, ln1_b, w_router, b_router,
           w1, b1, w2, b2, ln2_g, ln2_b):
    assert w_in.shape[0] == DEPTH
    B, T, D = x_prompt.shape
    Bs, Ts, _ = x_sample.shape
    P = cache_k.shape[2]
    row2 = lambda a: a.reshape(1, -1).astype(F32)

    f_off = MAIN_W
    g_off = MAIN_W + N_HEADS
    wm = w_in[0][:, :f_off].astype(BF16)
    bm = row2(b_in[0][:f_off])
    wf = jnp.pad(w_in[0][:, f_off:g_off], ((0, 0), (0, LANES - N_HEADS))).astype(BF16)
    bf = row2(jnp.pad(b_in[0][f_off:g_off], (0, LANES - N_HEADS)))
    wg = w_in[0][:, g_off:].astype(BF16)
    bg = row2(b_in[0][g_off:])
    lng, lnb = row2(ln_in_g), row2(ln_in_b)
    wr = jnp.pad(w_router[0], ((0, 0), (0, LANES - N_EXPERTS)))
    wr_hi = wr.astype(BF16)
    wr_lo = (wr - wr_hi.astype(F32)).astype(BF16)
    br = row2(jnp.pad(b_router[0], (0, LANES - N_EXPERTS)))
    mixer_weights = (lng, lnb, wg, bg, w_pool[0].astype(BF16), row2(s_pool[0]),
                     w_pool_proj[0].astype(BF16), w_att_proj[0].astype(BF16), w_out[0].astype(BF16),
                     row2(ln1_g[0]), row2(ln1_b[0]), wr_hi, wr_lo, br)

    zeros_f = jnp.zeros((B, 1, LANES), F32)
    u_p, k_p, v_p, logf_p, qa_p, ka_p, va_p, stats = _inproj(x_prompt, zeros_f, lng, lnb, wm, bm, wf, bf)
    tq, tk = min(ATT_TQ, T), min(ATT_TK, T)
    ya_p = _attention(_attn_skip_plan(stats, tq, tk, min(SEQ_TILE, T)), qa_p, ka_p, va_p, 0, tq, tk)

    clf = jnp.pad(cache_logf[0], ((0, 0), (0, 0), (0, LANES - N_HEADS)))
    ka_c, va_c, f_tot = _cache_prep(cache_k[0].reshape(Bs, P, ATT_W), cache_v[0].reshape(Bs, P, ATT_W), clf)
    u_s, k_s, v_s, logf_s, qa_s, ka_s, va_s, _ = _inproj(x_sample, f_tot, lng, lnb, wm, bm, wf, bf)
    pad_keys = (-(P + Ts)) % LANES
    tks = P + Ts + pad_keys
    zpad = jnp.zeros((Bs, N_HEADS, pad_keys, AUG_W), BF16)
    visit_all = jnp.zeros((Bs * (N_HEADS // 2),), I32)
    ya_s = _attention(visit_all, qa_s, jnp.concatenate([ka_c, ka_s, zpad], axis=2),
                      jnp.concatenate([va_c, va_s, zpad], axis=2), P, Ts, tks)

    cnt0 = jnp.zeros((1, LANES), F32)
    hist_p = jnp.zeros((B, HALO, POOL_W), F32)
    n_tok = B * T + Bs * Ts
    *bufs, code_p, cnt_p = _mixer(x_prompt, u_p, hist_p, ya_p, cnt0, 0, mixer_weights, n_tok, 0, ())
    u_full_s = jnp.concatenate([cache_pool[0].astype(F32), u_s], axis=1)
    hist_s = jnp.pad(cache_pool[0].astype(F32), ((0, 0), (HALO - POOL_HIST, 0), (0, 0)))
    h_all, hp_all, tw, code_s, cnt = _mixer(x_sample, u_s, hist_s, ya_s, cnt_p, P, mixer_weights,
                                            n_tok, B * T, tuple(bufs))
    by_slot = lambda c: c[:TOP_K] if c.shape[0] == CODE_ROWS else c[:, :TOP_K].T
    code = jnp.concatenate([by_slot(code_p), by_slot(code_s)], axis=1)
    ti, rk = code // RANK_SPAN, code % RANK_SPAN

    counts = cnt[0, :N_EXPERTS].astype(I32)
    padded = (counts + MOE_BLOCK - 1) // MOE_BLOCK * MOE_BLOCK
    pad_ends = jnp.cumsum(padded)
    pad_starts = pad_ends - padded
    n_blocks = (n_tok * TOP_K + N_EXPERTS * (MOE_BLOCK - 1) + MOE_BLOCK - 1) // MOE_BLOCK
    n_rows = n_blocks * MOE_BLOCK
    block_start = jnp.arange(n_blocks, dtype=I32) * MOE_BLOCK
    block_e = jnp.minimum(jnp.sum((pad_ends[None, :] <= block_start[:, None]).astype(I32), axis=1), N_EXPERTS - 1)
    n_used = (pad_ends[-1:] // MOE_BLOCK).astype(I32)
    dest_t = (pad_starts[ti] + rk).astype(I32)
    dest_planes = jnp.stack([dest_t, dest_t + n_rows], axis=1)

    xb = _sc_scatter_rows(hp_all.reshape(2 * n_tok, PLANE), dest_planes.reshape(TOP_K, 2 * n_tok), 2 * n_rows)
    first = jnp.concatenate([jnp.ones((1,), I32), (block_e[1:] != block_e[:-1]).astype(I32)])
    yb = _experts(block_e, n_used, first, xb.reshape(2, n_rows, PLANE),
                  w1[0], b1[0][:, None, :], w2[0], b2[0][:, None, :])
    ln2g, ln2b = row2(ln2_g[0]), row2(ln2_b[0])
    yb_rows = yb.reshape(2 * n_rows, PLANE)
    chunk = (B * T) // COMBINE_CHUNKS
    y_prompt = ()
    for c in range(COMBINE_CHUNKS):
        lo = c * chunk
        hi = n_tok if c == COMBINE_CHUNKS - 1 else lo + chunk
        y4 = _sc_gather_rows(yb_rows, dest_planes[:, :, lo:hi].reshape(1, TOP_K * 2 * (hi - lo)))
        y4 = y4.reshape(TOP_K, 2, hi - lo, PLANE)
        y_prompt = (_combine(h_all, tw, ln2g, ln2b, y4, lo, 0, lo, chunk, B * T, y_prompt),)
    y_sample = _combine(h_all, tw, ln2g, ln2b, y4, B * T, chunk, 0, Bs * Ts, Bs * Ts, ())
    y_prompt, y_sample = y_prompt[0].reshape(B, T, D), y_sample.reshape(Bs, Ts, D)
    heads = lambda a, b_, t_: a.reshape(1, b_, t_, N_HEADS, HEAD_DIM)
    return (y_prompt, y_sample,
            heads(k_p, B, T), heads(v_p, B, T), logf_p[None], u_p[:, -POOL_HIST:][None],
            heads(k_s, Bs, Ts), heads(v_s, Bs, Ts), logf_s[None], u_full_s[:, -POOL_HIST:][None])
```

```python
import functools

import jax
import jax.numpy as jnp
from jax import lax
from jax.experimental import pallas as pl
from jax.experimental.pallas import tpu as pltpu
from jax.experimental.pallas import tpu_sc as plsc

F32 = jnp.float32
BF16 = jnp.bfloat16
I32 = jnp.int32
U32 = jnp.uint32

D_MODEL = 1024
N_HEADS = 8
HEAD_DIM = 64
ATT_W = N_HEADS * HEAD_DIM
POOL_WINDOWS = (2, 4, 8, 16)
POOL_GC = 128
POOL_W = len(POOL_WINDOWS) * POOL_GC
POOL_HIST = max(POOL_WINDOWS) - 1
HALO = 16
N_EXPERTS = 32
TOP_K = 4
D_FF = D_MODEL
SWIGLU_ALPHA = 1.702
SWIGLU_LIMIT = 7.0
LN_EPS = 1e-5
DEPTH = 1
DEEPNORM_ALPHA = (2.0 * DEPTH) ** 0.25
ATT_SCALE = HEAD_DIM ** -0.5
LOG2E = 1.4426950408889634
MAIN_W = POOL_W + 3 * ATT_W
LANES = 128
AUG_W = LANES
HALF = D_MODEL // 2
VMEM_LIMIT = 56 * 1024 * 1024

SEQ_TILE = 512
ATT_TQ = 1024
ATT_TK = 1024
N_STATS = 4
SKIP_LOG2 = 150.0
MOE_BLOCK = 512
RANK_SPAN = 1 << 20
CODE_ROWS = 8
PLANE = HALF // 2
SC_WINDOW = 128


def _layer_norm(x, g, b):
    mu = jnp.mean(x, axis=-1, keepdims=True)
    xc = x - mu
    var = jnp.mean(xc * xc, axis=-1, keepdims=True)
    return xc * lax.rsqrt(var + LN_EPS) * g + b


def _split3(x):
    a = x.astype(BF16)
    r = x - a.astype(F32)
    b = r.astype(BF16)
    c = (r - b.astype(F32)).astype(BF16)
    return a, b, c


def _prefix_rows(x, tm, inclusive):
    kp = max(tm, LANES)
    r = lax.broadcasted_iota(I32, (tm, kp), 0)
    c = lax.broadcasted_iota(I32, (tm, kp), 1)
    tri = ((c <= r) if inclusive else (c < r)).astype(BF16)
    if kp > tm:
        x = jnp.concatenate([x, jnp.zeros((kp - tm, x.shape[1]), x.dtype)], axis=0)
    out = None
    for piece in _split3(x):
        y = jnp.dot(tri, piece, preferred_element_type=F32)
        out = y if out is None else out + y
    return out


def _pack_halves(y):
    hi = pltpu.bitcast(y[:, :HALF].astype(BF16).astype(F32), U32)
    lo = pltpu.bitcast(y[:, HALF:].astype(BF16).astype(F32), U32)
    return hi | (lo >> 16)


def _unpack_halves(w):
    hi = pltpu.bitcast(w & jnp.uint32(0xFFFF0000), F32)
    lo = pltpu.bitcast(w << 16, F32)
    return hi, lo


def _head_slab(p, off, h):
    s = p[:, off + (h // 2) * LANES: off + (h // 2) * LANES + LANES]
    return s if h % 2 == 0 else pltpu.roll(s, HEAD_DIM, 1)


def _augment(qs, ks, vs, f_col, lane):
    fc = jnp.broadcast_to(f_col, lane.shape)
    hi = fc.astype(BF16).astype(F32)
    r1 = fc - hi
    mid = r1.astype(BF16).astype(F32)
    lo = r1 - mid
    one = jnp.ones_like(fc)
    zero = jnp.zeros_like(fc)
    ka = jnp.where(lane < 64, ks, jnp.where(lane < 67, one, jnp.where(
        lane == 67, -hi, jnp.where(lane == 68, -mid, jnp.where(lane == 69, -lo, zero)))))
    va = jnp.where(lane < 64, vs, jnp.where(lane == 64, one, zero))
    if qs is None:
        return None, ka.astype(BF16), va.astype(BF16)
    qa = jnp.where(lane < 64, qs, jnp.where(lane == 64, hi, jnp.where(
        lane == 65, mid, jnp.where(lane == 66, lo, jnp.where(lane < 70, one, zero)))))
    return qa.astype(BF16), ka.astype(BF16), va.astype(BF16)


def _inproj_kernel(x_ref, f0_ref, lng_ref, lnb_ref, wm_ref, bm_ref, wf_ref, bf_ref,
                   u_ref, k_ref, v_ref, logf_ref, qa_ref, ka_ref, va_ref, st_ref, carry_ref, *, tm):
    t = pl.program_id(1)

    @pl.when(t == 0)
    def _():
        carry_ref[...] = f0_ref[...]

    xn = _layer_norm(x_ref[...], lng_ref[...], lnb_ref[...]).astype(BF16)
    p = jnp.dot(xn, wm_ref[...], preferred_element_type=F32) + bm_ref[...]
    fl = jnp.dot(xn, wf_ref[...], preferred_element_type=F32) + bf_ref[...]
    logf = jnp.minimum(fl, 0.0) - jnp.log(1.0 + jnp.exp(-jnp.abs(fl)))
    f_cum = _prefix_rows(logf, tm, True) + carry_ref[...]
    carry_ref[...] = f_cum[tm - 1:tm, :]

    u_ref[...] = p[:, :POOL_W]
    k_ref[...] = p[:, POOL_W + ATT_W:POOL_W + 2 * ATT_W]
    v_ref[...] = p[:, POOL_W + 2 * ATT_W:]
    logf_ref[...] = logf[:, :N_HEADS]

    lane = lax.broadcasted_iota(I32, (tm, AUG_W), 1)
    for h in range(N_HEADS):
        qs = _head_slab(p, POOL_W, h) * (ATT_SCALE * LOG2E)
        ks = _head_slab(p, POOL_W + ATT_W, h)
        vs = _head_slab(p, POOL_W + 2 * ATT_W, h)
        qa, ka, va = _augment(qs, ks, vs, f_cum[:, h:h + 1] * LOG2E, lane)
        qa_ref[h] = qa
        ka_ref[h] = ka
        va_ref[h] = va

    col = lax.broadcasted_iota(I32, (ATT_W, LANES), 0) // HEAD_DIM
    head_sum = (col == lax.broadcasted_iota(I32, (ATT_W, LANES), 1)).astype(BF16)
    qsec = (p[:, POOL_W:POOL_W + ATT_W] * (ATT_SCALE * LOG2E)).astype(BF16).astype(F32)
    ksec = p[:, POOL_W + ATT_W:POOL_W + 2 * ATT_W].astype(BF16).astype(F32)
    for r, sec in enumerate((qsec, ksec)):
        sq = jnp.dot((sec * sec).astype(BF16), head_sum, preferred_element_type=F32)
        st_ref[r:r + 1, :] = jnp.max(sq, axis=0, keepdims=True)
    st_ref[2:3, :] = f_cum[0:1, :] * LOG2E
    st_ref[3:4, :] = f_cum[tm - 1:tm, :] * LOG2E


def _inproj(x, f0, lng, lnb, wm, bm, wf, bf):
    B, T, D = x.shape
    tm = min(SEQ_TILE, T)
    grid = (B, T // tm)
    const = lambda b, t: (0, 0)
    row = lambda b, t: (b, t, 0)
    aug = pl.BlockSpec((None, N_HEADS, tm, AUG_W), lambda b, t: (b, 0, t, 0))
    aug_shape = jax.ShapeDtypeStruct((B, N_HEADS, T, AUG_W), BF16)
    return pl.pallas_call(
        functools.partial(_inproj_kernel, tm=tm),
        grid=grid,
        in_specs=[
            pl.BlockSpec((None, tm, D), row),
            pl.BlockSpec((None, 1, LANES), lambda b, t: (b, 0, 0)),
            pl.BlockSpec((1, D), const), pl.BlockSpec((1, D), const),
            pl.BlockSpec((D, MAIN_W), const), pl.BlockSpec((1, MAIN_W), const),
            pl.BlockSpec((D, LANES), const), pl.BlockSpec((1, LANES), const),
        ],
        out_specs=[
            pl.BlockSpec((None, tm, POOL_W), row),
            pl.BlockSpec((None, tm, ATT_W), row),
            pl.BlockSpec((None, tm, ATT_W), row),
            pl.BlockSpec((None, tm, N_HEADS), row),
            aug, aug, aug,
            pl.BlockSpec((None, None, N_STATS, LANES), lambda b, t: (b, t, 0, 0)),
        ],
        out_shape=[
            jax.ShapeDtypeStruct((B, T, POOL_W), F32),
            jax.ShapeDtypeStruct((B, T, ATT_W), F32),
            jax.ShapeDtypeStruct((B, T, ATT_W), F32),
            jax.ShapeDtypeStruct((B, T, N_HEADS), F32),
            aug_shape, aug_shape, aug_shape,
            jax.ShapeDtypeStruct((B, T // tm, N_STATS, LANES), F32),
        ],
        scratch_shapes=[pltpu.VMEM((1, LANES), F32)],
        compiler_params=pltpu.CompilerParams(
            dimension_semantics=("arbitrary", "arbitrary"), vmem_limit_bytes=VMEM_LIMIT),
    )(x, f0, lng, lnb, wm, bm, wf, bf)


def _cache_kernel(ck_ref, cv_ref, clf_ref, ka_ref, va_ref, ftot_ref, carry_ref, *, tp):
    t = pl.program_id(1)

    @pl.when(t == 0)
    def _():
        carry_ref[...] = jnp.zeros_like(carry_ref)

    f_cum = _prefix_rows(clf_ref[...], tp, True) + carry_ref[...]
    carry_ref[...] = f_cum[tp - 1:tp, :]
    ftot_ref[...] = f_cum[tp - 1:tp, :]
    ck = ck_ref[...]
    cv = cv_ref[...]
    lane = lax.broadcasted_iota(I32, (tp, AUG_W), 1)
    for h in range(N_HEADS):
        _, ka, va = _augment(None, _head_slab(ck, 0, h), _head_slab(cv, 0, h), f_cum[:, h:h + 1] * LOG2E, lane)
        ka_ref[h] = ka
        va_ref[h] = va


def _cache_prep(ck, cv, clf):
    B, P, _ = ck.shape
    tp = min(SEQ_TILE, P)
    row = lambda b, t: (b, t, 0)
    aug = pl.BlockSpec((None, N_HEADS, tp, AUG_W), lambda b, t: (b, 0, t, 0))
    aug_shape = jax.ShapeDtypeStruct((B, N_HEADS, P, AUG_W), BF16)
    return pl.pallas_call(
        functools.partial(_cache_kernel, tp=tp),
        grid=(B, P // tp),
        in_specs=[pl.BlockSpec((None, tp, ATT_W), row), pl.BlockSpec((None, tp, ATT_W), row),
                  pl.BlockSpec((None, tp, LANES), row)],
        out_specs=[aug, aug, pl.BlockSpec((None, 1, LANES), lambda b, t: (b, 0, 0))],
        out_shape=[aug_shape, aug_shape, jax.ShapeDtypeStruct((B, 1, LANES), F32)],
        scratch_shapes=[pltpu.VMEM((1, LANES), F32)],
        compiler_params=pltpu.CompilerParams(
            dimension_semantics=("arbitrary", "arbitrary"), vmem_limit_bytes=VMEM_LIMIT),
    )(ck, cv, clf)


def _attn_kernel(jmin_ref, qa_ref, ka_ref, va_ref, o_ref, *, tq, tk, q_off):
    b, hp, iq = pl.program_id(0), pl.program_id(1), pl.program_id(2)
    q_lo = q_off + iq * tq
    n_full = (q_lo + 1) // tk
    n_all = (q_lo + tq - 1) // tk + 1
    j_first = jmin_ref[(b * pl.num_programs(1) + hp) * pl.num_programs(2) + iq]
    lane = lax.broadcasted_iota(I32, (tq, AUG_W), 1)
    qs = (qa_ref[0], qa_ref[1])

    def step(j, carry, masked):
        k0 = pl.multiple_of(j * tk, tk)
        new = []
        for hh in range(2):
            m, acc = carry[hh]
            k = ka_ref[hh, pl.ds(k0, tk), :]
            v = va_ref[hh, pl.ds(k0, tk), :]
            s = lax.dot_general(qs[hh], k, (((1,), (1,)), ((), ())), preferred_element_type=F32)
            if masked:
                qpos = q_lo + lax.broadcasted_iota(I32, (tq, tk), 0)
                kpos = k0 + lax.broadcasted_iota(I32, (tq, tk), 1)
                s = jnp.where(qpos >= kpos, s, -jnp.inf)
            m_new = jnp.maximum(m, jnp.max(s, axis=-1, keepdims=True))
            alpha = jnp.exp2(m - m_new)
            p = jnp.exp2(s - m_new)
            acc = acc * alpha + jnp.dot(p.astype(BF16), v, preferred_element_type=F32)
            new.append((m_new, acc))
        return tuple(new)

    init = (jnp.full((tq, 1), -jnp.inf, F32), jnp.zeros((tq, AUG_W), F32))
    carry = lax.fori_loop(j_first, n_full, functools.partial(step, masked=False), (init, init))
    carry = lax.fori_loop(n_full, n_all, functools.partial(step, masked=True), carry)
    outs = [acc / acc[:, HEAD_DIM:HEAD_DIM + 1] for _, acc in carry]
    o_ref[...] = jnp.where(lane < HEAD_DIM, outs[0], pltpu.roll(outs[1], HEAD_DIM, 1)).astype(BF16)


def _attention(jmin, qa, ka, va, q_off, tq, tk):
    B, H, Tq, _ = qa.shape
    Tk = ka.shape[2]
    return pl.pallas_call(
        functools.partial(_attn_kernel, tq=tq, tk=tk, q_off=q_off),
        grid_spec=pltpu.PrefetchScalarGridSpec(
            num_scalar_prefetch=1,
            grid=(B, H // 2, Tq // tq),
            in_specs=[
                pl.BlockSpec((None, 2, tq, AUG_W), lambda b, hp, iq, jm: (b, hp, iq, 0)),
                pl.BlockSpec((None, 2, Tk, AUG_W), lambda b, hp, iq, jm: (b, hp, 0, 0)),
                pl.BlockSpec((None, 2, Tk, AUG_W), lambda b, hp, iq, jm: (b, hp, 0, 0)),
            ],
            out_specs=pl.BlockSpec((None, tq, 2 * HEAD_DIM), lambda b, hp, iq, jm: (b, iq, hp)),
        ),
        out_shape=jax.ShapeDtypeStruct((B, Tq, ATT_W), BF16),
        compiler_params=pltpu.CompilerParams(
            dimension_semantics=("arbitrary", "arbitrary", "arbitrary"), vmem_limit_bytes=VMEM_LIMIT),
    )(jmin, qa, ka, va)


def _attn_skip_plan(stats, tq, tk, tm):
    B, nt = stats.shape[:2]
    st = stats[..., :N_HEADS]
    per = lambda row, r: st[:, :, row].reshape(B, nt // r, r, N_HEADS)
    rq, rk = tq // tm, tk // tm
    nq, nk = nt // rq, nt // rk
    qn2, kn2, kn2_own = per(0, rq).max(2), per(1, rk).max(2), per(1, rq).max(2)
    f_first, f_last = per(2, rq)[:, :, 0], per(3, rk)[:, :, -1]
    slack = 1.01
    upper = jnp.sqrt(qn2[:, :, None] * kn2[:, None, :]) * slack + (f_first[:, :, None] - f_last[:, None, :])
    lower = -jnp.sqrt(qn2 * kn2_own) * slack
    weightless = upper - lower[:, :, None] <= -(SKIP_LOG2 + 2.0)
    j = jnp.arange(nk, dtype=I32)[None, None, :, None]
    n_full = ((jnp.arange(nq, dtype=I32) * tq + 1) // tk)[None, :, None, None]
    first = jnp.min(jnp.where(weightless | (j >= n_full), n_full, j), axis=2)
    first = jnp.min(first.reshape(B, nq, N_HEADS // 2, 2), axis=3)
    return jnp.swapaxes(first, 1, 2).reshape(-1).astype(I32)


def _mixer_kernel(x_ref, u_ref, uprev_ref, hist_ref, ya_ref, cnt0_ref,
                  lng_ref, lnb_ref, wg_ref, bg_ref, wpool_ref, spool_ref, wpp_ref, watt_ref, wout_ref,
                  ln1g_ref, ln1b_ref, wrh_ref, wrl_ref, br_ref, *rest, tm, start_pos):
    h_ref, hp_ref, tw_ref, code_ref, cnt_ref, uext_ref = rest[-6:]
    b = pl.program_id(0)
    t = pl.program_id(1)

    @pl.when((b == 0) & (t == 0))
    def _():
        cnt_ref[...] = cnt0_ref[...]

    @pl.when(t == 0)
    def _():
        uext_ref[0:HALO, :] = hist_ref[...]

    @pl.when(t > 0)
    def _():
        uext_ref[0:HALO, :] = uprev_ref[...]

    uext_ref[HALO:HALO + tm, :] = u_ref[...]
    _mixer_rows(0, tm, start_pos + t * tm, x_ref, ya_ref, lng_ref, lnb_ref, wg_ref, bg_ref, wpool_ref,
                spool_ref, wpp_ref, watt_ref, wout_ref, ln1g_ref, ln1b_ref, wrh_ref, wrl_ref, br_ref,
                h_ref, hp_ref, code_ref, tw_ref, cnt_ref, uext_ref)


def _mixer_rows(r0, rows, pos0, x_ref, ya_ref, lng_ref, lnb_ref, wg_ref, bg_ref, wpool_ref, spool_ref, wpp_ref,
                watt_ref, wout_ref, ln1g_ref, ln1b_ref, wrh_ref, wrl_ref, br_ref,
                h_ref, hp_ref, code_ref, tw_ref, cnt_ref, uext_ref):
    rs = slice(r0, r0 + rows)
    xn = _layer_norm(x_ref[rs, :], lng_ref[...], lnb_ref[...])
    g = jnp.dot(xn.astype(BF16), wg_ref[...], preferred_element_type=F32) + bg_ref[...]
    gates = jax.nn.sigmoid(g)

    pos = pos0 + r0 + lax.broadcasted_iota(I32, (rows, 1), 0)
    pooled = None
    for gi, w in enumerate(POOL_WINDOWS):
        sl = slice(gi * POOL_GC, (gi + 1) * POOL_GC)
        cur = uext_ref[HALO + r0:HALO + r0 + rows, sl]
        s = cur
        for j in range(1, w):
            s = s + uext_ref[HALO + r0 - j:HALO + r0 - j + rows, sl]
        count = jnp.minimum(pos + 1, w).astype(F32)
        d = s / count - cur
        yg = jnp.dot(d.astype(BF16), wpool_ref[gi], preferred_element_type=F32) * spool_ref[:, sl]
        contrib = jnp.dot(yg.astype(BF16), wpp_ref[sl, :], preferred_element_type=F32)
        pooled = contrib if pooled is None else pooled + contrib
    att = jnp.dot(ya_ref[rs, :], watt_ref[...], preferred_element_type=F32)
    m = gates[:, :D_MODEL] * pooled + gates[:, D_MODEL:] * att
    mix = jnp.dot(m.astype(BF16), wout_ref[...], preferred_element_type=F32)
    h = _layer_norm(DEEPNORM_ALPHA * xn + mix, ln1g_ref[...], ln1b_ref[...])
    h_ref[rs, :] = h
    packed = _pack_halves(h)
    hp_ref[0, rs, :] = packed[:, :PLANE]
    hp_ref[1, rs, :] = packed[:, PLANE:]

    hh = h.astype(BF16)
    hl = (h - hh.astype(F32)).astype(BF16)
    logits = (jnp.dot(hh, wrh_ref[...], preferred_element_type=F32)
              + jnp.dot(hl, wrh_ref[...], preferred_element_type=F32)
              + jnp.dot(hh, wrl_ref[...], preferred_element_type=F32) + br_ref[...])
    lane = lax.broadcasted_iota(I32, (rows, LANES), 1)
    work = jnp.where(lane < N_EXPERTS, logits, -jnp.inf)
    vals, idxs = [], []
    for _ in range(TOP_K):
        mx = jnp.max(work, axis=-1, keepdims=True)
        idx = jnp.min(jnp.where(work == mx, lane, LANES), axis=-1, keepdims=True)
        vals.append(mx)
        idxs.append(idx)
        work = jnp.where(lane == idx, -jnp.inf, work)
    exps = [jnp.exp(v - vals[0]) for v in vals]
    denom = exps[0] + exps[1] + exps[2] + exps[3]

    onehot = jnp.zeros((rows, LANES), F32)
    for idx in idxs:
        onehot = onehot + (lane == idx).astype(F32)
    base = _prefix_rows(onehot, rows, False) + cnt_ref[...]
    code = jnp.zeros((rows, LANES), I32)
    tw = jnp.zeros((rows, LANES), F32)
    for k in range(TOP_K):
        rank = jnp.sum(jnp.where(lane == idxs[k], base, 0.0), axis=-1, keepdims=True)
        code = jnp.where(lane == k, idxs[k] * RANK_SPAN + rank.astype(I32), code)
        tw = jnp.where(lane == k, exps[k] / denom, tw)
    tw_ref[rs, :] = tw
    if code_ref.shape[0] == CODE_ROWS:
        code_ref[:, rs] = jnp.transpose(code)[:CODE_ROWS, :]
    else:
        code_ref[rs, :] = code
    cnt_ref[...] = cnt_ref[...] + jnp.sum(onehot, axis=0, keepdims=True)


def _mixer(x, u, hist, yatt, cnt0, start_pos, weights, n_tok, row_off, prev):
    B, T, D = x.shape
    tm = min(SEQ_TILE, T)
    nt = T // tm
    row = lambda b, t: (b, t, 0)
    tok = lambda b, t: (row_off // tm + b * nt + t, 0)
    hpt = tm // HALO
    full = lambda a: pl.BlockSpec(a.shape, lambda b, t, _n=a.ndim: (0,) * _n)
    tok3 = lambda b, t: (0, row_off // tm + b * nt + t, 0)
    assert n_tok <= RANK_SPAN
    if tm % LANES == 0:
        code_spec = pl.BlockSpec((CODE_ROWS, tm), lambda b, t: (0, b * nt + t))
        code_shape = jax.ShapeDtypeStruct((CODE_ROWS, B * T), I32)
    else:
        code_spec = pl.BlockSpec((tm, LANES), lambda b, t: (b * nt + t, 0))
        code_shape = jax.ShapeDtypeStruct((B * T, LANES), I32)
    out_specs = [pl.BlockSpec((tm, D), tok), pl.BlockSpec((2, tm, PLANE), tok3), pl.BlockSpec((tm, LANES), tok),
                 code_spec, pl.BlockSpec((1, LANES), lambda b, t: (0, 0))]
    out_shape = [jax.ShapeDtypeStruct((n_tok, D), F32), jax.ShapeDtypeStruct((2, n_tok, PLANE), U32),
                 jax.ShapeDtypeStruct((n_tok, LANES), F32), code_shape, jax.ShapeDtypeStruct((1, LANES), F32)]
    n_in = 6 + len(weights)
    return pl.pallas_call(
        functools.partial(_mixer_kernel, tm=tm, start_pos=start_pos),
        grid=(B, nt),
        in_specs=[
            pl.BlockSpec((None, tm, D), row),
            pl.BlockSpec((None, tm, POOL_W), row),
            pl.BlockSpec((None, HALO, POOL_W), lambda b, t: (b, jnp.maximum(t * hpt - 1, 0), 0)),
            pl.BlockSpec((None, HALO, POOL_W), lambda b, t: (b, 0, 0)),
            pl.BlockSpec((None, tm, ATT_W), row),
            pl.BlockSpec((1, LANES), lambda b, t: (0, 0)),
        ] + [full(w) for w in weights] + [pl.BlockSpec(memory_space=pl.ANY) for _ in prev],
        out_specs=out_specs,
        out_shape=out_shape,
        scratch_shapes=[pltpu.VMEM((HALO + tm, POOL_W), F32)],
        input_output_aliases={n_in + i: i for i in range(len(prev))},
        compiler_params=pltpu.CompilerParams(
            dimension_semantics=("arbitrary", "arbitrary"), vmem_limit_bytes=VMEM_LIMIT),
    )(x, u, u, hist, yatt, cnt0, *weights, *prev)


def _sc_mesh():
    return plsc.VectorSubcoreMesh(core_axis_name="c", subcore_axis_name="s")


def _sc_scatter_rows(x, idx, n_rows):
    n = x.shape[0]

    @pl.kernel(out_type=jax.ShapeDtypeStruct((n_rows, PLANE), x.dtype), mesh=_sc_mesh(), scratch_types=[])
    def scatter(x_hbm, i_hbm, o_hbm):
        def body(x_vmem, i_vmem):
            for k in range(TOP_K):
                pltpu.sync_copy(x_vmem, o_hbm.at[i_vmem.at[k]])

        pltpu.emit_pipeline(
            body,
            grid=(n // SC_WINDOW,),
            in_specs=[pl.BlockSpec((SC_WINDOW, PLANE), index_map=lambda i: (i, 0)),
                      pl.BlockSpec((TOP_K, SC_WINDOW), index_map=lambda i: (0, i))],
            out_specs=[],
            core_axis_name=("c", "s"),
            dimension_semantics=(pltpu.PARALLEL,),
        )(x_hbm, i_hbm)

    return scatter(x, idx)


def _sc_gather_rows(y, idx):
    n = idx.shape[1]

    @pl.kernel(out_type=jax.ShapeDtypeStruct((n, PLANE), y.dtype), mesh=_sc_mesh(), scratch_types=[])
    def gather(y_hbm, i_hbm, o_hbm):
        def body(i_vmem, o_vmem):
            pltpu.sync_copy(y_hbm.at[i_vmem.at[0]], o_vmem)

        pltpu.emit_pipeline(
            body,
            grid=(n // SC_WINDOW,),
            in_specs=[pl.BlockSpec((1, SC_WINDOW), index_map=lambda i: (0, i))],
            out_specs=[pl.BlockSpec((SC_WINDOW, PLANE), index_map=lambda i: (i, 0))],
            core_axis_name=("c", "s"),
            dimension_semantics=(pltpu.PARALLEL,),
        )(i_hbm, o_hbm)

    return gather(y, idx)


def _join_planes(ref):
    return jnp.concatenate([ref[0], ref[1]], axis=1)


def _expert_kernel(be_ref, nb_ref, first_ref, xb_ref, w1_ref, b1_ref, w2_ref, b2_ref, yb_ref, w1b_ref, w2b_ref):
    del be_ref
    i = pl.program_id(0)

    @pl.when(first_ref[i] == 1)
    def _():
        w1b_ref[...] = w1_ref[...].astype(BF16)
        w2b_ref[...] = w2_ref[...].astype(BF16)

    @pl.when(i < nb_ref[0])
    def _():
        xa, xc = _unpack_halves(_join_planes(xb_ref))
        hfull = (jnp.dot(xa.astype(BF16), w1b_ref[:HALF, :], preferred_element_type=F32)
                 + jnp.dot(xc.astype(BF16), w1b_ref[HALF:, :], preferred_element_type=F32) + b1_ref[...])
        glu = jnp.minimum(hfull[:, :D_FF], SWIGLU_LIMIT)
        lin = jnp.clip(hfull[:, D_FF:], -SWIGLU_LIMIT, SWIGLU_LIMIT)
        a = glu * jax.nn.sigmoid(SWIGLU_ALPHA * glu) * (lin + 1.0)
        y = jnp.dot(a.astype(BF16), w2b_ref[...], preferred_element_type=F32) + b2_ref[...]
        packed = _pack_halves(y)
        yb_ref[0] = packed[:, :PLANE]
        yb_ref[1] = packed[:, PLANE:]

    @pl.when(i >= nb_ref[0])
    def _():
        yb_ref[...] = jnp.zeros_like(yb_ref)


def _experts(block_e, n_used, first, xb, w1, b1, w2, b2):
    n_rows = xb.shape[1]
    bm = MOE_BLOCK
    rows = pl.BlockSpec((2, bm, PLANE), lambda i, be, nb, fi: (0, i, 0))
    per_expert = lambda r, c: pl.BlockSpec((None, r, c), lambda i, be, nb, fi: (be[i], 0, 0))
    return pl.pallas_call(
        _expert_kernel,
        grid_spec=pltpu.PrefetchScalarGridSpec(
            num_scalar_prefetch=3,
            grid=(n_rows // bm,),
            in_specs=[rows, per_expert(D_MODEL, 2 * D_FF), per_expert(1, 2 * D_FF),
                      per_expert(D_FF, D_MODEL), per_expert(1, D_MODEL)],
            out_specs=rows,
            scratch_shapes=[pltpu.VMEM((D_MODEL, 2 * D_FF), BF16), pltpu.VMEM((D_FF, D_MODEL), BF16)],
        ),
        out_shape=jax.ShapeDtypeStruct((2, n_rows, PLANE), U32),
        compiler_params=pltpu.CompilerParams(
            dimension_semantics=("arbitrary",), vmem_limit_bytes=VMEM_LIMIT),
    )(block_e, n_used, first, xb, w1, b1, w2, b2)


def _combine_kernel(h_ref, tw_ref, g_ref, b_ref, y4_ref, out_ref):
    tw = tw_ref[...]
    acc_hi = None
    acc_lo = None
    for k in range(TOP_K):
        hi, lo = _unpack_halves(_join_planes(y4_ref.at[k]))
        w = tw[:, k:k + 1]
        acc_hi = w * hi if acc_hi is None else acc_hi + w * hi
        acc_lo = w * lo if acc_lo is None else acc_lo + w * lo
    moe = jnp.concatenate([acc_hi, acc_lo], axis=1)
    out_ref[...] = _layer_norm(DEEPNORM_ALPHA * h_ref[...] + moe, g_ref[...], b_ref[...])


def _combine(h, tw, ln2g, ln2b, y4, row_off, n_rows):
    D = h.shape[1]
    tn = min(SEQ_TILE, n_rows)
    off = row_off // tn
    const = lambda i: (0, 0)
    return pl.pallas_call(
        _combine_kernel,
        grid=(n_rows // tn,),
        in_specs=[
            pl.BlockSpec((tn, D), lambda i: (off + i, 0)),
            pl.BlockSpec((tn, LANES), lambda i: (off + i, 0)),
            pl.BlockSpec((1, D), const), pl.BlockSpec((1, D), const),
            pl.BlockSpec((TOP_K, 2, tn, PLANE), lambda i: (0, 0, off + i, 0)),
        ],
        out_specs=pl.BlockSpec((tn, D), lambda i: (i, 0)),
        out_shape=jax.ShapeDtypeStruct((n_rows, D), F32),
        compiler_params=pltpu.CompilerParams(
            dimension_semantics=("arbitrary",), vmem_limit_bytes=VMEM_LIMIT),
    )(h, tw, ln2g, ln2b, y4)


def kernel(x_prompt, x_sample, cache_pool, cache_k, cache_v, cache_logf, ln_in_g, ln_in_b, w_in, b_in,
           w_pool, s_pool, w_pool_proj, w_att_proj, w_out, ln1_g, ln1_b, w_router, b_router,
           w1, b1, w2, b2, ln2_g, ln2_b):
    assert w_in.shape[0] == DEPTH
    B, T, D = x_prompt.shape
    Bs, Ts, _ = x_sample.shape
    P = cache_k.shape[2]
    row2 = lambda a: a.reshape(1, -1).astype(F32)

    f_off = MAIN_W
    g_off = MAIN_W + N_HEADS
    wm = w_in[0][:, :f_off].astype(BF16)
    bm = row2(b_in[0][:f_off])
    wf = jnp.pad(w_in[0][:, f_off:g_off], ((0, 0), (0, LANES - N_HEADS))).astype(BF16)
    bf = row2(jnp.pad(b_in[0][f_off:g_off], (0, LANES - N_HEADS)))
    wg = w_in[0][:, g_off:].astype(BF16)
    bg = row2(b_in[0][g_off:])
    lng, lnb = row2(ln_in_g), row2(ln_in_b)
    wr = jnp.pad(w_router[0], ((0, 0), (0, LANES - N_EXPERTS)))
    wr_hi = wr.astype(BF16)
    wr_lo = (wr - wr_hi.astype(F32)).astype(BF16)
    br = row2(jnp.pad(b_router[0], (0, LANES - N_EXPERTS)))
    mixer_weights = (lng, lnb, wg, bg, w_pool[0].astype(BF16), row2(s_pool[0]),
                     w_pool_proj[0].astype(BF16), w_att_proj[0].astype(BF16), w_out[0].astype(BF16),
                     row2(ln1_g[0]), row2(ln1_b[0]), wr_hi, wr_lo, br)

    zeros_f = jnp.zeros((B, 1, LANES), F32)
    u_p, k_p, v_p, logf_p, qa_p, ka_p, va_p, stats = _inproj(x_prompt, zeros_f, lng, lnb, wm, bm, wf, bf)
    tq, tk = min(ATT_TQ, T), min(ATT_TK, T)
    ya_p = _attention(_attn_skip_plan(stats, tq, tk, min(SEQ_TILE, T)), qa_p, ka_p, va_p, 0, tq, tk)

    clf = jnp.pad(cache_logf[0], ((0, 0), (0, 0), (0, LANES - N_HEADS)))
    ka_c, va_c, f_tot = _cache_prep(cache_k[0].reshape(Bs, P, ATT_W), cache_v[0].reshape(Bs, P, ATT_W), clf)
    u_s, k_s, v_s, logf_s, qa_s, ka_s, va_s, _ = _inproj(x_sample, f_tot, lng, lnb, wm, bm, wf, bf)
    pad_keys = (-(P + Ts)) % LANES
    tks = P + Ts + pad_keys
    zpad = jnp.zeros((Bs, N_HEADS, pad_keys, AUG_W), BF16)
    visit_all = jnp.zeros((Bs * (N_HEADS // 2),), I32)
    ya_s = _attention(visit_all, qa_s, jnp.concatenate([ka_c, ka_s, zpad], axis=2),
                      jnp.concatenate([va_c, va_s, zpad], axis=2), P, Ts, tks)

    cnt0 = jnp.zeros((1, LANES), F32)
    hist_p = jnp.zeros((B, HALO, POOL_W), F32)
    n_tok = B * T + Bs * Ts
    *bufs, code_p, cnt_p = _mixer(x_prompt, u_p, hist_p, ya_p, cnt0, 0, mixer_weights, n_tok, 0, ())
    u_full_s = jnp.concatenate([cache_pool[0].astype(F32), u_s], axis=1)
    hist_s = jnp.pad(cache_pool[0].astype(F32), ((0, 0), (HALO - POOL_HIST, 0), (0, 0)))
    h_all, hp_all, tw, code_s, cnt = _mixer(x_sample, u_s, hist_s, ya_s, cnt_p, P, mixer_weights,
                                            n_tok, B * T, tuple(bufs))
    by_slot = lambda c: c[:TOP_K] if c.shape[0] == CODE_ROWS else c[:, :TOP_K].T
    code = jnp.concatenate([by_slot(code_p), by_slot(code_s)], axis=1)
    ti, rk = code // RANK_SPAN, code % RANK_SPAN

    counts = cnt[0, :N_EXPERTS].astype(I32)
    padded = (counts + MOE_BLOCK - 1) // MOE_BLOCK * MOE_BLOCK
    pad_ends = jnp.cumsum(padded)
    pad_starts = pad_ends - padded
    n_blocks = (n_tok * TOP_K + N_EXPERTS * (MOE_BLOCK - 1) + MOE_BLOCK - 1) // MOE_BLOCK
    n_rows = n_blocks * MOE_BLOCK
    block_start = jnp.arange(n_blocks, dtype=I32) * MOE_BLOCK
    block_e = jnp.minimum(jnp.sum((pad_ends[None, :] <= block_start[:, None]).astype(I32), axis=1), N_EXPERTS - 1)
    n_used = (pad_ends[-1:] // MOE_BLOCK).astype(I32)
    start_of = sum(jnp.where(ti == e, pad_starts[e], 0) for e in range(N_EXPERTS))
    dest_t = (start_of + rk).astype(I32)
    dest_planes = jnp.stack([dest_t, dest_t + n_rows], axis=1)

    xb = _sc_scatter_rows(hp_all.reshape(2 * n_tok, PLANE), dest_planes.reshape(TOP_K, 2 * n_tok), 2 * n_rows)
    first = jnp.concatenate([jnp.ones((1,), I32), (block_e[1:] != block_e[:-1]).astype(I32)])
    yb = _experts(block_e, n_used, first, xb.reshape(2, n_rows, PLANE),
                  w1[0], b1[0][:, None, :], w2[0], b2[0][:, None, :])
    y4 = _sc_gather_rows(yb.reshape(2 * n_rows, PLANE), dest_planes.reshape(1, TOP_K * 2 * n_tok))
    y4 = y4.reshape(TOP_K, 2, n_tok, PLANE)
    ln2g, ln2b = row2(ln2_g[0]), row2(ln2_b[0])
    y_prompt = _combine(h_all, tw, ln2g, ln2b, y4, 0, B * T).reshape(B, T, D)
    y_sample = _combine(h_all, tw, ln2g, ln2b, y4, B * T, Bs * Ts).reshape(Bs, Ts, D)
    heads = lambda a, b_, t_: a.reshape(1, b_, t_, N_HEADS, HEAD_DIM)
    return (y_prompt, y_sample,
            heads(k_p, B, T), heads(v_p, B, T), logf_p[None], u_p[:, -POOL_HIST:][None],
            heads(k_s, Bs, Ts), heads(v_s, Bs, Ts), logf_s[None], u_full_s[:, -POOL_HIST:][None])
```

```python
import functools

import jax
import jax.numpy as jnp
from jax import lax
from jax.experimental import pallas as pl
from jax.experimental.pallas import tpu as pltpu
from jax.experimental.pallas import tpu_sc as plsc

F32 = jnp.float32
BF16 = jnp.bfloat16
I32 = jnp.int32
U32 = jnp.uint32

D_MODEL = 1024
N_HEADS = 8
HEAD_DIM = 64
ATT_W = N_HEADS * HEAD_DIM
POOL_WINDOWS = (2, 4, 8, 16)
POOL_GC = 128
POOL_W = len(POOL_WINDOWS) * POOL_GC
POOL_HIST = max(POOL_WINDOWS) - 1
HALO = 16
N_EXPERTS = 32
TOP_K = 4
D_FF = D_MODEL
SWIGLU_ALPHA = 1.702
SWIGLU_LIMIT = 7.0
LN_EPS = 1e-5
DEPTH = 1
DEEPNORM_ALPHA = (2.0 * DEPTH) ** 0.25
ATT_SCALE = HEAD_DIM ** -0.5
LOG2E = 1.4426950408889634
MAIN_W = POOL_W + 3 * ATT_W
LANES = 128
AUG_W = LANES
HALF = D_MODEL // 2
VMEM_LIMIT = 56 * 1024 * 1024

SEQ_TILE = 512
ATT_TQ = 1024
ATT_TK = 1024
N_STATS = 4
SKIP_LOG2 = 150.0
MOE_BLOCK = 512
RANK_SPAN = 1 << 20
CODE_ROWS = 8
PLANE = HALF // 2
SC_WINDOW = 128


def _layer_norm(x, g, b):
    mu = jnp.mean(x, axis=-1, keepdims=True)
    xc = x - mu
    var = jnp.mean(xc * xc, axis=-1, keepdims=True)
    return xc * lax.rsqrt(var + LN_EPS) * g + b


def _split3(x):
    a = x.astype(BF16)
    r = x - a.astype(F32)
    b = r.astype(BF16)
    c = (r - b.astype(F32)).astype(BF16)
    return a, b, c


def _prefix_rows(x, tm, inclusive):
    kp = max(tm, LANES)
    r = lax.broadcasted_iota(I32, (tm, kp), 0)
    c = lax.broadcasted_iota(I32, (tm, kp), 1)
    tri = ((c <= r) if inclusive else (c < r)).astype(BF16)
    if kp > tm:
        x = jnp.concatenate([x, jnp.zeros((kp - tm, x.shape[1]), x.dtype)], axis=0)
    out = None
    for piece in _split3(x):
        y = jnp.dot(tri, piece, preferred_element_type=F32)
        out = y if out is None else out + y
    return out


def _pack_halves(y):
    hi = pltpu.bitcast(y[:, :HALF].astype(BF16).astype(F32), U32)
    lo = pltpu.bitcast(y[:, HALF:].astype(BF16).astype(F32), U32)
    return hi | (lo >> 16)


def _unpack_halves(w):
    hi = pltpu.bitcast(w & jnp.uint32(0xFFFF0000), F32)
    lo = pltpu.bitcast(w << 16, F32)
    return hi, lo


def _head_slab(p, off, h):
    s = p[:, off + (h // 2) * LANES: off + (h // 2) * LANES + LANES]
    return s if h % 2 == 0 else pltpu.roll(s, HEAD_DIM, 1)


def _augment(qs, ks, vs, f_col, lane):
    fc = jnp.broadcast_to(f_col, lane.shape)
    hi = fc.astype(BF16).astype(F32)
    r1 = fc - hi
    mid = r1.astype(BF16).astype(F32)
    lo = r1 - mid
    one = jnp.ones_like(fc)
    zero = jnp.zeros_like(fc)
    ka = jnp.where(lane < 64, ks, jnp.where(lane < 67, one, jnp.where(
        lane == 67, -hi, jnp.where(lane == 68, -mid, jnp.where(lane == 69, -lo, zero)))))
    va = jnp.where(lane < 64, vs, jnp.where(lane == 64, one, zero))
    if qs is None:
        return None, ka.astype(BF16), va.astype(BF16)
    qa = jnp.where(lane < 64, qs, jnp.where(lane == 64, hi, jnp.where(
        lane == 65, mid, jnp.where(lane == 66, lo, jnp.where(lane < 70, one, zero)))))
    return qa.astype(BF16), ka.astype(BF16), va.astype(BF16)


def _inproj_kernel(x_ref, f0_ref, lng_ref, lnb_ref, wm_ref, bm_ref, wf_ref, bf_ref,
                   u_ref, k_ref, v_ref, logf_ref, qa_ref, ka_ref, va_ref, st_ref, carry_ref, *, tm):
    t = pl.program_id(1)

    @pl.when(t == 0)
    def _():
        carry_ref[...] = f0_ref[...]

    xn = _layer_norm(x_ref[...], lng_ref[...], lnb_ref[...]).astype(BF16)
    p = jnp.dot(xn, wm_ref[...], preferred_element_type=F32) + bm_ref[...]
    fl = jnp.dot(xn, wf_ref[...], preferred_element_type=F32) + bf_ref[...]
    logf = jnp.minimum(fl, 0.0) - jnp.log(1.0 + jnp.exp(-jnp.abs(fl)))
    f_cum = _prefix_rows(logf, tm, True) + carry_ref[...]
    carry_ref[...] = f_cum[tm - 1:tm, :]

    u_ref[...] = p[:, :POOL_W]
    k_ref[...] = p[:, POOL_W + ATT_W:POOL_W + 2 * ATT_W]
    v_ref[...] = p[:, POOL_W + 2 * ATT_W:]
    logf_ref[...] = logf[:, :N_HEADS]

    lane = lax.broadcasted_iota(I32, (tm, AUG_W), 1)
    for h in range(N_HEADS):
        qs = _head_slab(p, POOL_W, h) * (ATT_SCALE * LOG2E)
        ks = _head_slab(p, POOL_W + ATT_W, h)
        vs = _head_slab(p, POOL_W + 2 * ATT_W, h)
        qa, ka, va = _augment(qs, ks, vs, f_cum[:, h:h + 1] * LOG2E, lane)
        qa_ref[h] = qa
        ka_ref[h] = ka
        va_ref[h] = va

    col = lax.broadcasted_iota(I32, (ATT_W, LANES), 0) // HEAD_DIM
    head_sum = (col == lax.broadcasted_iota(I32, (ATT_W, LANES), 1)).astype(BF16)
    qsec = (p[:, POOL_W:POOL_W + ATT_W] * (ATT_SCALE * LOG2E)).astype(BF16).astype(F32)
    ksec = p[:, POOL_W + ATT_W:POOL_W + 2 * ATT_W].astype(BF16).astype(F32)
    for r, sec in enumerate((qsec, ksec)):
        sq = jnp.dot((sec * sec).astype(BF16), head_sum, preferred_element_type=F32)
        st_ref[r:r + 1, :] = jnp.max(sq, axis=0, keepdims=True)
    st_ref[2:3, :] = f_cum[0:1, :] * LOG2E
    st_ref[3:4, :] = f_cum[tm - 1:tm, :] * LOG2E


def _inproj(x, f0, lng, lnb, wm, bm, wf, bf):
    B, T, D = x.shape
    tm = min(SEQ_TILE, T)
    grid = (B, T // tm)
    const = lambda b, t: (0, 0)
    row = lambda b, t: (b, t, 0)
    aug = pl.BlockSpec((None, N_HEADS, tm, AUG_W), lambda b, t: (b, 0, t, 0))
    aug_shape = jax.ShapeDtypeStruct((B, N_HEADS, T, AUG_W), BF16)
    return pl.pallas_call(
        functools.partial(_inproj_kernel, tm=tm),
        grid=grid,
        in_specs=[
            pl.BlockSpec((None, tm, D), row),
            pl.BlockSpec((None, 1, LANES), lambda b, t: (b, 0, 0)),
            pl.BlockSpec((1, D), const), pl.BlockSpec((1, D), const),
            pl.BlockSpec((D, MAIN_W), const), pl.BlockSpec((1, MAIN_W), const),
            pl.BlockSpec((D, LANES), const), pl.BlockSpec((1, LANES), const),
        ],
        out_specs=[
            pl.BlockSpec((None, tm, POOL_W), row),
            pl.BlockSpec((None, tm, ATT_W), row),
            pl.BlockSpec((None, tm, ATT_W), row),
            pl.BlockSpec((None, tm, N_HEADS), row),
            aug, aug, aug,
            pl.BlockSpec((None, None, N_STATS, LANES), lambda b, t: (b, t, 0, 0)),
        ],
        out_shape=[
            jax.ShapeDtypeStruct((B, T, POOL_W), F32),
            jax.ShapeDtypeStruct((B, T, ATT_W), F32),
            jax.ShapeDtypeStruct((B, T, ATT_W), F32),
            jax.ShapeDtypeStruct((B, T, N_HEADS), F32),
            aug_shape, aug_shape, aug_shape,
            jax.ShapeDtypeStruct((B, T // tm, N_STATS, LANES), F32),
        ],
        scratch_shapes=[pltpu.VMEM((1, LANES), F32)],
        compiler_params=pltpu.CompilerParams(
            dimension_semantics=("arbitrary", "arbitrary"), vmem_limit_bytes=VMEM_LIMIT),
    )(x, f0, lng, lnb, wm, bm, wf, bf)


def _cache_kernel(ck_ref, cv_ref, clf_ref, ka_ref, va_ref, ftot_ref, carry_ref, *, tp):
    t = pl.program_id(1)

    @pl.when(t == 0)
    def _():
        carry_ref[...] = jnp.zeros_like(carry_ref)

    f_cum = _prefix_rows(clf_ref[...], tp, True) + carry_ref[...]
    carry_ref[...] = f_cum[tp - 1:tp, :]
    ftot_ref[...] = f_cum[tp - 1:tp, :]
    ck = ck_ref[...]
    cv = cv_ref[...]
    lane = lax.broadcasted_iota(I32, (tp, AUG_W), 1)
    for h in range(N_HEADS):
        _, ka, va = _augment(None, _head_slab(ck, 0, h), _head_slab(cv, 0, h), f_cum[:, h:h + 1] * LOG2E, lane)
        ka_ref[h] = ka
        va_ref[h] = va


def _cache_prep(ck, cv, clf):
    B, P, _ = ck.shape
    tp = min(SEQ_TILE, P)
    row = lambda b, t: (b, t, 0)
    aug = pl.BlockSpec((None, N_HEADS, tp, AUG_W), lambda b, t: (b, 0, t, 0))
    aug_shape = jax.ShapeDtypeStruct((B, N_HEADS, P, AUG_W), BF16)
    return pl.pallas_call(
        functools.partial(_cache_kernel, tp=tp),
        grid=(B, P // tp),
        in_specs=[pl.BlockSpec((None, tp, ATT_W), row), pl.BlockSpec((None, tp, ATT_W), row),
                  pl.BlockSpec((None, tp, LANES), row)],
        out_specs=[aug, aug, pl.BlockSpec((None, 1, LANES), lambda b, t: (b, 0, 0))],
        out_shape=[aug_shape, aug_shape, jax.ShapeDtypeStruct((B, 1, LANES), F32)],
        scratch_shapes=[pltpu.VMEM((1, LANES), F32)],
        compiler_params=pltpu.CompilerParams(
            dimension_semantics=("arbitrary", "arbitrary"), vmem_limit_bytes=VMEM_LIMIT),
    )(ck, cv, clf)


def _attn_kernel(jmin_ref, qa_ref, ka_ref, va_ref, o_ref, *, tq, tk, q_off):
    b, hp, iq = pl.program_id(0), pl.program_id(1), pl.program_id(2)
    q_lo = q_off + iq * tq
    n_full = (q_lo + 1) // tk
    j_first = jmin_ref[(b * pl.num_programs(1) + hp) * pl.num_programs(2) + iq]
    lane = lax.broadcasted_iota(I32, (tq, AUG_W), 1)
    qs = (qa_ref[0], qa_ref[1])

    def visit(carry, k0, nk, r0, masked):
        new = []
        for hh in range(2):
            m_all, acc_all = carry[hh]
            m, acc, q = m_all[r0:], acc_all[r0:], qs[hh][r0:]
            k = ka_ref[hh, pl.ds(k0, nk), :]
            v = va_ref[hh, pl.ds(k0, nk), :]
            s = lax.dot_general(q, k, (((1,), (1,)), ((), ())), preferred_element_type=F32)
            if masked:
                qpos = q_lo + r0 + lax.broadcasted_iota(I32, (tq - r0, nk), 0)
                kpos = k0 + lax.broadcasted_iota(I32, (tq - r0, nk), 1)
                s = jnp.where(qpos >= kpos, s, -jnp.inf)
            m_new = jnp.maximum(m, jnp.max(s, axis=-1, keepdims=True))
            alpha = jnp.exp2(m - m_new)
            p = jnp.exp2(s - m_new)
            acc = acc * alpha + jnp.dot(p.astype(BF16), v, preferred_element_type=F32)
            if r0:
                m_new = jnp.concatenate([m_all[:r0], m_new], axis=0)
                acc = jnp.concatenate([acc_all[:r0], acc], axis=0)
            new.append((m_new, acc))
        return tuple(new)

    def step(j, carry):
        return visit(carry, pl.multiple_of(j * tk, tk), tk, 0, False)

    init = (jnp.full((tq, 1), -jnp.inf, F32), jnp.zeros((tq, AUG_W), F32))
    carry = lax.fori_loop(j_first, n_full, step, (init, init))
    k_diag = pl.multiple_of(n_full * tk, tk)
    if tq == tk and tq % (2 * LANES) == 0:
        half = tq // 2
        carry = visit(carry, k_diag, half, 0, True)
        carry = visit(carry, pl.multiple_of(k_diag + half, half), half, half, True)
    else:
        carry = visit(carry, k_diag, tk, 0, True)
    outs = [acc / acc[:, HEAD_DIM:HEAD_DIM + 1] for _, acc in carry]
    o_ref[...] = jnp.where(lane < HEAD_DIM, outs[0], pltpu.roll(outs[1], HEAD_DIM, 1)).astype(BF16)


def _attention(jmin, qa, ka, va, q_off, tq, tk):
    B, H, Tq, _ = qa.shape
    Tk = ka.shape[2]
    assert tk % tq == 0 and q_off % tq == 0 and tq > 1
    return pl.pallas_call(
        functools.partial(_attn_kernel, tq=tq, tk=tk, q_off=q_off),
        grid_spec=pltpu.PrefetchScalarGridSpec(
            num_scalar_prefetch=1,
            grid=(B, H // 2, Tq // tq),
            in_specs=[
                pl.BlockSpec((None, 2, tq, AUG_W), lambda b, hp, iq, jm: (b, hp, iq, 0)),
                pl.BlockSpec((None, 2, Tk, AUG_W), lambda b, hp, iq, jm: (b, hp, 0, 0)),
                pl.BlockSpec((None, 2, Tk, AUG_W), lambda b, hp, iq, jm: (b, hp, 0, 0)),
            ],
            out_specs=pl.BlockSpec((None, tq, 2 * HEAD_DIM), lambda b, hp, iq, jm: (b, iq, hp)),
        ),
        out_shape=jax.ShapeDtypeStruct((B, Tq, ATT_W), BF16),
        compiler_params=pltpu.CompilerParams(
            dimension_semantics=("arbitrary", "arbitrary", "arbitrary"), vmem_limit_bytes=VMEM_LIMIT),
    )(jmin, qa, ka, va)


def _attn_skip_plan(stats, tq, tk, tm):
    B, nt = stats.shape[:2]
    st = stats[..., :N_HEADS]
    per = lambda row, r: st[:, :, row].reshape(B, nt // r, r, N_HEADS)
    rq, rk = tq // tm, tk // tm
    nq, nk = nt // rq, nt // rk
    qn2, kn2, kn2_own = per(0, rq).max(2), per(1, rk).max(2), per(1, rq).max(2)
    f_first, f_last = per(2, rq)[:, :, 0], per(3, rk)[:, :, -1]
    slack = 1.01
    upper = jnp.sqrt(qn2[:, :, None] * kn2[:, None, :]) * slack + (f_first[:, :, None] - f_last[:, None, :])
    lower = -jnp.sqrt(qn2 * kn2_own) * slack
    weightless = upper - lower[:, :, None] <= -(SKIP_LOG2 + 2.0)
    j = jnp.arange(nk, dtype=I32)[None, None, :, None]
    n_full = ((jnp.arange(nq, dtype=I32) * tq + 1) // tk)[None, :, None, None]
    first = jnp.min(jnp.where(weightless | (j >= n_full), n_full, j), axis=2)
    first = jnp.min(first.reshape(B, nq, N_HEADS // 2, 2), axis=3)
    return jnp.swapaxes(first, 1, 2).reshape(-1).astype(I32)


def _mixer_kernel(x_ref, u_ref, uprev_ref, hist_ref, ya_ref, cnt0_ref,
                  lng_ref, lnb_ref, wg_ref, bg_ref, wpool_ref, spool_ref, wpp_ref, watt_ref, wout_ref,
                  ln1g_ref, ln1b_ref, wrh_ref, wrl_ref, br_ref, *rest, tm, start_pos):
    h_ref, hp_ref, tw_ref, code_ref, cnt_ref, uext_ref = rest[-6:]
    b = pl.program_id(0)
    t = pl.program_id(1)

    @pl.when((b == 0) & (t == 0))
    def _():
        cnt_ref[...] = cnt0_ref[...]

    @pl.when(t == 0)
    def _():
        uext_ref[0:HALO, :] = hist_ref[...]

    @pl.when(t > 0)
    def _():
        uext_ref[0:HALO, :] = uprev_ref[...]

    uext_ref[HALO:HALO + tm, :] = u_ref[...]
    _mixer_rows(0, tm, start_pos + t * tm, x_ref, ya_ref, lng_ref, lnb_ref, wg_ref, bg_ref, wpool_ref,
                spool_ref, wpp_ref, watt_ref, wout_ref, ln1g_ref, ln1b_ref, wrh_ref, wrl_ref, br_ref,
                h_ref, hp_ref, code_ref, tw_ref, cnt_ref, uext_ref)


def _mixer_rows(r0, rows, pos0, x_ref, ya_ref, lng_ref, lnb_ref, wg_ref, bg_ref, wpool_ref, spool_ref, wpp_ref,
                watt_ref, wout_ref, ln1g_ref, ln1b_ref, wrh_ref, wrl_ref, br_ref,
                h_ref, hp_ref, code_ref, tw_ref, cnt_ref, uext_ref):
    rs = slice(r0, r0 + rows)
    xn = _layer_norm(x_ref[rs, :], lng_ref[...], lnb_ref[...])
    g = jnp.dot(xn.astype(BF16), wg_ref[...], preferred_element_type=F32) + bg_ref[...]
    gates = jax.nn.sigmoid(g)

    pos = pos0 + r0 + lax.broadcasted_iota(I32, (rows, 1), 0)
    pooled = None
    for gi, w in enumerate(POOL_WINDOWS):
        sl = slice(gi * POOL_GC, (gi + 1) * POOL_GC)
        cur = uext_ref[HALO + r0:HALO + r0 + rows, sl]
        s = cur
        for j in range(1, w):
            s = s + uext_ref[HALO + r0 - j:HALO + r0 - j + rows, sl]
        count = jnp.minimum(pos + 1, w).astype(F32)
        d = s / count - cur
        yg = jnp.dot(d.astype(BF16), wpool_ref[gi], preferred_element_type=F32) * spool_ref[:, sl]
        contrib = jnp.dot(yg.astype(BF16), wpp_ref[sl, :], preferred_element_type=F32)
        pooled = contrib if pooled is None else pooled + contrib
    att = jnp.dot(ya_ref[rs, :], watt_ref[...], preferred_element_type=F32)
    m = gates[:, :D_MODEL] * pooled + gates[:, D_MODEL:] * att
    mix = jnp.dot(m.astype(BF16), wout_ref[...], preferred_element_type=F32)
    h = _layer_norm(DEEPNORM_ALPHA * xn + mix, ln1g_ref[...], ln1b_ref[...])
    h_ref[rs, :] = h
    packed = _pack_halves(h)
    hp_ref[0, rs, :] = packed[:, :PLANE]
    hp_ref[1, rs, :] = packed[:, PLANE:]

    hh = h.astype(BF16)
    hl = (h - hh.astype(F32)).astype(BF16)
    logits = (jnp.dot(hh, wrh_ref[...], preferred_element_type=F32)
              + jnp.dot(hl, wrh_ref[...], preferred_element_type=F32)
              + jnp.dot(hh, wrl_ref[...], preferred_element_type=F32) + br_ref[...])
    lane = lax.broadcasted_iota(I32, (rows, LANES), 1)
    work = jnp.where(lane < N_EXPERTS, logits, -jnp.inf)
    vals, idxs = [], []
    for _ in range(TOP_K):
        mx = jnp.max(work, axis=-1, keepdims=True)
        idx = jnp.min(jnp.where(work == mx, lane, LANES), axis=-1, keepdims=True)
        vals.append(mx)
        idxs.append(idx)
        work = jnp.where(lane == idx, -jnp.inf, work)
    exps = [jnp.exp(v - vals[0]) for v in vals]
    denom = exps[0] + exps[1] + exps[2] + exps[3]

    onehot = jnp.zeros((rows, LANES), F32)
    for idx in idxs:
        onehot = onehot + (lane == idx).astype(F32)
    base = _prefix_rows(onehot, rows, False) + cnt_ref[...]
    code = jnp.zeros((rows, LANES), I32)
    tw = jnp.zeros((rows, LANES), F32)
    for k in range(TOP_K):
        rank = jnp.sum(jnp.where(lane == idxs[k], base, 0.0), axis=-1, keepdims=True)
        code = jnp.where(lane == k, idxs[k] * RANK_SPAN + rank.astype(I32), code)
        tw = jnp.where(lane == k, exps[k] / denom, tw)
    tw_ref[rs, :] = tw
    if code_ref.shape[0] == CODE_ROWS:
        code_ref[:, rs] = jnp.transpose(code)[:CODE_ROWS, :]
    else:
        code_ref[rs, :] = code
    cnt_ref[...] = cnt_ref[...] + jnp.sum(onehot, axis=0, keepdims=True)


def _mixer(x, u, hist, yatt, cnt0, start_pos, weights, n_tok, row_off, prev):
    B, T, D = x.shape
    tm = min(SEQ_TILE, T)
    nt = T // tm
    row = lambda b, t: (b, t, 0)
    tok = lambda b, t: (row_off // tm + b * nt + t, 0)
    hpt = tm // HALO
    full = lambda a: pl.BlockSpec(a.shape, lambda b, t, _n=a.ndim: (0,) * _n)
    tok3 = lambda b, t: (0, row_off // tm + b * nt + t, 0)
    assert n_tok <= RANK_SPAN
    if tm % LANES == 0:
        code_spec = pl.BlockSpec((CODE_ROWS, tm), lambda b, t: (0, b * nt + t))
        code_shape = jax.ShapeDtypeStruct((CODE_ROWS, B * T), I32)
    else:
        code_spec = pl.BlockSpec((tm, LANES), lambda b, t: (b * nt + t, 0))
        code_shape = jax.ShapeDtypeStruct((B * T, LANES), I32)
    out_specs = [pl.BlockSpec((tm, D), tok), pl.BlockSpec((2, tm, PLANE), tok3), pl.BlockSpec((tm, LANES), tok),
                 code_spec, pl.BlockSpec((1, LANES), lambda b, t: (0, 0))]
    out_shape = [jax.ShapeDtypeStruct((n_tok, D), F32), jax.ShapeDtypeStruct((2, n_tok, PLANE), U32),
                 jax.ShapeDtypeStruct((n_tok, LANES), F32), code_shape, jax.ShapeDtypeStruct((1, LANES), F32)]
    n_in = 6 + len(weights)
    return pl.pallas_call(
        functools.partial(_mixer_kernel, tm=tm, start_pos=start_pos),
        grid=(B, nt),
        in_specs=[
            pl.BlockSpec((None, tm, D), row),
            pl.BlockSpec((None, tm, POOL_W), row),
            pl.BlockSpec((None, HALO, POOL_W), lambda b, t: (b, jnp.maximum(t * hpt - 1, 0), 0)),
            pl.BlockSpec((None, HALO, POOL_W), lambda b, t: (b, 0, 0)),
            pl.BlockSpec((None, tm, ATT_W), row),
            pl.BlockSpec((1, LANES), lambda b, t: (0, 0)),
        ] + [full(w) for w in weights] + [pl.BlockSpec(memory_space=pl.ANY) for _ in prev],
        out_specs=out_specs,
        out_shape=out_shape,
        scratch_shapes=[pltpu.VMEM((HALO + tm, POOL_W), F32)],
        input_output_aliases={n_in + i: i for i in range(len(prev))},
        compiler_params=pltpu.CompilerParams(
            dimension_semantics=("arbitrary", "arbitrary"), vmem_limit_bytes=VMEM_LIMIT),
    )(x, u, u, hist, yatt, cnt0, *weights, *prev)


def _sc_mesh():
    return plsc.VectorSubcoreMesh(core_axis_name="c", subcore_axis_name="s")


def _sc_scatter_rows(x, idx, n_rows):
    n = x.shape[0]

    @pl.kernel(out_type=jax.ShapeDtypeStruct((n_rows, PLANE), x.dtype), mesh=_sc_mesh(), scratch_types=[])
    def scatter(x_hbm, i_hbm, o_hbm):
        def body(x_vmem, i_vmem):
            for k in range(TOP_K):
                pltpu.sync_copy(x_vmem, o_hbm.at[i_vmem.at[k]])

        pltpu.emit_pipeline(
            body,
            grid=(n // SC_WINDOW,),
            in_specs=[pl.BlockSpec((SC_WINDOW, PLANE), index_map=lambda i: (i, 0)),
                      pl.BlockSpec((TOP_K, SC_WINDOW), index_map=lambda i: (0, i))],
            out_specs=[],
            core_axis_name=("c", "s"),
            dimension_semantics=(pltpu.PARALLEL,),
        )(x_hbm, i_hbm)

    return scatter(x, idx)


def _sc_gather_rows(y, idx):
    n = idx.shape[1]

    @pl.kernel(out_type=jax.ShapeDtypeStruct((n, PLANE), y.dtype), mesh=_sc_mesh(), scratch_types=[])
    def gather(y_hbm, i_hbm, o_hbm):
        def body(i_vmem, o_vmem):
            pltpu.sync_copy(y_hbm.at[i_vmem.at[0]], o_vmem)

        pltpu.emit_pipeline(
            body,
            grid=(n // SC_WINDOW,),
            in_specs=[pl.BlockSpec((1, SC_WINDOW), index_map=lambda i: (0, i))],
            out_specs=[pl.BlockSpec((SC_WINDOW, PLANE), index_map=lambda i: (i, 0))],
            core_axis_name=("c", "s"),
            dimension_semantics=(pltpu.PARALLEL,),
        )(i_hbm, o_hbm)

    return gather(y, idx)


def _join_planes(ref):
    return jnp.concatenate([ref[0], ref[1]], axis=1)


def _expert_kernel(be_ref, nb_ref, first_ref, xb_ref, w1_ref, b1_ref, w2_ref, b2_ref, yb_ref, w1b_ref, w2b_ref):
    del be_ref
    i = pl.program_id(0)

    @pl.when(first_ref[i] == 1)
    def _():
        w1b_ref[...] = w1_ref[...].astype(BF16)
        w2b_ref[...] = w2_ref[...].astype(BF16)

    @pl.when(i < nb_ref[0])
    def _():
        xa, xc = _unpack_halves(_join_planes(xb_ref))
        hfull = (jnp.dot(xa.astype(BF16), w1b_ref[:HALF, :], preferred_element_type=F32)
                 + jnp.dot(xc.astype(BF16), w1b_ref[HALF:, :], preferred_element_type=F32) + b1_ref[...])
        glu = jnp.minimum(hfull[:, :D_FF], SWIGLU_LIMIT)
        lin = jnp.clip(hfull[:, D_FF:], -SWIGLU_LIMIT, SWIGLU_LIMIT)
        a = glu * jax.nn.sigmoid(SWIGLU_ALPHA * glu) * (lin + 1.0)
        y = jnp.dot(a.astype(BF16), w2b_ref[...], preferred_element_type=F32) + b2_ref[...]
        packed = _pack_halves(y)
        yb_ref[0] = packed[:, :PLANE]
        yb_ref[1] = packed[:, PLANE:]

    @pl.when(i >= nb_ref[0])
    def _():
        yb_ref[...] = jnp.zeros_like(yb_ref)


def _experts(block_e, n_used, first, xb, w1, b1, w2, b2):
    n_rows = xb.shape[1]
    bm = MOE_BLOCK
    rows = pl.BlockSpec((2, bm, PLANE), lambda i, be, nb, fi: (0, i, 0))
    per_expert = lambda r, c: pl.BlockSpec((None, r, c), lambda i, be, nb, fi: (be[i], 0, 0))
    return pl.pallas_call(
        _expert_kernel,
        grid_spec=pltpu.PrefetchScalarGridSpec(
            num_scalar_prefetch=3,
            grid=(n_rows // bm,),
            in_specs=[rows, per_expert(D_MODEL, 2 * D_FF), per_expert(1, 2 * D_FF),
                      per_expert(D_FF, D_MODEL), per_expert(1, D_MODEL)],
            out_specs=rows,
            scratch_shapes=[pltpu.VMEM((D_MODEL, 2 * D_FF), BF16), pltpu.VMEM((D_FF, D_MODEL), BF16)],
        ),
        out_shape=jax.ShapeDtypeStruct((2, n_rows, PLANE), U32),
        compiler_params=pltpu.CompilerParams(
            dimension_semantics=("arbitrary",), vmem_limit_bytes=VMEM_LIMIT),
    )(block_e, n_used, first, xb, w1, b1, w2, b2)


def _combine_kernel(h_ref, tw_ref, g_ref, b_ref, y4_ref, out_ref):
    tw = tw_ref[...]
    acc_hi = None
    acc_lo = None
    for k in range(TOP_K):
        hi, lo = _unpack_halves(_join_planes(y4_ref.at[k]))
        w = tw[:, k:k + 1]
        acc_hi = w * hi if acc_hi is None else acc_hi + w * hi
        acc_lo = w * lo if acc_lo is None else acc_lo + w * lo
    moe = jnp.concatenate([acc_hi, acc_lo], axis=1)
    out_ref[...] = _layer_norm(DEEPNORM_ALPHA * h_ref[...] + moe, g_ref[...], b_ref[...])


def _combine(h, tw, ln2g, ln2b, y4, row_off, n_rows):
    D = h.shape[1]
    tn = min(SEQ_TILE, n_rows)
    off = row_off // tn
    const = lambda i: (0, 0)
    return pl.pallas_call(
        _combine_kernel,
        grid=(n_rows // tn,),
        in_specs=[
            pl.BlockSpec((tn, D), lambda i: (off + i, 0)),
            pl.BlockSpec((tn, LANES), lambda i: (off + i, 0)),
            pl.BlockSpec((1, D), const), pl.BlockSpec((1, D), const),
            pl.BlockSpec((TOP_K, 2, tn, PLANE), lambda i: (0, 0, off + i, 0)),
        ],
        out_specs=pl.BlockSpec((tn, D), lambda i: (i, 0)),
        out_shape=jax.ShapeDtypeStruct((n_rows, D), F32),
        compiler_params=pltpu.CompilerParams(
            dimension_semantics=("arbitrary",), vmem_limit_bytes=VMEM_LIMIT),
    )(h, tw, ln2g, ln2b, y4)


def kernel(x_prompt, x_sample, cache_pool, cache_k, cache_v, cache_logf, ln_in_g, ln_in_b, w_in, b_in,
           w_pool, s_pool, w_pool_proj, w_att_proj, w_out, ln1_g, ln1_b, w_router, b_router,
           w1, b1, w2, b2, ln2_g, ln2_b):
    assert w_in.shape[0] == DEPTH
    B, T, D = x_prompt.shape
    Bs, Ts, _ = x_sample.shape
    P = cache_k.shape[2]
    row2 = lambda a: a.reshape(1, -1).astype(F32)

    f_off = MAIN_W
    g_off = MAIN_W + N_HEADS
    wm = w_in[0][:, :f_off].astype(BF16)
    bm = row2(b_in[0][:f_off])
    wf = jnp.pad(w_in[0][:, f_off:g_off], ((0, 0), (0, LANES - N_HEADS))).astype(BF16)
    bf = row2(jnp.pad(b_in[0][f_off:g_off], (0, LANES - N_HEADS)))
    wg = w_in[0][:, g_off:].astype(BF16)
    bg = row2(b_in[0][g_off:])
    lng, lnb = row2(ln_in_g), row2(ln_in_b)
    wr = jnp.pad(w_router[0], ((0, 0), (0, LANES - N_EXPERTS)))
    wr_hi = wr.astype(BF16)
    wr_lo = (wr - wr_hi.astype(F32)).astype(BF16)
    br = row2(jnp.pad(b_router[0], (0, LANES - N_EXPERTS)))
    mixer_weights = (lng, lnb, wg, bg, w_pool[0].astype(BF16), row2(s_pool[0]),
                     w_pool_proj[0].astype(BF16), w_att_proj[0].astype(BF16), w_out[0].astype(BF16),
                     row2(ln1_g[0]), row2(ln1_b[0]), wr_hi, wr_lo, br)

    zeros_f = jnp.zeros((B, 1, LANES), F32)
    u_p, k_p, v_p, logf_p, qa_p, ka_p, va_p, stats = _inproj(x_prompt, zeros_f, lng, lnb, wm, bm, wf, bf)
    tq, tk = min(ATT_TQ, T), min(ATT_TK, T)
    ya_p = _attention(_attn_skip_plan(stats, tq, tk, min(SEQ_TILE, T)), qa_p, ka_p, va_p, 0, tq, tk)

    clf = jnp.pad(cache_logf[0], ((0, 0), (0, 0), (0, LANES - N_HEADS)))
    ka_c, va_c, f_tot = _cache_prep(cache_k[0].reshape(Bs, P, ATT_W), cache_v[0].reshape(Bs, P, ATT_W), clf)
    u_s, k_s, v_s, logf_s, qa_s, ka_s, va_s, _ = _inproj(x_sample, f_tot, lng, lnb, wm, bm, wf, bf)
    pad_keys = (-(P + Ts)) % LANES
    tks = P + Ts + pad_keys
    zpad = jnp.zeros((Bs, N_HEADS, pad_keys, AUG_W), BF16)
    visit_all = jnp.zeros((Bs * (N_HEADS // 2),), I32)
    ya_s = _attention(visit_all, qa_s, jnp.concatenate([ka_c, ka_s, zpad], axis=2),
                      jnp.concatenate([va_c, va_s, zpad], axis=2), P, Ts, tks)

    cnt0 = jnp.zeros((1, LANES), F32)
    hist_p = jnp.zeros((B, HALO, POOL_W), F32)
    n_tok = B * T + Bs * Ts
    *bufs, code_p, cnt_p = _mixer(x_prompt, u_p, hist_p, ya_p, cnt0, 0, mixer_weights, n_tok, 0, ())
    u_full_s = jnp.concatenate([cache_pool[0].astype(F32), u_s], axis=1)
    hist_s = jnp.pad(cache_pool[0].astype(F32), ((0, 0), (HALO - POOL_HIST, 0), (0, 0)))
    h_all, hp_all, tw, code_s, cnt = _mixer(x_sample, u_s, hist_s, ya_s, cnt_p, P, mixer_weights,
                                            n_tok, B * T, tuple(bufs))
    by_slot = lambda c: c[:TOP_K] if c.shape[0] == CODE_ROWS else c[:, :TOP_K].T
    code = jnp.concatenate([by_slot(code_p), by_slot(code_s)], axis=1)
    ti, rk = code // RANK_SPAN, code % RANK_SPAN

    counts = cnt[0, :N_EXPERTS].astype(I32)
    padded = (counts + MOE_BLOCK - 1) // MOE_BLOCK * MOE_BLOCK
    pad_ends = jnp.cumsum(padded)
    pad_starts = pad_ends - padded
    n_blocks = (n_tok * TOP_K + N_EXPERTS * (MOE_BLOCK - 1) + MOE_BLOCK - 1) // MOE_BLOCK
    n_rows = n_blocks * MOE_BLOCK
    block_start = jnp.arange(n_blocks, dtype=I32) * MOE_BLOCK
    block_e = jnp.minimum(jnp.sum((pad_ends[None, :] <= block_start[:, None]).astype(I32), axis=1), N_EXPERTS - 1)
    n_used = (pad_ends[-1:] // MOE_BLOCK).astype(I32)
    start_of = sum(jnp.where(ti == e, pad_starts[e], 0) for e in range(N_EXPERTS))
    dest_t = (start_of + rk).astype(I32)
    dest_planes = jnp.stack([dest_t, dest_t + n_rows], axis=1)

    xb = _sc_scatter_rows(hp_all.reshape(2 * n_tok, PLANE), dest_planes.reshape(TOP_K, 2 * n_tok), 2 * n_rows)
    first = jnp.concatenate([jnp.ones((1,), I32), (block_e[1:] != block_e[:-1]).astype(I32)])
    yb = _experts(block_e, n_used, first, xb.reshape(2, n_rows, PLANE),
                  w1[0], b1[0][:, None, :], w2[0], b2[0][:, None, :])
    y4 = _sc_gather_rows(yb.reshape(2 * n_rows, PLANE), dest_planes.reshape(1, TOP_K * 2 * n_tok))
    y4 = y4.reshape(TOP_K, 2, n_tok, PLANE)
    ln2g, ln2b = row2(ln2_g[0]), row2(ln2_b[0])
    y_prompt = _combine(h_all, tw, ln2g, ln2b, y4, 0, B * T).reshape(B, T, D)
    y_sample = _combine(h_all, tw, ln2g, ln2b, y4, B * T, Bs * Ts).reshape(Bs, Ts, D)
    heads = lambda a, b_, t_: a.reshape(1, b_, t_, N_HEADS, HEAD_DIM)
    return (y_prompt, y_sample,
            heads(k_p, B, T), heads(v_p, B, T), logf_p[None], u_p[:, -POOL_HIST:][None],
            heads(k_s, Bs, Ts), heads(v_s, Bs, Ts), logf_s[None], u_full_s[:, -POOL_HIST:][None])
```

```python
import functools

import jax
import jax.numpy as jnp
from jax import lax
from jax.experimental import pallas as pl
from jax.experimental.pallas import tpu as pltpu
from jax.experimental.pallas import tpu_sc as plsc

F32 = jnp.float32
BF16 = jnp.bfloat16
I32 = jnp.int32
U32 = jnp.uint32

D_MODEL = 1024
N_HEADS = 8
HEAD_DIM = 64
ATT_W = N_HEADS * HEAD_DIM
POOL_WINDOWS = (2, 4, 8, 16)
POOL_GC = 128
POOL_W = len(POOL_WINDOWS) * POOL_GC
POOL_HIST = max(POOL_WINDOWS) - 1
HALO = 16
N_EXPERTS = 32
TOP_K = 4
D_FF = D_MODEL
SWIGLU_ALPHA = 1.702
SWIGLU_LIMIT = 7.0
LN_EPS = 1e-5
DEPTH = 1
DEEPNORM_ALPHA = (2.0 * DEPTH) ** 0.25
ATT_SCALE = HEAD_DIM ** -0.5
LOG2E = 1.4426950408889634
MAIN_W = POOL_W + 3 * ATT_W
LANES = 128
AUG_W = LANES
HALF = D_MODEL // 2
VMEM_LIMIT = 56 * 1024 * 1024

SEQ_TILE = 512
ATT_TQ = 1024
ATT_TK = 1024
N_STATS = 4
SKIP_LOG2 = 150.0
MOE_BLOCK = 512
RANK_SPAN = 1 << 20
CODE_ROWS = 8
PLANE = HALF // 2
SC_WINDOW = 128


def _layer_norm(x, g, b):
    mu = jnp.mean(x, axis=-1, keepdims=True)
    xc = x - mu
    var = jnp.mean(xc * xc, axis=-1, keepdims=True)
    return xc * lax.rsqrt(var + LN_EPS) * g + b


def _split3(x):
    a = x.astype(BF16)
    r = x - a.astype(F32)
    b = r.astype(BF16)
    c = (r - b.astype(F32)).astype(BF16)
    return a, b, c


def _prefix_rows(x, tm, inclusive, small_ints=False):
    kp = max(tm, LANES)
    r = lax.broadcasted_iota(I32, (tm, kp), 0)
    c = lax.broadcasted_iota(I32, (tm, kp), 1)
    tri = ((c <= r) if inclusive else (c < r)).astype(BF16)
    if kp > tm:
        x = jnp.concatenate([x, jnp.zeros((kp - tm, x.shape[1]), x.dtype)], axis=0)
    out = None
    for piece in ((x.astype(BF16),) if small_ints else _split3(x)):
        y = jnp.dot(tri, piece, preferred_element_type=F32)
        out = y if out is None else out + y
    return out


def _pack_halves(y):
    hi = pltpu.bitcast(y[:, :HALF].astype(BF16).astype(F32), U32)
    lo = pltpu.bitcast(y[:, HALF:].astype(BF16).astype(F32), U32)
    return hi | (lo >> 16)


def _unpack_halves(w):
    hi = pltpu.bitcast(w & jnp.uint32(0xFFFF0000), F32)
    lo = pltpu.bitcast(w << 16, F32)
    return hi, lo


def _head_slab(p, off, h):
    s = p[:, off + (h // 2) * LANES: off + (h // 2) * LANES + LANES]
    return s if h % 2 == 0 else pltpu.roll(s, HEAD_DIM, 1)


def _augment(qs, ks, vs, f_col, lane):
    fc = jnp.broadcast_to(f_col, lane.shape)
    hi = fc.astype(BF16).astype(F32)
    r1 = fc - hi
    mid = r1.astype(BF16).astype(F32)
    lo = r1 - mid
    one = jnp.ones_like(fc)
    zero = jnp.zeros_like(fc)
    ka = jnp.where(lane < 64, ks, jnp.where(lane < 67, one, jnp.where(
        lane == 67, -hi, jnp.where(lane == 68, -mid, jnp.where(lane == 69, -lo, zero)))))
    va = jnp.where(lane < 64, vs, jnp.where(lane == 64, one, zero))
    if qs is None:
        return None, ka.astype(BF16), va.astype(BF16)
    qa = jnp.where(lane < 64, qs, jnp.where(lane == 64, hi, jnp.where(
        lane == 65, mid, jnp.where(lane == 66, lo, jnp.where(lane < 70, one, zero)))))
    return qa.astype(BF16), ka.astype(BF16), va.astype(BF16)


PROJ_W = MAIN_W + LANES
PROJ_CHUNKS = ((0, 512), (512, 1024), (1024, 1536), (1536, PROJ_W))


def _inproj_kernel(x_ref, xnext_ref, f0_ref, lng_ref, lnb_ref, w_ref, b_ref,
                   u_ref, k_ref, v_ref, logf_ref, qa_ref, ka_ref, va_ref, st_ref, carry_ref, proj_ref, *, tm):
    t = pl.program_id(1)
    lin = pl.program_id(0) * pl.num_programs(1) + t

    @pl.when(t == 0)
    def _():
        carry_ref[...] = f0_ref[...]

    def project(xn_bf, chunk):
        lo, hi = PROJ_CHUNKS[chunk]
        return jnp.dot(xn_bf, w_ref[:, lo:hi], preferred_element_type=F32) + b_ref[:, lo:hi]

    @pl.when(lin == 0)
    def _():
        xn = _layer_norm(x_ref[...], lng_ref[...], lnb_ref[...]).astype(BF16)
        for c, (lo, hi) in enumerate(PROJ_CHUNKS):
            proj_ref[:, lo:hi] = project(xn, c)

    p = proj_ref
    next_chunks = []
    xn_next = _layer_norm(xnext_ref[...], lng_ref[...], lnb_ref[...]).astype(BF16)
    fl = p[:, MAIN_W:]
    logf = jnp.minimum(fl, 0.0) - jnp.log(1.0 + jnp.exp(-jnp.abs(fl)))
    f_cum = _prefix_rows(logf, tm, True) + carry_ref[...]
    carry_ref[...] = f_cum[tm - 1:tm, :]

    u_ref[...] = p[:, :POOL_W]
    k_ref[...] = p[:, POOL_W + ATT_W:POOL_W + 2 * ATT_W]
    v_ref[...] = p[:, POOL_W + 2 * ATT_W:MAIN_W]
    logf_ref[...] = logf[:, :N_HEADS]

    lane = lax.broadcasted_iota(I32, (tm, AUG_W), 1)
    for h in range(N_HEADS):
        qs = _head_slab(p, POOL_W, h) * (ATT_SCALE * LOG2E)
        ks = _head_slab(p, POOL_W + ATT_W, h)
        vs = _head_slab(p, POOL_W + 2 * ATT_W, h)
        qa, ka, va = _augment(qs, ks, vs, f_cum[:, h:h + 1] * LOG2E, lane)
        qa_ref[h] = qa
        ka_ref[h] = ka
        va_ref[h] = va
        if h % 2 == 1:
            next_chunks.append(project(xn_next, h // 2))

    col = lax.broadcasted_iota(I32, (ATT_W, LANES), 0) // HEAD_DIM
    head_sum = (col == lax.broadcasted_iota(I32, (ATT_W, LANES), 1)).astype(BF16)
    qsec = (p[:, POOL_W:POOL_W + ATT_W] * (ATT_SCALE * LOG2E)).astype(BF16).astype(F32)
    ksec = p[:, POOL_W + ATT_W:POOL_W + 2 * ATT_W].astype(BF16).astype(F32)
    for r, sec in enumerate((qsec, ksec)):
        sq = jnp.dot((sec * sec).astype(BF16), head_sum, preferred_element_type=F32)
        st_ref[r:r + 1, :] = jnp.max(sq, axis=0, keepdims=True)
    st_ref[2:3, :] = f_cum[0:1, :] * LOG2E
    st_ref[3:4, :] = f_cum[tm - 1:tm, :] * LOG2E
    for (lo, hi), chunk in zip(PROJ_CHUNKS, next_chunks):
        proj_ref[:, lo:hi] = chunk


def _inproj(x, f0, lng, lnb, w, bias):
    B, T, D = x.shape
    tm = min(SEQ_TILE, T)
    nt = T // tm
    grid = (B, nt)
    const = lambda b, t: (0, 0)
    row = lambda b, t: (b, t, 0)

    def next_row(b, t):
        lin = jnp.minimum(b * nt + t + 1, B * nt - 1)
        return (lin // nt, lin % nt, 0)

    aug = pl.BlockSpec((None, N_HEADS, tm, AUG_W), lambda b, t: (b, 0, t, 0))
    aug_shape = jax.ShapeDtypeStruct((B, N_HEADS, T, AUG_W), BF16)
    return pl.pallas_call(
        functools.partial(_inproj_kernel, tm=tm),
        grid=grid,
        in_specs=[
            pl.BlockSpec((None, tm, D), row),
            pl.BlockSpec((None, tm, D), next_row),
            pl.BlockSpec((None, 1, LANES), lambda b, t: (b, 0, 0)),
            pl.BlockSpec((1, D), const), pl.BlockSpec((1, D), const),
            pl.BlockSpec((D, PROJ_W), const), pl.BlockSpec((1, PROJ_W), const),
        ],
        out_specs=[
            pl.BlockSpec((None, tm, POOL_W), row),
            pl.BlockSpec((None, tm, ATT_W), row),
            pl.BlockSpec((None, tm, ATT_W), row),
            pl.BlockSpec((None, tm, N_HEADS), row),
            aug, aug, aug,
            pl.BlockSpec((None, None, N_STATS, LANES), lambda b, t: (b, t, 0, 0)),
        ],
        out_shape=[
            jax.ShapeDtypeStruct((B, T, POOL_W), F32),
            jax.ShapeDtypeStruct((B, T, ATT_W), F32),
            jax.ShapeDtypeStruct((B, T, ATT_W), F32),
            jax.ShapeDtypeStruct((B, T, N_HEADS), F32),
            aug_shape, aug_shape, aug_shape,
            jax.ShapeDtypeStruct((B, T // tm, N_STATS, LANES), F32),
        ],
        scratch_shapes=[pltpu.VMEM((1, LANES), F32), pltpu.VMEM((tm, PROJ_W), F32)],
        compiler_params=pltpu.CompilerParams(
            dimension_semantics=("arbitrary", "arbitrary"), vmem_limit_bytes=VMEM_LIMIT),
    )(x, x, f0, lng, lnb, w, bias)


def _cache_kernel(ck_ref, cv_ref, clf_ref, ka_ref, va_ref, ftot_ref, carry_ref, *, tp):
    t = pl.program_id(1)

    @pl.when(t == 0)
    def _():
        carry_ref[...] = jnp.zeros_like(carry_ref)

    f_cum = _prefix_rows(clf_ref[...], tp, True) + carry_ref[...]
    carry_ref[...] = f_cum[tp - 1:tp, :]
    ftot_ref[...] = f_cum[tp - 1:tp, :]
    ck = ck_ref[...]
    cv = cv_ref[...]
    lane = lax.broadcasted_iota(I32, (tp, AUG_W), 1)
    for h in range(N_HEADS):
        _, ka, va = _augment(None, _head_slab(ck, 0, h), _head_slab(cv, 0, h), f_cum[:, h:h + 1] * LOG2E, lane)
        ka_ref[h] = ka
        va_ref[h] = va


def _cache_prep(ck, cv, clf):
    B, P, _ = ck.shape
    tp = min(SEQ_TILE, P)
    row = lambda b, t: (b, t, 0)
    aug = pl.BlockSpec((None, N_HEADS, tp, AUG_W), lambda b, t: (b, 0, t, 0))
    aug_shape = jax.ShapeDtypeStruct((B, N_HEADS, P, AUG_W), BF16)
    return pl.pallas_call(
        functools.partial(_cache_kernel, tp=tp),
        grid=(B, P // tp),
        in_specs=[pl.BlockSpec((None, tp, ATT_W), row), pl.BlockSpec((None, tp, ATT_W), row),
                  pl.BlockSpec((None, tp, LANES), row)],
        out_specs=[aug, aug, pl.BlockSpec((None, 1, LANES), lambda b, t: (b, 0, 0))],
        out_shape=[aug_shape, aug_shape, jax.ShapeDtypeStruct((B, 1, LANES), F32)],
        scratch_shapes=[pltpu.VMEM((1, LANES), F32)],
        compiler_params=pltpu.CompilerParams(
            dimension_semantics=("arbitrary", "arbitrary"), vmem_limit_bytes=VMEM_LIMIT),
    )(ck, cv, clf)


def _attn_kernel(jmin_ref, qa_ref, ka_ref, va_ref, o_ref, *, tq, tk, q_off):
    b, hp, iq = pl.program_id(0), pl.program_id(1), pl.program_id(2)
    q_lo = q_off + iq * tq
    n_full = (q_lo + 1) // tk
    j_first = jmin_ref[(b * pl.num_programs(1) + hp) * pl.num_programs(2) + iq]
    lane = lax.broadcasted_iota(I32, (tq, AUG_W), 1)
    qs = (qa_ref[0], qa_ref[1])

    def visit(carry, k0, nk, r0, masked):
        new = []
        for hh in range(2):
            m_all, acc_all = carry[hh]
            m, acc, q = m_all[r0:], acc_all[r0:], qs[hh][r0:]
            k = ka_ref[hh, pl.ds(k0, nk), :]
            v = va_ref[hh, pl.ds(k0, nk), :]
            s = lax.dot_general(q, k, (((1,), (1,)), ((), ())), preferred_element_type=F32)
            if masked:
                qpos = q_lo + r0 + lax.broadcasted_iota(I32, (tq - r0, nk), 0)
                kpos = k0 + lax.broadcasted_iota(I32, (tq - r0, nk), 1)
                s = jnp.where(qpos >= kpos, s, -jnp.inf)
            m_new = jnp.maximum(m, jnp.max(s, axis=-1, keepdims=True))
            alpha = jnp.exp2(m - m_new)
            p = jnp.exp2(s - m_new)
            acc = acc * alpha + jnp.dot(p.astype(BF16), v, preferred_element_type=F32)
            if r0:
                m_new = jnp.concatenate([m_all[:r0], m_new], axis=0)
                acc = jnp.concatenate([acc_all[:r0], acc], axis=0)
            new.append((m_new, acc))
        return tuple(new)

    def step(j, carry):
        return visit(carry, pl.multiple_of(j * tk, tk), tk, 0, False)

    init = (jnp.full((tq, 1), -jnp.inf, F32), jnp.zeros((tq, AUG_W), F32))
    carry = lax.fori_loop(j_first, n_full, step, (init, init))
    k_diag = pl.multiple_of(n_full * tk, tk)
    if tq == tk and tq % (2 * LANES) == 0:
        half = tq // 2
        carry = visit(carry, k_diag, half, 0, True)
        carry = visit(carry, pl.multiple_of(k_diag + half, half), half, half, True)
    else:
        carry = visit(carry, k_diag, tk, 0, True)
    outs = [acc / acc[:, HEAD_DIM:HEAD_DIM + 1] for _, acc in carry]
    o_ref[...] = jnp.where(lane < HEAD_DIM, outs[0], pltpu.roll(outs[1], HEAD_DIM, 1)).astype(BF16)


def _attention(jmin, qa, ka, va, q_off, tq, tk):
    B, H, Tq, _ = qa.shape
    Tk = ka.shape[2]
    assert tk % tq == 0 and q_off % tq == 0 and tq > 1
    return pl.pallas_call(
        functools.partial(_attn_kernel, tq=tq, tk=tk, q_off=q_off),
        grid_spec=pltpu.PrefetchScalarGridSpec(
            num_scalar_prefetch=1,
            grid=(B, H // 2, Tq // tq),
            in_specs=[
                pl.BlockSpec((None, 2, tq, AUG_W), lambda b, hp, iq, jm: (b, hp, iq, 0)),
                pl.BlockSpec((None, 2, Tk, AUG_W), lambda b, hp, iq, jm: (b, hp, 0, 0)),
                pl.BlockSpec((None, 2, Tk, AUG_W), lambda b, hp, iq, jm: (b, hp, 0, 0)),
            ],
            out_specs=pl.BlockSpec((None, tq, 2 * HEAD_DIM), lambda b, hp, iq, jm: (b, iq, hp)),
        ),
        out_shape=jax.ShapeDtypeStruct((B, Tq, ATT_W), BF16),
        compiler_params=pltpu.CompilerParams(
            dimension_semantics=("arbitrary", "arbitrary", "arbitrary"), vmem_limit_bytes=VMEM_LIMIT),
    )(jmin, qa, ka, va)


def _attn_skip_plan(stats, tq, tk, tm):
    B, nt = stats.shape[:2]
    st = stats[..., :N_HEADS]
    per = lambda row, r: st[:, :, row].reshape(B, nt // r, r, N_HEADS)
    rq, rk = tq // tm, tk // tm
    nq, nk = nt // rq, nt // rk
    qn2, kn2, kn2_own = per(0, rq).max(2), per(1, rk).max(2), per(1, rq).max(2)
    f_first, f_last = per(2, rq)[:, :, 0], per(3, rk)[:, :, -1]
    slack = 1.01
    upper = jnp.sqrt(qn2[:, :, None] * kn2[:, None, :]) * slack + (f_first[:, :, None] - f_last[:, None, :])
    lower = -jnp.sqrt(qn2 * kn2_own) * slack
    weightless = upper - lower[:, :, None] <= -(SKIP_LOG2 + 2.0)
    j = jnp.arange(nk, dtype=I32)[None, None, :, None]
    n_full = ((jnp.arange(nq, dtype=I32) * tq + 1) // tk)[None, :, None, None]
    first = jnp.min(jnp.where(weightless | (j >= n_full), n_full, j), axis=2)
    first = jnp.min(first.reshape(B, nq, N_HEADS // 2, 2), axis=3)
    return jnp.swapaxes(first, 1, 2).reshape(-1).astype(I32)


GATE_CHUNKS = 4


def _gate_chunk(xn_bf, wg_ref, bg_ref, c):
    cols = slice(c * (2 * D_MODEL // GATE_CHUNKS), (c + 1) * (2 * D_MODEL // GATE_CHUNKS))
    return jax.nn.sigmoid(jnp.dot(xn_bf, wg_ref[:, cols], preferred_element_type=F32) + bg_ref[:, cols]), cols


def _mixer_kernel(x_ref, xnext_ref, u_ref, uprev_ref, hist_ref, ya_ref, cnt0_ref,
                  lng_ref, lnb_ref, wg_ref, bg_ref, wpool_ref, spool_ref, wpp_ref, watt_ref, wout_ref,
                  ln1g_ref, ln1b_ref, wr_ref, br_ref, *rest, tm, start_pos):
    h_ref, hp_ref, tw_ref, code_ref, cnt_ref, uext_ref, gates_ref = rest[-7:]
    b = pl.program_id(0)
    t = pl.program_id(1)
    xn = _layer_norm(x_ref[...], lng_ref[...], lnb_ref[...])

    @pl.when((b == 0) & (t == 0))
    def _():
        cnt_ref[...] = cnt0_ref[...]
        for c in range(GATE_CHUNKS):
            g, cols = _gate_chunk(xn.astype(BF16), wg_ref, bg_ref, c)
            gates_ref[:, cols] = g

    @pl.when(t == 0)
    def _():
        uext_ref[0:HALO, :] = hist_ref[...]

    @pl.when(t > 0)
    def _():
        uext_ref[0:HALO, :] = uprev_ref[...]

    uext_ref[HALO:HALO + tm, :] = u_ref[...]

    pos = start_pos + t * tm + lax.broadcasted_iota(I32, (tm, 1), 0)
    groups = []
    for gi, w in enumerate(POOL_WINDOWS):
        sl = slice(gi * POOL_GC, (gi + 1) * POOL_GC)
        cur = uext_ref[HALO:HALO + tm, sl]
        s = cur
        for j in range(1, w):
            s = s + uext_ref[HALO - j:HALO - j + tm, sl]
        count = jnp.minimum(pos + 1, w).astype(F32)
        d = s / count - cur
        yg = jnp.dot(d.astype(BF16), wpool_ref[gi], preferred_element_type=F32) * spool_ref[:, sl]
        groups.append(yg.astype(BF16))
    pooled = jnp.dot(jnp.concatenate(groups, axis=1), wpp_ref[...], preferred_element_type=F32)
    att = jnp.dot(ya_ref[...], watt_ref[...], preferred_element_type=F32)
    m = gates_ref[:, :D_MODEL] * pooled + gates_ref[:, D_MODEL:] * att
    mix = jnp.dot(m.astype(BF16), wout_ref[...], preferred_element_type=F32)

    xn_next = _layer_norm(xnext_ref[...], lng_ref[...], lnb_ref[...]).astype(BF16)

    def emit_next_gates(c):
        g, cols = _gate_chunk(xn_next, wg_ref, bg_ref, c)
        gates_ref[:, cols] = g

    emit_next_gates(0)

    h = _layer_norm(DEEPNORM_ALPHA * xn + mix, ln1g_ref[...], ln1b_ref[...])
    h_ref[...] = h
    packed = _pack_halves(h)
    hp_ref[0] = packed[:, :PLANE]
    hp_ref[1] = packed[:, PLANE:]
    emit_next_gates(1)

    hh = h.astype(BF16)
    hl = (h - hh.astype(F32)).astype(BF16)
    hi_pass = jnp.dot(hh, wr_ref[...], preferred_element_type=F32)
    lo_pass = jnp.dot(hl, wr_ref[:, :LANES], preferred_element_type=F32)
    logits = hi_pass[:, :LANES] + lo_pass + hi_pass[:, LANES:] + br_ref[...]
    emit_next_gates(2)
    rows = tm
    lane = lax.broadcasted_iota(I32, (rows, LANES), 1)
    work = jnp.where(lane < N_EXPERTS, logits, -jnp.inf)
    vals, idxs = [], []
    for _ in range(TOP_K):
        mx = jnp.max(work, axis=-1, keepdims=True)
        idx = jnp.min(jnp.where(work == mx, lane, LANES), axis=-1, keepdims=True)
        vals.append(mx)
        idxs.append(idx)
        work = jnp.where(lane == idx, -jnp.inf, work)
    exps = [jnp.exp(v - vals[0]) for v in vals]
    denom = exps[0] + exps[1] + exps[2] + exps[3]
    emit_next_gates(3)

    onehot = jnp.zeros((rows, LANES), F32)
    for idx in idxs:
        onehot = onehot + (lane == idx).astype(F32)
    base = _prefix_rows(onehot, rows, False, small_ints=True) + cnt_ref[...]
    code = jnp.zeros((rows, LANES), I32)
    tw = jnp.zeros((rows, LANES), F32)
    for k in range(TOP_K):
        rank = jnp.sum(jnp.where(lane == idxs[k], base, 0.0), axis=-1, keepdims=True)
        code = jnp.where(lane == k, idxs[k] * RANK_SPAN + rank.astype(I32), code)
        tw = jnp.where(lane == k, exps[k] / denom, tw)
    tw_ref[...] = tw
    if code_ref.shape[0] == CODE_ROWS:
        code_ref[...] = jnp.transpose(code)[:CODE_ROWS, :]
    else:
        code_ref[...] = code
    cnt_ref[...] = cnt_ref[...] + jnp.sum(onehot, axis=0, keepdims=True)


def _mixer(x, u, hist, yatt, cnt0, start_pos, weights, n_tok, row_off, prev):
    B, T, D = x.shape
    tm = min(SEQ_TILE, T)
    nt = T // tm
    row = lambda b, t: (b, t, 0)
    tok = lambda b, t: (row_off // tm + b * nt + t, 0)
    hpt = tm // HALO
    full = lambda a: pl.BlockSpec(a.shape, lambda b, t, _n=a.ndim: (0,) * _n)
    tok3 = lambda b, t: (0, row_off // tm + b * nt + t, 0)
    assert n_tok <= RANK_SPAN
    if tm % LANES == 0:
        code_spec = pl.BlockSpec((CODE_ROWS, tm), lambda b, t: (0, b * nt + t))
        code_shape = jax.ShapeDtypeStruct((CODE_ROWS, B * T), I32)
    else:
        code_spec = pl.BlockSpec((tm, LANES), lambda b, t: (b * nt + t, 0))
        code_shape = jax.ShapeDtypeStruct((B * T, LANES), I32)
    out_specs = [pl.BlockSpec((tm, D), tok), pl.BlockSpec((2, tm, PLANE), tok3), pl.BlockSpec((tm, LANES), tok),
                 code_spec, pl.BlockSpec((1, LANES), lambda b, t: (0, 0))]
    out_shape = [jax.ShapeDtypeStruct((n_tok, D), F32), jax.ShapeDtypeStruct((2, n_tok, PLANE), U32),
                 jax.ShapeDtypeStruct((n_tok, LANES), F32), code_shape, jax.ShapeDtypeStruct((1, LANES), F32)]
    n_in = 7 + len(weights)

    def next_row(b, t):
        lin = jnp.minimum(b * nt + t + 1, B * nt - 1)
        return (lin // nt, lin % nt, 0)

    return pl.pallas_call(
        functools.partial(_mixer_kernel, tm=tm, start_pos=start_pos),
        grid=(B, nt),
        in_specs=[
            pl.BlockSpec((None, tm, D), row),
            pl.BlockSpec((None, tm, D), next_row),
            pl.BlockSpec((None, tm, POOL_W), row),
            pl.BlockSpec((None, HALO, POOL_W), lambda b, t: (b, jnp.maximum(t * hpt - 1, 0), 0)),
            pl.BlockSpec((None, HALO, POOL_W), lambda b, t: (b, 0, 0)),
            pl.BlockSpec((None, tm, ATT_W), row),
            pl.BlockSpec((1, LANES), lambda b, t: (0, 0)),
        ] + [full(w) for w in weights] + [pl.BlockSpec(memory_space=pl.ANY) for _ in prev],
        out_specs=out_specs,
        out_shape=out_shape,
        scratch_shapes=[pltpu.VMEM((HALO + tm, POOL_W), F32), pltpu.VMEM((tm, 2 * D), F32)],
        input_output_aliases={n_in + i: i for i in range(len(prev))},
        compiler_params=pltpu.CompilerParams(
            dimension_semantics=("arbitrary", "arbitrary"), vmem_limit_bytes=VMEM_LIMIT),
    )(x, x, u, u, hist, yatt, cnt0, *weights, *prev)


def _sc_mesh():
    return plsc.VectorSubcoreMesh(core_axis_name="c", subcore_axis_name="s")


def _sc_scatter_rows(x, idx, n_rows):
    n = x.shape[0]

    @pl.kernel(out_type=jax.ShapeDtypeStruct((n_rows, PLANE), x.dtype), mesh=_sc_mesh(), scratch_types=[])
    def scatter(x_hbm, i_hbm, o_hbm):
        def body(x_vmem, i_vmem):
            for k in range(TOP_K):
                pltpu.sync_copy(x_vmem, o_hbm.at[i_vmem.at[k]])

        pltpu.emit_pipeline(
            body,
            grid=(n // SC_WINDOW,),
            in_specs=[pl.BlockSpec((SC_WINDOW, PLANE), index_map=lambda i: (i, 0)),
                      pl.BlockSpec((TOP_K, SC_WINDOW), index_map=lambda i: (0, i))],
            out_specs=[],
            core_axis_name=("c", "s"),
            dimension_semantics=(pltpu.PARALLEL,),
        )(x_hbm, i_hbm)

    return scatter(x, idx)


def _sc_gather_rows(y, idx):
    n = idx.shape[1]

    @pl.kernel(out_type=jax.ShapeDtypeStruct((n, PLANE), y.dtype), mesh=_sc_mesh(), scratch_types=[])
    def gather(y_hbm, i_hbm, o_hbm):
        def body(i_vmem, o_vmem):
            pltpu.sync_copy(y_hbm.at[i_vmem.at[0]], o_vmem)

        pltpu.emit_pipeline(
            body,
            grid=(n // SC_WINDOW,),
            in_specs=[pl.BlockSpec((1, SC_WINDOW), index_map=lambda i: (0, i))],
            out_specs=[pl.BlockSpec((SC_WINDOW, PLANE), index_map=lambda i: (i, 0))],
            core_axis_name=("c", "s"),
            dimension_semantics=(pltpu.PARALLEL,),
        )(i_hbm, o_hbm)

    return gather(y, idx)


def _join_planes(ref):
    return jnp.concatenate([ref[0], ref[1]], axis=1)


def _expert_kernel(be_ref, nb_ref, first_ref, xb_ref, w1_ref, b1_ref, w2_ref, b2_ref, yb_ref, w1b_ref, w2b_ref):
    del be_ref
    i = pl.program_id(0)

    @pl.when(first_ref[i] == 1)
    def _():
        w1b_ref[...] = w1_ref[...].astype(BF16)
        w2b_ref[...] = w2_ref[...].astype(BF16)

    @pl.when(i < nb_ref[0])
    def _():
        xa, xc = _unpack_halves(_join_planes(xb_ref))
        hfull = (jnp.dot(xa.astype(BF16), w1b_ref[:HALF, :], preferred_element_type=F32)
                 + jnp.dot(xc.astype(BF16), w1b_ref[HALF:, :], preferred_element_type=F32) + b1_ref[...])
        glu = jnp.minimum(hfull[:, :D_FF], SWIGLU_LIMIT)
        lin = jnp.clip(hfull[:, D_FF:], -SWIGLU_LIMIT, SWIGLU_LIMIT)
        a = glu * jax.nn.sigmoid(SWIGLU_ALPHA * glu) * (lin + 1.0)
        y = jnp.dot(a.astype(BF16), w2b_ref[...], preferred_element_type=F32) + b2_ref[...]
        packed = _pack_halves(y)
        yb_ref[0] = packed[:, :PLANE]
        yb_ref[1] = packed[:, PLANE:]

    @pl.when(i >= nb_ref[0])
    def _():
        yb_ref[...] = jnp.zeros_like(yb_ref)


def _experts(block_e, n_used, first, xb, w1, b1, w2, b2):
    n_rows = xb.shape[1]
    bm = MOE_BLOCK
    rows = pl.BlockSpec((2, bm, PLANE), lambda i, be, nb, fi: (0, i, 0))
    per_expert = lambda r, c: pl.BlockSpec((None, r, c), lambda i, be, nb, fi: (be[i], 0, 0))
    return pl.pallas_call(
        _expert_kernel,
        grid_spec=pltpu.PrefetchScalarGridSpec(
            num_scalar_prefetch=3,
            grid=(n_rows // bm,),
            in_specs=[rows, per_expert(D_MODEL, 2 * D_FF), per_expert(1, 2 * D_FF),
                      per_expert(D_FF, D_MODEL), per_expert(1, D_MODEL)],
            out_specs=rows,
            scratch_shapes=[pltpu.VMEM((D_MODEL, 2 * D_FF), BF16), pltpu.VMEM((D_FF, D_MODEL), BF16)],
        ),
        out_shape=jax.ShapeDtypeStruct((2, n_rows, PLANE), U32),
        compiler_params=pltpu.CompilerParams(
            dimension_semantics=("arbitrary",), vmem_limit_bytes=VMEM_LIMIT),
    )(block_e, n_used, first, xb, w1, b1, w2, b2)


def _combine_kernel(h_ref, tw_ref, g_ref, b_ref, y4_ref, out_ref):
    tw = tw_ref[...]
    acc_hi = None
    acc_lo = None
    for k in range(TOP_K):
        hi, lo = _unpack_halves(_join_planes(y4_ref.at[k]))
        w = tw[:, k:k + 1]
        acc_hi = w * hi if acc_hi is None else acc_hi + w * hi
        acc_lo = w * lo if acc_lo is None else acc_lo + w * lo
    moe = jnp.concatenate([acc_hi, acc_lo], axis=1)
    out_ref[...] = _layer_norm(DEEPNORM_ALPHA * h_ref[...] + moe, g_ref[...], b_ref[...])


def _combine(h, tw, ln2g, ln2b, y4, row_off, n_rows):
    D = h.shape[1]
    tn = min(SEQ_TILE, n_rows)
    off = row_off // tn
    const = lambda i: (0, 0)
    return pl.pallas_call(
        _combine_kernel,
        grid=(n_rows // tn,),
        in_specs=[
            pl.BlockSpec((tn, D), lambda i: (off + i, 0)),
            pl.BlockSpec((tn, LANES), lambda i: (off + i, 0)),
            pl.BlockSpec((1, D), const), pl.BlockSpec((1, D), const),
            pl.BlockSpec((TOP_K, 2, tn, PLANE), lambda i: (0, 0, off + i, 0)),
        ],
        out_specs=pl.BlockSpec((tn, D), lambda i: (i, 0)),
        out_shape=jax.ShapeDtypeStruct((n_rows, D), F32),
        compiler_params=pltpu.CompilerParams(
            dimension_semantics=("arbitrary",), vmem_limit_bytes=VMEM_LIMIT),
    )(h, tw, ln2g, ln2b, y4)


def kernel(x_prompt, x_sample, cache_pool, cache_k, cache_v, cache_logf, ln_in_g, ln_in_b, w_in, b_in,
           w_pool, s_pool, w_pool_proj, w_att_proj, w_out, ln1_g, ln1_b, w_router, b_router,
           w1, b1, w2, b2, ln2_g, ln2_b):
    assert w_in.shape[0] == DEPTH
    B, T, D = x_prompt.shape
    Bs, Ts, _ = x_sample.shape
    P = cache_k.shape[2]
    row2 = lambda a: a.reshape(1, -1).astype(F32)

    f_off = MAIN_W
    g_off = MAIN_W + N_HEADS
    w_proj = jnp.pad(w_in[0][:, :g_off], ((0, 0), (0, LANES - N_HEADS))).astype(BF16)
    b_proj = row2(jnp.pad(b_in[0][:g_off], (0, LANES - N_HEADS)))
    wg = w_in[0][:, g_off:].astype(BF16)
    bg = row2(b_in[0][g_off:])
    lng, lnb = row2(ln_in_g), row2(ln_in_b)
    wr = jnp.pad(w_router[0], ((0, 0), (0, LANES - N_EXPERTS)))
    wr_hi = wr.astype(BF16)
    wr_lo = (wr - wr_hi.astype(F32)).astype(BF16)
    br = row2(jnp.pad(b_router[0], (0, LANES - N_EXPERTS)))
    mixer_weights = (lng, lnb, wg, bg, w_pool[0].astype(BF16), row2(s_pool[0]),
                     w_pool_proj[0].astype(BF16), w_att_proj[0].astype(BF16), w_out[0].astype(BF16),
                     row2(ln1_g[0]), row2(ln1_b[0]), jnp.concatenate([wr_hi, wr_lo], axis=1), br)

    zeros_f = jnp.zeros((B, 1, LANES), F32)
    u_p, k_p, v_p, logf_p, qa_p, ka_p, va_p, stats = _inproj(x_prompt, zeros_f, lng, lnb, w_proj, b_proj)
    tq, tk = min(ATT_TQ, T), min(ATT_TK, T)
    ya_p = _attention(_attn_skip_plan(stats, tq, tk, min(SEQ_TILE, T)), qa_p, ka_p, va_p, 0, tq, tk)

    clf = jnp.pad(cache_logf[0], ((0, 0), (0, 0), (0, LANES - N_HEADS)))
    ka_c, va_c, f_tot = _cache_prep(cache_k[0].reshape(Bs, P, ATT_W), cache_v[0].reshape(Bs, P, ATT_W), clf)
    u_s, k_s, v_s, logf_s, qa_s, ka_s, va_s, _ = _inproj(x_sample, f_tot, lng, lnb, w_proj, b_proj)
    pad_keys = (-(P + Ts)) % LANES
    tks = P + Ts + pad_keys
    zpad = jnp.zeros((Bs, N_HEADS, pad_keys, AUG_W), BF16)
    visit_all = jnp.zeros((Bs * (N_HEADS // 2),), I32)
    ya_s = _attention(visit_all, qa_s, jnp.concatenate([ka_c, ka_s, zpad], axis=2),
                      jnp.concatenate([va_c, va_s, zpad], axis=2), P, Ts, tks)

    cnt0 = jnp.zeros((1, LANES), F32)
    hist_p = jnp.zeros((B, HALO, POOL_W), F32)
    n_tok = B * T + Bs * Ts
    *bufs, code_p, cnt_p = _mixer(x_prompt, u_p, hist_p, ya_p, cnt0, 0, mixer_weights, n_tok, 0, ())
    u_full_s = jnp.concatenate([cache_pool[0].astype(F32), u_s], axis=1)
    hist_s = jnp.pad(cache_pool[0].astype(F32), ((0, 0), (HALO - POOL_HIST, 0), (0, 0)))
    h_all, hp_all, tw, code_s, cnt = _mixer(x_sample, u_s, hist_s, ya_s, cnt_p, P, mixer_weights,
                                            n_tok, B * T, tuple(bufs))
    by_slot = lambda c: c[:TOP_K] if c.shape[0] == CODE_ROWS else c[:, :TOP_K].T
    code = jnp.concatenate([by_slot(code_p), by_slot(code_s)], axis=1)
    ti, rk = code // RANK_SPAN, code % RANK_SPAN

    counts = cnt[0, :N_EXPERTS].astype(I32)
    padded = (counts + MOE_BLOCK - 1) // MOE_BLOCK * MOE_BLOCK
    pad_ends = jnp.cumsum(padded)
    pad_starts = pad_ends - padded
    n_blocks = (n_tok * TOP_K + N_EXPERTS * (MOE_BLOCK - 1) + MOE_BLOCK - 1) // MOE_BLOCK
    n_rows = n_blocks * MOE_BLOCK
    block_start = jnp.arange(n_blocks, dtype=I32) * MOE_BLOCK
    block_e = jnp.minimum(jnp.sum((pad_ends[None, :] <= block_start[:, None]).astype(I32), axis=1), N_EXPERTS - 1)
    n_used = (pad_ends[-1:] // MOE_BLOCK).astype(I32)
    start_of = sum(jnp.where(ti == e, pad_starts[e], 0) for e in range(N_EXPERTS))
    dest_t = (start_of + rk).astype(I32)
    dest_planes = jnp.stack([dest_t, dest_t + n_rows], axis=1)

    xb = _sc_scatter_rows(hp_all.reshape(2 * n_tok, PLANE), dest_planes.reshape(TOP_K, 2 * n_tok), 2 * n_rows)
    first = jnp.concatenate([jnp.ones((1,), I32), (block_e[1:] != block_e[:-1]).astype(I32)])
    yb = _experts(block_e, n_used, first, xb.reshape(2, n_rows, PLANE),
                  w1[0], b1[0][:, None, :], w2[0], b2[0][:, None, :])
    y4 = _sc_gather_rows(yb.reshape(2 * n_rows, PLANE), dest_planes.reshape(1, TOP_K * 2 * n_tok))
    y4 = y4.reshape(TOP_K, 2, n_tok, PLANE)
    ln2g, ln2b = row2(ln2_g[0]), row2(ln2_b[0])
    y_prompt = _combine(h_all, tw, ln2g, ln2b, y4, 0, B * T).reshape(B, T, D)
    y_sample = _combine(h_all, tw, ln2g, ln2b, y4, B * T, Bs * Ts).reshape(Bs, Ts, D)
    heads = lambda a, b_, t_: a.reshape(1, b_, t_, N_HEADS, HEAD_DIM)
    return (y_prompt, y_sample,
            heads(k_p, B, T), heads(v_p, B, T), logf_p[None], u_p[:, -POOL_HIST:][None],
            heads(k_s, Bs, Ts), heads(v_s, Bs, Ts), logf_s[None], u_full_s[:, -POOL_HIST:][None])
```

```python
import functools

import jax
import jax.numpy as jnp
from jax import lax
from jax.experimental import pallas as pl
from jax.experimental.pallas import tpu as pltpu
from jax.experimental.pallas import tpu_sc as plsc

F32 = jnp.float32
BF16 = jnp.bfloat16
I32 = jnp.int32
U32 = jnp.uint32

D_MODEL = 1024
N_HEADS = 8
HEAD_DIM = 64
ATT_W = N_HEADS * HEAD_DIM
POOL_WINDOWS = (2, 4, 8, 16)
POOL_GC = 128
POOL_W = len(POOL_WINDOWS) * POOL_GC
POOL_HIST = max(POOL_WINDOWS) - 1
HALO = 16
N_EXPERTS = 32
TOP_K = 4
D_FF = D_MODEL
SWIGLU_ALPHA = 1.702
SWIGLU_LIMIT = 7.0
LN_EPS = 1e-5
DEPTH = 1
DEEPNORM_ALPHA = (2.0 * DEPTH) ** 0.25
ATT_SCALE = HEAD_DIM ** -0.5
LOG2E = 1.4426950408889634
MAIN_W = POOL_W + 3 * ATT_W
LANES = 128
AUG_W = LANES
HALF = D_MODEL // 2
VMEM_LIMIT = 56 * 1024 * 1024

SEQ_TILE = 512
ATT_TQ = 1024
ATT_TK = 1024
N_STATS = 4
SKIP_LOG2 = 150.0
MOE_BLOCK = 512
RANK_SPAN = 1 << 20
CODE_ROWS = 8
PLANE = HALF // 2
SC_WINDOW = 128


def _layer_norm(x, g, b):
    mu = jnp.mean(x, axis=-1, keepdims=True)
    xc = x - mu
    var = jnp.mean(xc * xc, axis=-1, keepdims=True)
    return xc * lax.rsqrt(var + LN_EPS) * g + b


def _split3(x):
    a = x.astype(BF16)
    r = x - a.astype(F32)
    b = r.astype(BF16)
    c = (r - b.astype(F32)).astype(BF16)
    return a, b, c


def _prefix_rows(x, tm, inclusive, small_ints=False):
    kp = max(tm, LANES)
    r = lax.broadcasted_iota(I32, (tm, kp), 0)
    c = lax.broadcasted_iota(I32, (tm, kp), 1)
    tri = ((c <= r) if inclusive else (c < r)).astype(BF16)
    if kp > tm:
        x = jnp.concatenate([x, jnp.zeros((kp - tm, x.shape[1]), x.dtype)], axis=0)
    out = None
    for piece in ((x.astype(BF16),) if small_ints else _split3(x)):
        y = jnp.dot(tri, piece, preferred_element_type=F32)
        out = y if out is None else out + y
    return out


def _pack_halves(y):
    hi = pltpu.bitcast(y[:, :HALF].astype(BF16).astype(F32), U32)
    lo = pltpu.bitcast(y[:, HALF:].astype(BF16).astype(F32), U32)
    return hi | (lo >> 16)


def _unpack_halves(w):
    hi = pltpu.bitcast(w & jnp.uint32(0xFFFF0000), F32)
    lo = pltpu.bitcast(w << 16, F32)
    return hi, lo


def _head_slab(p, off, h):
    s = p[:, off + (h // 2) * LANES: off + (h // 2) * LANES + LANES]
    return s if h % 2 == 0 else pltpu.roll(s, HEAD_DIM, 1)


def _augment(qs, ks, vs, f_col, lane):
    fc = jnp.broadcast_to(f_col, lane.shape)
    hi = fc.astype(BF16).astype(F32)
    r1 = fc - hi
    mid = r1.astype(BF16).astype(F32)
    lo = r1 - mid
    one = jnp.ones_like(fc)
    zero = jnp.zeros_like(fc)
    ka = jnp.where(lane < 64, ks, jnp.where(lane < 67, one, jnp.where(
        lane == 67, -hi, jnp.where(lane == 68, -mid, jnp.where(lane == 69, -lo, zero)))))
    va = jnp.where(lane < 64, vs, jnp.where(lane == 64, one, zero))
    if qs is None:
        return None, ka.astype(BF16), va.astype(BF16)
    qa = jnp.where(lane < 64, qs, jnp.where(lane == 64, hi, jnp.where(
        lane == 65, mid, jnp.where(lane == 66, lo, jnp.where(lane < 70, one, zero)))))
    return qa.astype(BF16), ka.astype(BF16), va.astype(BF16)


PROJ_W = MAIN_W + LANES
PROJ_CHUNKS = ((0, 512), (512, 1024), (1024, 1536), (1536, PROJ_W))


def _inproj_kernel(x_ref, xnext_ref, f0_ref, lng_ref, lnb_ref, w_ref, b_ref, *rest, tm):
    u_ref, k_ref, v_ref, logf_ref, qa_ref, ka_ref, va_ref, st_ref, carry_ref, proj_ref = rest[-10:]
    t = pl.program_id(1)
    lin = pl.program_id(0) * pl.num_programs(1) + t

    @pl.when(t == 0)
    def _():
        carry_ref[...] = f0_ref[...]

    def project(xn_bf, chunk):
        lo, hi = PROJ_CHUNKS[chunk]
        return jnp.dot(xn_bf, w_ref[:, lo:hi], preferred_element_type=F32) + b_ref[:, lo:hi]

    @pl.when(lin == 0)
    def _():
        xn = _layer_norm(x_ref[...], lng_ref[...], lnb_ref[...]).astype(BF16)
        for c, (lo, hi) in enumerate(PROJ_CHUNKS):
            proj_ref[:, lo:hi] = project(xn, c)

    p = proj_ref
    next_chunks = []
    xn_next = _layer_norm(xnext_ref[...], lng_ref[...], lnb_ref[...]).astype(BF16)
    fl = p[:, MAIN_W:]
    logf = jnp.minimum(fl, 0.0) - jnp.log(1.0 + jnp.exp(-jnp.abs(fl)))
    f_cum = _prefix_rows(logf, tm, True) + carry_ref[...]
    carry_ref[...] = f_cum[tm - 1:tm, :]

    u_ref[...] = p[:, :POOL_W]
    k_ref[...] = p[:, POOL_W + ATT_W:POOL_W + 2 * ATT_W]
    v_ref[...] = p[:, POOL_W + 2 * ATT_W:MAIN_W]
    logf_ref[...] = logf[:, :N_HEADS]

    lane = lax.broadcasted_iota(I32, (tm, AUG_W), 1)
    for h in range(N_HEADS):
        qs = _head_slab(p, POOL_W, h) * (ATT_SCALE * LOG2E)
        ks = _head_slab(p, POOL_W + ATT_W, h)
        vs = _head_slab(p, POOL_W + 2 * ATT_W, h)
        qa, ka, va = _augment(qs, ks, vs, f_cum[:, h:h + 1] * LOG2E, lane)
        qa_ref[h] = qa
        ka_ref[h] = ka
        va_ref[h] = va
        if h % 2 == 1:
            next_chunks.append(project(xn_next, h // 2))

    col = lax.broadcasted_iota(I32, (ATT_W, LANES), 0) // HEAD_DIM
    head_sum = (col == lax.broadcasted_iota(I32, (ATT_W, LANES), 1)).astype(BF16)
    qsec = (p[:, POOL_W:POOL_W + ATT_W] * (ATT_SCALE * LOG2E)).astype(BF16).astype(F32)
    ksec = p[:, POOL_W + ATT_W:POOL_W + 2 * ATT_W].astype(BF16).astype(F32)
    for r, sec in enumerate((qsec, ksec)):
        sq = jnp.dot((sec * sec).astype(BF16), head_sum, preferred_element_type=F32)
        st_ref[r:r + 1, :] = jnp.max(sq, axis=0, keepdims=True)
    st_ref[2:3, :] = f_cum[0:1, :] * LOG2E
    st_ref[3:4, :] = f_cum[tm - 1:tm, :] * LOG2E
    for (lo, hi), chunk in zip(PROJ_CHUNKS, next_chunks):
        proj_ref[:, lo:hi] = chunk


def _inproj(x, f0, lng, lnb, w, bias, kv_prev=(), kv_row_off=0):
    B, T, D = x.shape
    tm = min(SEQ_TILE, T)
    nt = T // tm
    grid = (B, nt)
    const = lambda b, t: (0, 0)
    row = lambda b, t: (b, t, 0)

    def next_row(b, t):
        lin = jnp.minimum(b * nt + t + 1, B * nt - 1)
        return (lin // nt, lin % nt, 0)

    aug = pl.BlockSpec((None, N_HEADS, tm, AUG_W), lambda b, t: (b, 0, t, 0))
    aug_shape = jax.ShapeDtypeStruct((B, N_HEADS, T, AUG_W), BF16)
    kv = pl.BlockSpec((None, N_HEADS, tm, AUG_W), lambda b, t: (b, 0, kv_row_off // tm + t, 0))
    kv_shape = jax.ShapeDtypeStruct(kv_prev[0].shape, BF16) if kv_prev else aug_shape
    n_in = 7
    return pl.pallas_call(
        functools.partial(_inproj_kernel, tm=tm),
        grid=grid,
        in_specs=[
            pl.BlockSpec((None, tm, D), row),
            pl.BlockSpec((None, tm, D), next_row),
            pl.BlockSpec((None, 1, LANES), lambda b, t: (b, 0, 0)),
            pl.BlockSpec((1, D), const), pl.BlockSpec((1, D), const),
            pl.BlockSpec((D, PROJ_W), const), pl.BlockSpec((1, PROJ_W), const),
        ] + [pl.BlockSpec(memory_space=pl.ANY) for _ in kv_prev],
        out_specs=[
            pl.BlockSpec((None, tm, POOL_W), row),
            pl.BlockSpec((None, tm, ATT_W), row),
            pl.BlockSpec((None, tm, ATT_W), row),
            pl.BlockSpec((None, tm, N_HEADS), row),
            aug, kv, kv,
            pl.BlockSpec((None, None, N_STATS, LANES), lambda b, t: (b, t, 0, 0)),
        ],
        out_shape=[
            jax.ShapeDtypeStruct((B, T, POOL_W), F32),
            jax.ShapeDtypeStruct((B, T, ATT_W), F32),
            jax.ShapeDtypeStruct((B, T, ATT_W), F32),
            jax.ShapeDtypeStruct((B, T, N_HEADS), F32),
            aug_shape, kv_shape, kv_shape,
            jax.ShapeDtypeStruct((B, T // tm, N_STATS, LANES), F32),
        ],
        scratch_shapes=[pltpu.VMEM((1, LANES), F32), pltpu.VMEM((tm, PROJ_W), F32)],
        input_output_aliases={n_in + i: 5 + i for i in range(len(kv_prev))},
        compiler_params=pltpu.CompilerParams(
            dimension_semantics=("arbitrary", "arbitrary"), vmem_limit_bytes=VMEM_LIMIT),
    )(x, x, f0, lng, lnb, w, bias, *kv_prev)


def _cache_kernel(ck_ref, cv_ref, clf_ref, ka_ref, va_ref, ftot_ref, carry_ref, *, tp):
    t = pl.program_id(1)

    @pl.when(t == 0)
    def _():
        carry_ref[...] = jnp.zeros_like(carry_ref)

    @pl.when(t < pl.num_programs(1) - 1)
    def _():
        f_cum = _prefix_rows(clf_ref[...], tp, True) + carry_ref[...]
        carry_ref[...] = f_cum[tp - 1:tp, :]
        ftot_ref[...] = f_cum[tp - 1:tp, :]
        ck = ck_ref[...].astype(F32)
        cv = cv_ref[...].astype(F32)
        lane = lax.broadcasted_iota(I32, (tp, AUG_W), 1)
        for h in range(N_HEADS):
            _, ka, va = _augment(None, _head_slab(ck, 0, h), _head_slab(cv, 0, h), f_cum[:, h:h + 1] * LOG2E, lane)
            ka_ref[h] = ka
            va_ref[h] = va

    @pl.when(t == pl.num_programs(1) - 1)
    def _():
        ka_ref[...] = jnp.zeros_like(ka_ref)
        va_ref[...] = jnp.zeros_like(va_ref)


def _cache_prep(ck, cv, clf, n_keys):
    B, P, _ = ck.shape
    tp = min(SEQ_TILE, P)
    nt = P // tp
    row = lambda b, t: (b, jnp.minimum(t, nt - 1), 0)
    aug = pl.BlockSpec((None, N_HEADS, tp, AUG_W), lambda b, t: (b, 0, t, 0))
    aug_shape = jax.ShapeDtypeStruct((B, N_HEADS, n_keys, AUG_W), BF16)
    assert P < n_keys <= P + tp
    return pl.pallas_call(
        functools.partial(_cache_kernel, tp=tp),
        grid=(B, nt + 1),
        in_specs=[pl.BlockSpec((None, tp, ATT_W), row), pl.BlockSpec((None, tp, ATT_W), row),
                  pl.BlockSpec((None, tp, LANES), row)],
        out_specs=[aug, aug, pl.BlockSpec((None, 1, LANES), lambda b, t: (b, 0, 0))],
        out_shape=[aug_shape, aug_shape, jax.ShapeDtypeStruct((B, 1, LANES), F32)],
        scratch_shapes=[pltpu.VMEM((1, LANES), F32)],
        compiler_params=pltpu.CompilerParams(
            dimension_semantics=("arbitrary", "arbitrary"), vmem_limit_bytes=VMEM_LIMIT),
    )(ck, cv, clf)


def _attn_kernel(jmin_ref, qa_ref, ka_ref, va_ref, o_ref, *, tq, tk, q_off):
    b, hp, iq = pl.program_id(0), pl.program_id(1), pl.program_id(2)
    q_lo = q_off + iq * tq
    n_full = (q_lo + 1) // tk
    j_first = jmin_ref[(b * pl.num_programs(1) + hp) * pl.num_programs(2) + iq]
    lane = lax.broadcasted_iota(I32, (tq, AUG_W), 1)
    qs = (qa_ref[0], qa_ref[1])

    def visit(carry, k0, nk, r0, masked):
        new = []
        for hh in range(2):
            m_all, acc_all = carry[hh]
            m, acc, q = m_all[r0:], acc_all[r0:], qs[hh][r0:]
            k = ka_ref[hh, pl.ds(k0, nk), :]
            v = va_ref[hh, pl.ds(k0, nk), :]
            s = lax.dot_general(q, k, (((1,), (1,)), ((), ())), preferred_element_type=F32)
            if masked:
                qpos = q_lo + r0 + lax.broadcasted_iota(I32, (tq - r0, nk), 0)
                kpos = k0 + lax.broadcasted_iota(I32, (tq - r0, nk), 1)
                s = jnp.where(qpos >= kpos, s, -jnp.inf)
            m_new = jnp.maximum(m, jnp.max(s, axis=-1, keepdims=True))
            alpha = jnp.exp2(m - m_new)
            p = jnp.exp2(s - m_new)
            acc = acc * alpha + jnp.dot(p.astype(BF16), v, preferred_element_type=F32)
            if r0:
                m_new = jnp.concatenate([m_all[:r0], m_new], axis=0)
                acc = jnp.concatenate([acc_all[:r0], acc], axis=0)
            new.append((m_new, acc))
        return tuple(new)

    def step(j, carry):
        return visit(carry, pl.multiple_of(j * tk, tk), tk, 0, False)

    init = (jnp.full((tq, 1), -jnp.inf, F32), jnp.zeros((tq, AUG_W), F32))
    carry = lax.fori_loop(j_first, n_full, step, (init, init))
    k_diag = pl.multiple_of(n_full * tk, tk)
    if tq == tk and tq % (2 * LANES) == 0:
        half = tq // 2
        carry = visit(carry, k_diag, half, 0, True)
        carry = visit(carry, pl.multiple_of(k_diag + half, half), half, half, True)
    else:
        carry = visit(carry, k_diag, tk, 0, True)
    outs = [acc / acc[:, HEAD_DIM:HEAD_DIM + 1] for _, acc in carry]
    o_ref[...] = jnp.where(lane < HEAD_DIM, outs[0], pltpu.roll(outs[1], HEAD_DIM, 1)).astype(BF16)


def _attention(jmin, qa, ka, va, q_off, tq, tk):
    B, H, Tq, _ = qa.shape
    Tk = ka.shape[2]
    assert tk % tq == 0 and q_off % tq == 0 and tq > 1
    return pl.pallas_call(
        functools.partial(_attn_kernel, tq=tq, tk=tk, q_off=q_off),
        grid_spec=pltpu.PrefetchScalarGridSpec(
            num_scalar_prefetch=1,
            grid=(B, H // 2, Tq // tq),
            in_specs=[
                pl.BlockSpec((None, 2, tq, AUG_W), lambda b, hp, iq, jm: (b, hp, iq, 0)),
                pl.BlockSpec((None, 2, Tk, AUG_W), lambda b, hp, iq, jm: (b, hp, 0, 0)),
                pl.BlockSpec((None, 2, Tk, AUG_W), lambda b, hp, iq, jm: (b, hp, 0, 0)),
            ],
            out_specs=pl.BlockSpec((None, tq, 2 * HEAD_DIM), lambda b, hp, iq, jm: (b, iq, hp)),
        ),
        out_shape=jax.ShapeDtypeStruct((B, Tq, ATT_W), BF16),
        compiler_params=pltpu.CompilerParams(
            dimension_semantics=("arbitrary", "arbitrary", "arbitrary"), vmem_limit_bytes=VMEM_LIMIT),
    )(jmin, qa, ka, va)


def _attn_skip_plan(stats, tq, tk, tm):
    B, nt = stats.shape[:2]
    st = stats[..., :N_HEADS]
    per = lambda row, r: st[:, :, row].reshape(B, nt // r, r, N_HEADS)
    rq, rk = tq // tm, tk // tm
    nq, nk = nt // rq, nt // rk
    qn2, kn2, kn2_own = per(0, rq).max(2), per(1, rk).max(2), per(1, rq).max(2)
    f_first, f_last = per(2, rq)[:, :, 0], per(3, rk)[:, :, -1]
    slack = 1.01
    upper = jnp.sqrt(qn2[:, :, None] * kn2[:, None, :]) * slack + (f_first[:, :, None] - f_last[:, None, :])
    lower = -jnp.sqrt(qn2 * kn2_own) * slack
    weightless = upper - lower[:, :, None] <= -(SKIP_LOG2 + 2.0)
    j = jnp.arange(nk, dtype=I32)[None, None, :, None]
    n_full = ((jnp.arange(nq, dtype=I32) * tq + 1) // tk)[None, :, None, None]
    first = jnp.min(jnp.where(weightless | (j >= n_full), n_full, j), axis=2)
    first = jnp.min(first.reshape(B, nq, N_HEADS // 2, 2), axis=3)
    return jnp.swapaxes(first, 1, 2).reshape(-1).astype(I32)


GATE_CHUNKS = 4


def _gate_chunk(xn_bf, wg_ref, bg_ref, c):
    cols = slice(c * (2 * D_MODEL // GATE_CHUNKS), (c + 1) * (2 * D_MODEL // GATE_CHUNKS))
    return jax.nn.sigmoid(jnp.dot(xn_bf, wg_ref[:, cols], preferred_element_type=F32) + bg_ref[:, cols]), cols


def _mixer_kernel(x_ref, xnext_ref, u_ref, uprev_ref, hist_ref, ya_ref, cnt0_ref,
                  lng_ref, lnb_ref, wg_ref, bg_ref, wpool_ref, spool_ref, wpp_ref, watt_ref, wout_ref,
                  ln1g_ref, ln1b_ref, wr_ref, br_ref, *rest, tm, start_pos):
    h_ref, hp_ref, tw_ref, code_ref, cnt_ref, uext_ref, gates_ref = rest[-7:]
    b = pl.program_id(0)
    t = pl.program_id(1)
    xn = _layer_norm(x_ref[...], lng_ref[...], lnb_ref[...])

    @pl.when((b == 0) & (t == 0))
    def _():
        cnt_ref[...] = cnt0_ref[...]
        for c in range(GATE_CHUNKS):
            g, cols = _gate_chunk(xn.astype(BF16), wg_ref, bg_ref, c)
            gates_ref[:, cols] = g

    @pl.when(t == 0)
    def _():
        uext_ref[0:HALO, :] = hist_ref[...]

    @pl.when(t > 0)
    def _():
        uext_ref[0:HALO, :] = uprev_ref[...]

    uext_ref[HALO:HALO + tm, :] = u_ref[...]

    pos = start_pos + t * tm + lax.broadcasted_iota(I32, (tm, 1), 0)
    groups = []
    for gi, w in enumerate(POOL_WINDOWS):
        sl = slice(gi * POOL_GC, (gi + 1) * POOL_GC)
        cur = uext_ref[HALO:HALO + tm, sl]
        s = cur
        for j in range(1, w):
            s = s + uext_ref[HALO - j:HALO - j + tm, sl]
        count = jnp.minimum(pos + 1, w).astype(F32)
        d = s / count - cur
        yg = jnp.dot(d.astype(BF16), wpool_ref[gi], preferred_element_type=F32) * spool_ref[:, sl]
        groups.append(yg.astype(BF16))
    pooled = jnp.dot(jnp.concatenate(groups, axis=1), wpp_ref[...], preferred_element_type=F32)
    att = jnp.dot(ya_ref[...], watt_ref[...], preferred_element_type=F32)
    m = gates_ref[:, :D_MODEL] * pooled + gates_ref[:, D_MODEL:] * att
    mix = jnp.dot(m.astype(BF16), wout_ref[...], preferred_element_type=F32)

    xn_next = _layer_norm(xnext_ref[...], lng_ref[...], lnb_ref[...]).astype(BF16)

    def emit_next_gates(c):
        g, cols = _gate_chunk(xn_next, wg_ref, bg_ref, c)
        gates_ref[:, cols] = g

    emit_next_gates(0)

    h = _layer_norm(DEEPNORM_ALPHA * xn + mix, ln1g_ref[...], ln1b_ref[...])
    h_ref[...] = h
    packed = _pack_halves(h)
    hp_ref[0] = packed[:, :PLANE]
    hp_ref[1] = packed[:, PLANE:]
    emit_next_gates(1)

    hh = h.astype(BF16)
    hl = (h - hh.astype(F32)).astype(BF16)
    hi_pass = jnp.dot(hh, wr_ref[...], preferred_element_type=F32)
    lo_pass = jnp.dot(hl, wr_ref[:, :LANES], preferred_element_type=F32)
    logits = hi_pass[:, :LANES] + lo_pass + hi_pass[:, LANES:] + br_ref[...]
    emit_next_gates(2)
    rows = tm
    lane = lax.broadcasted_iota(I32, (rows, LANES), 1)
    work = jnp.where(lane < N_EXPERTS, logits, -jnp.inf)
    vals, idxs = [], []
    for _ in range(TOP_K):
        mx = jnp.max(work, axis=-1, keepdims=True)
        idx = jnp.min(jnp.where(work == mx, lane, LANES), axis=-1, keepdims=True)
        vals.append(mx)
        idxs.append(idx)
        work = jnp.where(lane == idx, -jnp.inf, work)
    exps = [jnp.exp(v - vals[0]) for v in vals]
    denom = exps[0] + exps[1] + exps[2] + exps[3]
    emit_next_gates(3)

    onehot = jnp.zeros((rows, LANES), F32)
    for idx in idxs:
        onehot = onehot + (lane == idx).astype(F32)
    base = _prefix_rows(onehot, rows, False, small_ints=True) + cnt_ref[...]
    code = jnp.zeros((rows, LANES), I32)
    tw = jnp.zeros((rows, LANES), F32)
    for k in range(TOP_K):
        rank = jnp.sum(jnp.where(lane == idxs[k], base, 0.0), axis=-1, keepdims=True)
        code = jnp.where(lane == k, idxs[k] * RANK_SPAN + rank.astype(I32), code)
        tw = jnp.where(lane == k, exps[k] / denom, tw)
    tw_ref[...] = tw
    if code_ref.shape[0] == CODE_ROWS:
        code_ref[...] = jnp.transpose(code)[:CODE_ROWS, :]
    else:
        code_ref[...] = code
    cnt_ref[...] = cnt_ref[...] + jnp.sum(onehot, axis=0, keepdims=True)


def _mixer(x, u, hist, yatt, cnt0, start_pos, weights, n_tok, row_off, prev):
    B, T, D = x.shape
    tm = min(SEQ_TILE, T)
    nt = T // tm
    row = lambda b, t: (b, t, 0)
    tok = lambda b, t: (row_off // tm + b * nt + t, 0)
    hpt = tm // HALO
    full = lambda a: pl.BlockSpec(a.shape, lambda b, t, _n=a.ndim: (0,) * _n)
    tok3 = lambda b, t: (0, row_off // tm + b * nt + t, 0)
    assert n_tok <= RANK_SPAN
    if tm % LANES == 0:
        code_spec = pl.BlockSpec((CODE_ROWS, tm), lambda b, t: (0, b * nt + t))
        code_shape = jax.ShapeDtypeStruct((CODE_ROWS, B * T), I32)
    else:
        code_spec = pl.BlockSpec((tm, LANES), lambda b, t: (b * nt + t, 0))
        code_shape = jax.ShapeDtypeStruct((B * T, LANES), I32)
    out_specs = [pl.BlockSpec((tm, D), tok), pl.BlockSpec((2, tm, PLANE), tok3), pl.BlockSpec((tm, LANES), tok),
                 code_spec, pl.BlockSpec((1, LANES), lambda b, t: (0, 0))]
    out_shape = [jax.ShapeDtypeStruct((n_tok, D), F32), jax.ShapeDtypeStruct((2, n_tok, PLANE), U32),
                 jax.ShapeDtypeStruct((n_tok, LANES), F32), code_shape, jax.ShapeDtypeStruct((1, LANES), F32)]
    n_in = 7 + len(weights)

    def next_row(b, t):
        lin = jnp.minimum(b * nt + t + 1, B * nt - 1)
        return (lin // nt, lin % nt, 0)

    return pl.pallas_call(
        functools.partial(_mixer_kernel, tm=tm, start_pos=start_pos),
        grid=(B, nt),
        in_specs=[
            pl.BlockSpec((None, tm, D), row),
            pl.BlockSpec((None, tm, D), next_row),
            pl.BlockSpec((None, tm, POOL_W), row),
            pl.BlockSpec((None, HALO, POOL_W), lambda b, t: (b, jnp.maximum(t * hpt - 1, 0), 0)),
            pl.BlockSpec((None, HALO, POOL_W), lambda b, t: (b, 0, 0)),
            pl.BlockSpec((None, tm, ATT_W), row),
            pl.BlockSpec((1, LANES), lambda b, t: (0, 0)),
        ] + [full(w) for w in weights] + [pl.BlockSpec(memory_space=pl.ANY) for _ in prev],
        out_specs=out_specs,
        out_shape=out_shape,
        scratch_shapes=[pltpu.VMEM((HALO + tm, POOL_W), F32), pltpu.VMEM((tm, 2 * D), F32)],
        input_output_aliases={n_in + i: i for i in range(len(prev))},
        compiler_params=pltpu.CompilerParams(
            dimension_semantics=("arbitrary", "arbitrary"), vmem_limit_bytes=VMEM_LIMIT),
    )(x, x, u, u, hist, yatt, cnt0, *weights, *prev)


def _sc_mesh():
    return plsc.VectorSubcoreMesh(core_axis_name="c", subcore_axis_name="s")


def _sc_scatter_rows(x, idx, n_rows):
    n = x.shape[0]

    @pl.kernel(out_type=jax.ShapeDtypeStruct((n_rows, PLANE), x.dtype), mesh=_sc_mesh(), scratch_types=[])
    def scatter(x_hbm, i_hbm, o_hbm):
        def body(x_vmem, i_vmem):
            for k in range(TOP_K):
                pltpu.sync_copy(x_vmem, o_hbm.at[i_vmem.at[k]])

        pltpu.emit_pipeline(
            body,
            grid=(n // SC_WINDOW,),
            in_specs=[pl.BlockSpec((SC_WINDOW, PLANE), index_map=lambda i: (i, 0)),
                      pl.BlockSpec((TOP_K, SC_WINDOW), index_map=lambda i: (0, i))],
            out_specs=[],
            core_axis_name=("c", "s"),
            dimension_semantics=(pltpu.PARALLEL,),
        )(x_hbm, i_hbm)

    return scatter(x, idx)


def _sc_gather_rows(y, idx):
    n = idx.shape[1]

    @pl.kernel(out_type=jax.ShapeDtypeStruct((n, PLANE), y.dtype), mesh=_sc_mesh(), scratch_types=[])
    def gather(y_hbm, i_hbm, o_hbm):
        def body(i_vmem, o_vmem):
            pltpu.sync_copy(y_hbm.at[i_vmem.at[0]], o_vmem)

        pltpu.emit_pipeline(
            body,
            grid=(n // SC_WINDOW,),
            in_specs=[pl.BlockSpec((1, SC_WINDOW), index_map=lambda i: (0, i))],
            out_specs=[pl.BlockSpec((SC_WINDOW, PLANE), index_map=lambda i: (i, 0))],
            core_axis_name=("c", "s"),
            dimension_semantics=(pltpu.PARALLEL,),
        )(i_hbm, o_hbm)

    return gather(y, idx)


def _join_planes(ref):
    return jnp.concatenate([ref[0], ref[1]], axis=1)


def _unpack_rows(xb_ref):
    xa, xc = _unpack_halves(_join_planes(xb_ref))
    return jnp.concatenate([xa.astype(BF16), xc.astype(BF16)], axis=1)


def _expert_kernel(be_ref, nb_ref, first_ref, xb_ref, xbnext_ref, w1_ref, b1_ref, w2_ref, b2_ref,
                   yb_ref, w1b_ref, w2b_ref, x_ref):
    del be_ref
    i = pl.program_id(0)

    @pl.when(i == 0)
    def _():
        x_ref[...] = _unpack_rows(xb_ref)

    @pl.when(first_ref[i] == 1)
    def _():
        w1b_ref[...] = w1_ref[...].astype(BF16)
        w2b_ref[...] = w2_ref[...].astype(BF16)

    @pl.when(i < nb_ref[0])
    def _():
        hfull = jnp.dot(x_ref[...], w1b_ref[...], preferred_element_type=F32) + b1_ref[...]
        glu = jnp.minimum(hfull[:, :D_FF], SWIGLU_LIMIT)
        lin = jnp.clip(hfull[:, D_FF:], -SWIGLU_LIMIT, SWIGLU_LIMIT)
        a = glu * jax.nn.sigmoid(SWIGLU_ALPHA * glu) * (lin + 1.0)
        x_next = _unpack_rows(xbnext_ref)
        y = jnp.dot(a.astype(BF16), w2b_ref[...], preferred_element_type=F32) + b2_ref[...]
        packed = _pack_halves(y)
        yb_ref[0] = packed[:, :PLANE]
        yb_ref[1] = packed[:, PLANE:]
        x_ref[...] = x_next

    @pl.when(i >= nb_ref[0])
    def _():
        yb_ref[...] = jnp.zeros_like(yb_ref)


def _experts(block_e, n_used, first, xb, w1, b1, w2, b2):
    n_rows = xb.shape[1]
    bm = MOE_BLOCK
    n_blocks = n_rows // bm
    rows = pl.BlockSpec((2, bm, PLANE), lambda i, be, nb, fi: (0, i, 0))
    next_rows = pl.BlockSpec((2, bm, PLANE), lambda i, be, nb, fi: (0, jnp.minimum(i + 1, n_blocks - 1), 0))
    per_expert = lambda r, c: pl.BlockSpec((None, r, c), lambda i, be, nb, fi: (be[i], 0, 0))
    return pl.pallas_call(
        _expert_kernel,
        grid_spec=pltpu.PrefetchScalarGridSpec(
            num_scalar_prefetch=3,
            grid=(n_blocks,),
            in_specs=[rows, next_rows, per_expert(D_MODEL, 2 * D_FF), per_expert(1, 2 * D_FF),
                      per_expert(D_FF, D_MODEL), per_expert(1, D_MODEL)],
            out_specs=rows,
            scratch_shapes=[pltpu.VMEM((D_MODEL, 2 * D_FF), BF16), pltpu.VMEM((D_FF, D_MODEL), BF16),
                            pltpu.VMEM((bm, D_MODEL), BF16)],
        ),
        out_shape=jax.ShapeDtypeStruct((2, n_rows, PLANE), U32),
        compiler_params=pltpu.CompilerParams(
            dimension_semantics=("arbitrary",), vmem_limit_bytes=VMEM_LIMIT),
    )(block_e, n_used, first, xb, xb, w1, b1, w2, b2)


def _combine_kernel(h_ref, tw_ref, g_ref, b_ref, y4_ref, out_ref):
    tw = tw_ref[...]
    acc_hi = None
    acc_lo = None
    for k in range(TOP_K):
        hi, lo = _unpack_halves(_join_planes(y4_ref.at[k]))
        w = tw[:, k:k + 1]
        acc_hi = w * hi if acc_hi is None else acc_hi + w * hi
        acc_lo = w * lo if acc_lo is None else acc_lo + w * lo
    moe = jnp.concatenate([acc_hi, acc_lo], axis=1)
    out_ref[...] = _layer_norm(DEEPNORM_ALPHA * h_ref[...] + moe, g_ref[...], b_ref[...])


def _combine(h, tw, ln2g, ln2b, y4, row_off, n_rows):
    D = h.shape[1]
    tn = min(SEQ_TILE, n_rows)
    off = row_off // tn
    const = lambda i: (0, 0)
    return pl.pallas_call(
        _combine_kernel,
        grid=(n_rows // tn,),
        in_specs=[
            pl.BlockSpec((tn, D), lambda i: (off + i, 0)),
            pl.BlockSpec((tn, LANES), lambda i: (off + i, 0)),
            pl.BlockSpec((1, D), const), pl.BlockSpec((1, D), const),
            pl.BlockSpec((TOP_K, 2, tn, PLANE), lambda i: (0, 0, off + i, 0)),
        ],
        out_specs=pl.BlockSpec((tn, D), lambda i: (i, 0)),
        out_shape=jax.ShapeDtypeStruct((n_rows, D), F32),
        compiler_params=pltpu.CompilerParams(
            dimension_semantics=("arbitrary",), vmem_limit_bytes=VMEM_LIMIT),
    )(h, tw, ln2g, ln2b, y4)


def kernel(x_prompt, x_sample, cache_pool, cache_k, cache_v, cache_logf, ln_in_g, ln_in_b, w_in, b_in,
           w_pool, s_pool, w_pool_proj, w_att_proj, w_out, ln1_g, ln1_b, w_router, b_router,
           w1, b1, w2, b2, ln2_g, ln2_b):
    assert w_in.shape[0] == DEPTH
    B, T, D = x_prompt.shape
    Bs, Ts, _ = x_sample.shape
    P = cache_k.shape[2]
    row2 = lambda a: a.reshape(1, -1).astype(F32)

    f_off = MAIN_W
    g_off = MAIN_W + N_HEADS
    w_proj = jnp.pad(w_in[0][:, :g_off], ((0, 0), (0, LANES - N_HEADS))).astype(BF16)
    b_proj = row2(jnp.pad(b_in[0][:g_off], (0, LANES - N_HEADS)))
    wg = w_in[0][:, g_off:].astype(BF16)
    bg = row2(b_in[0][g_off:])
    lng, lnb = row2(ln_in_g), row2(ln_in_b)
    wr = jnp.pad(w_router[0], ((0, 0), (0, LANES - N_EXPERTS)))
    wr_hi = wr.astype(BF16)
    wr_lo = (wr - wr_hi.astype(F32)).astype(BF16)
    br = row2(jnp.pad(b_router[0], (0, LANES - N_EXPERTS)))
    mixer_weights = (lng, lnb, wg, bg, w_pool[0].astype(BF16), row2(s_pool[0]),
                     w_pool_proj[0].astype(BF16), w_att_proj[0].astype(BF16), w_out[0].astype(BF16),
                     row2(ln1_g[0]), row2(ln1_b[0]), jnp.concatenate([wr_hi, wr_lo], axis=1), br)

    zeros_f = jnp.zeros((B, 1, LANES), F32)
    u_p, k_p, v_p, logf_p, qa_p, ka_p, va_p, stats = _inproj(x_prompt, zeros_f, lng, lnb, w_proj, b_proj)
    tq, tk = min(ATT_TQ, T), min(ATT_TK, T)
    ya_p = _attention(_attn_skip_plan(stats, tq, tk, min(SEQ_TILE, T)), qa_p, ka_p, va_p, 0, tq, tk)

    clf = jnp.pad(cache_logf[0], ((0, 0), (0, 0), (0, LANES - N_HEADS)))
    tks = P + Ts + (-(P + Ts)) % LANES
    ka_c, va_c, f_tot = _cache_prep(cache_k[0].reshape(Bs, P, ATT_W).astype(BF16),
                                    cache_v[0].reshape(Bs, P, ATT_W).astype(BF16), clf, tks)
    u_s, k_s, v_s, logf_s, qa_s, ka_s, va_s, _ = _inproj(x_sample, f_tot, lng, lnb, w_proj, b_proj, (ka_c, va_c), P)
    visit_all = jnp.zeros((Bs * (N_HEADS // 2),), I32)
    ya_s = _attention(visit_all, qa_s, ka_s, va_s, P, Ts, tks)

    cnt0 = jnp.zeros((1, LANES), F32)
    hist_p = jnp.zeros((B, HALO, POOL_W), F32)
    n_tok = B * T + Bs * Ts
    *bufs, code_p, cnt_p = _mixer(x_prompt, u_p, hist_p, ya_p, cnt0, 0, mixer_weights, n_tok, 0, ())
    u_full_s = jnp.concatenate([cache_pool[0].astype(F32), u_s], axis=1)
    hist_s = jnp.pad(cache_pool[0].astype(F32), ((0, 0), (HALO - POOL_HIST, 0), (0, 0)))
    h_all, hp_all, tw, code_s, cnt = _mixer(x_sample, u_s, hist_s, ya_s, cnt_p, P, mixer_weights,
                                            n_tok, B * T, tuple(bufs))
    by_slot = lambda c: c[:TOP_K] if c.shape[0] == CODE_ROWS else c[:, :TOP_K].T
    code = jnp.concatenate([by_slot(code_p), by_slot(code_s)], axis=1)
    ti, rk = code // RANK_SPAN, code % RANK_SPAN

    counts = cnt[0, :N_EXPERTS].astype(I32)
    padded = (counts + MOE_BLOCK - 1) // MOE_BLOCK * MOE_BLOCK
    pad_ends = jnp.cumsum(padded)
    pad_starts = pad_ends - padded
    n_blocks = (n_tok * TOP_K + N_EXPERTS * (MOE_BLOCK - 1) + MOE_BLOCK - 1) // MOE_BLOCK
    n_rows = n_blocks * MOE_BLOCK
    block_start = jnp.arange(n_blocks, dtype=I32) * MOE_BLOCK
    block_e = jnp.minimum(jnp.sum((pad_ends[None, :] <= block_start[:, None]).astype(I32), axis=1), N_EXPERTS - 1)
    n_used = (pad_ends[-1:] // MOE_BLOCK).astype(I32)
    start_of = sum(jnp.where(ti == e, pad_starts[e], 0) for e in range(N_EXPERTS))
    dest_t = (start_of + rk).astype(I32)
    dest_planes = jnp.stack([dest_t, dest_t + n_rows], axis=1)

    xb = _sc_scatter_rows(hp_all.reshape(2 * n_tok, PLANE), dest_planes.reshape(TOP_K, 2 * n_tok), 2 * n_rows)
    first = jnp.concatenate([jnp.ones((1,), I32), (block_e[1:] != block_e[:-1]).astype(I32)])
    yb = _experts(block_e, n_used, first, xb.reshape(2, n_rows, PLANE),
                  w1[0], b1[0][:, None, :], w2[0], b2[0][:, None, :])
    y4 = _sc_gather_rows(yb.reshape(2 * n_rows, PLANE), dest_planes.reshape(1, TOP_K * 2 * n_tok))
    y4 = y4.reshape(TOP_K, 2, n_tok, PLANE)
    ln2g, ln2b = row2(ln2_g[0]), row2(ln2_b[0])
    y_prompt = _combine(h_all, tw, ln2g, ln2b, y4, 0, B * T).reshape(B, T, D)
    y_sample = _combine(h_all, tw, ln2g, ln2b, y4, B * T, Bs * Ts).reshape(Bs, Ts, D)
    heads = lambda a, b_, t_: a.reshape(1, b_, t_, N_HEADS, HEAD_DIM)
    return (y_prompt, y_sample,
            heads(k_p, B, T), heads(v_p, B, T), logf_p[None], u_p[:, -POOL_HIST:][None],
            heads(k_s, Bs, Ts), heads(v_s, Bs, Ts), logf_s[None], u_full_s[:, -POOL_HIST:][None])
```

```python
import functools

import jax
import jax.numpy as jnp
from jax import lax
from jax.experimental import pallas as pl
from jax.experimental.pallas import tpu as pltpu
from jax.experimental.pallas import tpu_sc as plsc

F32 = jnp.float32
BF16 = jnp.bfloat16
I32 = jnp.int32
U32 = jnp.uint32

D_MODEL = 1024
N_HEADS = 8
HEAD_DIM = 64
ATT_W = N_HEADS * HEAD_DIM
POOL_WINDOWS = (2, 4, 8, 16)
POOL_GC = 128
POOL_W = len(POOL_WINDOWS) * POOL_GC
POOL_HIST = max(POOL_WINDOWS) - 1
HALO = 16
N_EXPERTS = 32
TOP_K = 4
D_FF = D_MODEL
SWIGLU_ALPHA = 1.702
SWIGLU_LIMIT = 7.0
LN_EPS = 1e-5
DEPTH = 1
DEEPNORM_ALPHA = (2.0 * DEPTH) ** 0.25
ATT_SCALE = HEAD_DIM ** -0.5
LOG2E = 1.4426950408889634
MAIN_W = POOL_W + 3 * ATT_W
LANES = 128
AUG_W = LANES
HALF = D_MODEL // 2
VMEM_LIMIT = 56 * 1024 * 1024

SEQ_TILE = 512
COMBINE_TILE = 1024
ATT_TQ = 1024
ATT_TK = 1024
N_STATS = 4
SKIP_LOG2 = 150.0
MOE_BLOCK = 512
RANK_SPAN = 1 << 20
CODE_ROWS = 8
PLANE = HALF // 2
SC_WINDOW = 128


def _layer_norm(x, g, b):
    mu = jnp.mean(x, axis=-1, keepdims=True)
    xc = x - mu
    var = jnp.mean(xc * xc, axis=-1, keepdims=True)
    return xc * lax.rsqrt(var + LN_EPS) * g + b


def _split3(x):
    a = x.astype(BF16)
    r = x - a.astype(F32)
    b = r.astype(BF16)
    c = (r - b.astype(F32)).astype(BF16)
    return a, b, c


def _prefix_rows(x, tm, inclusive, small_ints=False):
    kp = max(tm, LANES)
    r = lax.broadcasted_iota(I32, (tm, kp), 0)
    c = lax.broadcasted_iota(I32, (tm, kp), 1)
    tri = ((c <= r) if inclusive else (c < r)).astype(BF16)
    if kp > tm:
        x = jnp.concatenate([x, jnp.zeros((kp - tm, x.shape[1]), x.dtype)], axis=0)
    out = None
    for piece in ((x.astype(BF16),) if small_ints else _split3(x)):
        y = jnp.dot(tri, piece, preferred_element_type=F32)
        out = y if out is None else out + y
    return out


def _pack_halves(y):
    hi = pltpu.bitcast(y[:, :HALF].astype(BF16).astype(F32), U32)
    lo = pltpu.bitcast(y[:, HALF:].astype(BF16).astype(F32), U32)
    return hi | (lo >> 16)


def _unpack_halves(w):
    hi = pltpu.bitcast(w & jnp.uint32(0xFFFF0000), F32)
    lo = pltpu.bitcast(w << 16, F32)
    return hi, lo


def _head_slab(p, off, h):
    s = p[:, off + (h // 2) * LANES: off + (h // 2) * LANES + LANES]
    return s if h % 2 == 0 else pltpu.roll(s, HEAD_DIM, 1)


def _augment(qs, ks, vs, f_col, lane):
    fc = jnp.broadcast_to(f_col, lane.shape)
    hi = fc.astype(BF16).astype(F32)
    r1 = fc - hi
    mid = r1.astype(BF16).astype(F32)
    lo = r1 - mid
    one = jnp.ones_like(fc)
    zero = jnp.zeros_like(fc)
    ka = jnp.where(lane < 64, ks, jnp.where(lane < 67, one, jnp.where(
        lane == 67, -hi, jnp.where(lane == 68, -mid, jnp.where(lane == 69, -lo, zero)))))
    va = jnp.where(lane < 64, vs, jnp.where(lane == 64, one, zero))
    if qs is None:
        return None, ka.astype(BF16), va.astype(BF16)
    qa = jnp.where(lane < 64, qs, jnp.where(lane == 64, hi, jnp.where(
        lane == 65, mid, jnp.where(lane == 66, lo, jnp.where(lane < 70, one, zero)))))
    return qa.astype(BF16), ka.astype(BF16), va.astype(BF16)


PROJ_W = MAIN_W + LANES
PROJ_CHUNKS = ((0, 512), (512, 1024), (1024, 1536), (1536, PROJ_W))


def _inproj_kernel(x_ref, xnext_ref, f0_ref, lng_ref, lnb_ref, w_ref, b_ref, *rest, tm):
    u_ref, k_ref, v_ref, logf_ref, qa_ref, ka_ref, va_ref, st_ref, carry_ref, proj_ref = rest[-10:]
    t = pl.program_id(1)
    lin = pl.program_id(0) * pl.num_programs(1) + t

    @pl.when(t == 0)
    def _():
        carry_ref[...] = f0_ref[...]

    def project(xn_bf, chunk):
        lo, hi = PROJ_CHUNKS[chunk]
        return jnp.dot(xn_bf, w_ref[:, lo:hi], preferred_element_type=F32) + b_ref[:, lo:hi]

    @pl.when(lin == 0)
    def _():
        xn = _layer_norm(x_ref[...], lng_ref[...], lnb_ref[...]).astype(BF16)
        for c, (lo, hi) in enumerate(PROJ_CHUNKS):
            proj_ref[:, lo:hi] = project(xn, c)

    p = proj_ref
    next_chunks = []
    xn_next = _layer_norm(xnext_ref[...], lng_ref[...], lnb_ref[...]).astype(BF16)
    fl = p[:, MAIN_W:]
    logf = jnp.minimum(fl, 0.0) - jnp.log(1.0 + jnp.exp(-jnp.abs(fl)))
    f_cum = _prefix_rows(logf, tm, True) + carry_ref[...]
    carry_ref[...] = f_cum[tm - 1:tm, :]

    u_ref[...] = p[:, :POOL_W]
    k_ref[...] = p[:, POOL_W + ATT_W:POOL_W + 2 * ATT_W]
    v_ref[...] = p[:, POOL_W + 2 * ATT_W:MAIN_W]
    logf_ref[...] = logf[:, :N_HEADS]

    lane = lax.broadcasted_iota(I32, (tm, AUG_W), 1)
    for h in range(N_HEADS):
        qs = _head_slab(p, POOL_W, h) * (ATT_SCALE * LOG2E)
        ks = _head_slab(p, POOL_W + ATT_W, h)
        vs = _head_slab(p, POOL_W + 2 * ATT_W, h)
        qa, ka, va = _augment(qs, ks, vs, f_cum[:, h:h + 1] * LOG2E, lane)
        qa_ref[h] = qa
        ka_ref[h] = ka
        va_ref[h] = va
        if h % 2 == 1:
            next_chunks.append(project(xn_next, h // 2))

    col = lax.broadcasted_iota(I32, (ATT_W, LANES), 0) // HEAD_DIM
    head_sum = (col == lax.broadcasted_iota(I32, (ATT_W, LANES), 1)).astype(BF16)
    qsec = (p[:, POOL_W:POOL_W + ATT_W] * (ATT_SCALE * LOG2E)).astype(BF16).astype(F32)
    ksec = p[:, POOL_W + ATT_W:POOL_W + 2 * ATT_W].astype(BF16).astype(F32)
    for r, sec in enumerate((qsec, ksec)):
        sq = jnp.dot((sec * sec).astype(BF16), head_sum, preferred_element_type=F32)
        st_ref[r:r + 1, :] = jnp.max(sq, axis=0, keepdims=True)
    st_ref[2:3, :] = f_cum[0:1, :] * LOG2E
    st_ref[3:4, :] = f_cum[tm - 1:tm, :] * LOG2E
    for (lo, hi), chunk in zip(PROJ_CHUNKS, next_chunks):
        proj_ref[:, lo:hi] = chunk


def _inproj(x, f0, lng, lnb, w, bias, kv_prev=(), kv_row_off=0):
    B, T, D = x.shape
    tm = min(SEQ_TILE, T)
    nt = T // tm
    grid = (B, nt)
    const = lambda b, t: (0, 0)
    row = lambda b, t: (b, t, 0)

    def next_row(b, t):
        lin = jnp.minimum(b * nt + t + 1, B * nt - 1)
        return (lin // nt, lin % nt, 0)

    aug = pl.BlockSpec((None, N_HEADS, tm, AUG_W), lambda b, t: (b, 0, t, 0))
    aug_shape = jax.ShapeDtypeStruct((B, N_HEADS, T, AUG_W), BF16)
    kv = pl.BlockSpec((None, N_HEADS, tm, AUG_W), lambda b, t: (b, 0, kv_row_off // tm + t, 0))
    kv_shape = jax.ShapeDtypeStruct(kv_prev[0].shape, BF16) if kv_prev else aug_shape
    n_in = 7
    return pl.pallas_call(
        functools.partial(_inproj_kernel, tm=tm),
        grid=grid,
        in_specs=[
            pl.BlockSpec((None, tm, D), row),
            pl.BlockSpec((None, tm, D), next_row),
            pl.BlockSpec((None, 1, LANES), lambda b, t: (b, 0, 0)),
            pl.BlockSpec((1, D), const), pl.BlockSpec((1, D), const),
            pl.BlockSpec((D, PROJ_W), const), pl.BlockSpec((1, PROJ_W), const),
        ] + [pl.BlockSpec(memory_space=pl.ANY) for _ in kv_prev],
        out_specs=[
            pl.BlockSpec((None, tm, POOL_W), row),
            pl.BlockSpec((None, tm, ATT_W), row),
            pl.BlockSpec((None, tm, ATT_W), row),
            pl.BlockSpec((None, tm, N_HEADS), row),
            aug, kv, kv,
            pl.BlockSpec((None, None, N_STATS, LANES), lambda b, t: (b, t, 0, 0)),
        ],
        out_shape=[
            jax.ShapeDtypeStruct((B, T, POOL_W), F32),
            jax.ShapeDtypeStruct((B, T, ATT_W), F32),
            jax.ShapeDtypeStruct((B, T, ATT_W), F32),
            jax.ShapeDtypeStruct((B, T, N_HEADS), F32),
            aug_shape, kv_shape, kv_shape,
            jax.ShapeDtypeStruct((B, T // tm, N_STATS, LANES), F32),
        ],
        scratch_shapes=[pltpu.VMEM((1, LANES), F32), pltpu.VMEM((tm, PROJ_W), F32)],
        input_output_aliases={n_in + i: 5 + i for i in range(len(kv_prev))},
        compiler_params=pltpu.CompilerParams(
            dimension_semantics=("arbitrary", "arbitrary"), vmem_limit_bytes=VMEM_LIMIT),
    )(x, x, f0, lng, lnb, w, bias, *kv_prev)


def _cache_kernel(ck_ref, cv_ref, clf_ref, ka_ref, va_ref, ftot_ref, carry_ref, *, tp):
    t = pl.program_id(1)

    @pl.when(t == 0)
    def _():
        carry_ref[...] = jnp.zeros_like(carry_ref)

    @pl.when(t < pl.num_programs(1) - 1)
    def _():
        f_cum = _prefix_rows(clf_ref[...], tp, True) + carry_ref[...]
        carry_ref[...] = f_cum[tp - 1:tp, :]
        ftot_ref[...] = f_cum[tp - 1:tp, :]
        ck = ck_ref[...].astype(F32)
        cv = cv_ref[...].astype(F32)
        lane = lax.broadcasted_iota(I32, (tp, AUG_W), 1)
        for h in range(N_HEADS):
            _, ka, va = _augment(None, _head_slab(ck, 0, h), _head_slab(cv, 0, h), f_cum[:, h:h + 1] * LOG2E, lane)
            ka_ref[h] = ka
            va_ref[h] = va

    @pl.when(t == pl.num_programs(1) - 1)
    def _():
        ka_ref[...] = jnp.zeros_like(ka_ref)
        va_ref[...] = jnp.zeros_like(va_ref)


def _cache_prep(ck, cv, clf, n_keys):
    B, P, _ = ck.shape
    tp = min(SEQ_TILE, P)
    nt = P // tp
    row = lambda b, t: (b, jnp.minimum(t, nt - 1), 0)
    aug = pl.BlockSpec((None, N_HEADS, tp, AUG_W), lambda b, t: (b, 0, t, 0))
    aug_shape = jax.ShapeDtypeStruct((B, N_HEADS, n_keys, AUG_W), BF16)
    assert P < n_keys <= P + tp
    return pl.pallas_call(
        functools.partial(_cache_kernel, tp=tp),
        grid=(B, nt + 1),
        in_specs=[pl.BlockSpec((None, tp, ATT_W), row), pl.BlockSpec((None, tp, ATT_W), row),
                  pl.BlockSpec((None, tp, LANES), row)],
        out_specs=[aug, aug, pl.BlockSpec((None, 1, LANES), lambda b, t: (b, 0, 0))],
        out_shape=[aug_shape, aug_shape, jax.ShapeDtypeStruct((B, 1, LANES), F32)],
        scratch_shapes=[pltpu.VMEM((1, LANES), F32)],
        compiler_params=pltpu.CompilerParams(
            dimension_semantics=("arbitrary", "arbitrary"), vmem_limit_bytes=VMEM_LIMIT),
    )(ck, cv, clf)


def _attn_kernel(jmin_ref, qa_ref, ka_ref, va_ref, o_ref, *, tq, tk, q_off):
    b, hp, iq = pl.program_id(0), pl.program_id(1), pl.program_id(2)
    q_lo = q_off + iq * tq
    n_full = (q_lo + 1) // tk
    j_first = jmin_ref[(b * pl.num_programs(1) + hp) * pl.num_programs(2) + iq]
    lane = lax.broadcasted_iota(I32, (tq, AUG_W), 1)
    qs = (qa_ref[0], qa_ref[1])

    def visit(carry, k0, nk, r0, masked):
        new = []
        for hh in range(2):
            m_all, acc_all = carry[hh]
            m, acc, q = m_all[r0:], acc_all[r0:], qs[hh][r0:]
            k = ka_ref[hh, pl.ds(k0, nk), :]
            v = va_ref[hh, pl.ds(k0, nk), :]
            s = lax.dot_general(q, k, (((1,), (1,)), ((), ())), preferred_element_type=F32)
            if masked:
                qpos = q_lo + r0 + lax.broadcasted_iota(I32, (tq - r0, nk), 0)
                kpos = k0 + lax.broadcasted_iota(I32, (tq - r0, nk), 1)
                s = jnp.where(qpos >= kpos, s, -jnp.inf)
            m_new = jnp.maximum(m, jnp.max(s, axis=-1, keepdims=True))
            alpha = jnp.exp2(m - m_new)
            p = jnp.exp2(s - m_new)
            acc = acc * alpha + jnp.dot(p.astype(BF16), v, preferred_element_type=F32)
            if r0:
                m_new = jnp.concatenate([m_all[:r0], m_new], axis=0)
                acc = jnp.concatenate([acc_all[:r0], acc], axis=0)
            new.append((m_new, acc))
        return tuple(new)

    def step(j, carry):
        return visit(carry, pl.multiple_of(j * tk, tk), tk, 0, False)

    init = (jnp.full((tq, 1), -jnp.inf, F32), jnp.zeros((tq, AUG_W), F32))
    carry = lax.fori_loop(j_first, n_full, step, (init, init))
    k_diag = pl.multiple_of(n_full * tk, tk)
    if tq == tk and tq % (2 * LANES) == 0:
        half = tq // 2
        carry = visit(carry, k_diag, half, 0, True)
        carry = visit(carry, pl.multiple_of(k_diag + half, half), half, half, True)
    else:
        carry = visit(carry, k_diag, tk, 0, True)
    outs = [acc / acc[:, HEAD_DIM:HEAD_DIM + 1] for _, acc in carry]
    o_ref[...] = jnp.where(lane < HEAD_DIM, outs[0], pltpu.roll(outs[1], HEAD_DIM, 1)).astype(BF16)


def _attention(jmin, qa, ka, va, q_off, tq, tk):
    B, H, Tq, _ = qa.shape
    Tk = ka.shape[2]
    assert tk % tq == 0 and q_off % tq == 0 and tq > 1
    return pl.pallas_call(
        functools.partial(_attn_kernel, tq=tq, tk=tk, q_off=q_off),
        grid_spec=pltpu.PrefetchScalarGridSpec(
            num_scalar_prefetch=1,
            grid=(B, H // 2, Tq // tq),
            in_specs=[
                pl.BlockSpec((None, 2, tq, AUG_W), lambda b, hp, iq, jm: (b, hp, iq, 0)),
                pl.BlockSpec((None, 2, Tk, AUG_W), lambda b, hp, iq, jm: (b, hp, 0, 0)),
                pl.BlockSpec((None, 2, Tk, AUG_W), lambda b, hp, iq, jm: (b, hp, 0, 0)),
            ],
            out_specs=pl.BlockSpec((None, tq, 2 * HEAD_DIM), lambda b, hp, iq, jm: (b, iq, hp)),
        ),
        out_shape=jax.ShapeDtypeStruct((B, Tq, ATT_W), BF16),
        compiler_params=pltpu.CompilerParams(
            dimension_semantics=("arbitrary", "arbitrary", "arbitrary"), vmem_limit_bytes=VMEM_LIMIT),
    )(jmin, qa, ka, va)


def _attn_skip_plan(stats, tq, tk, tm):
    B, nt = stats.shape[:2]
    st = stats[..., :N_HEADS]
    per = lambda row, r: st[:, :, row].reshape(B, nt // r, r, N_HEADS)
    rq, rk = tq // tm, tk // tm
    nq, nk = nt // rq, nt // rk
    qn2, kn2, kn2_own = per(0, rq).max(2), per(1, rk).max(2), per(1, rq).max(2)
    f_first, f_last = per(2, rq)[:, :, 0], per(3, rk)[:, :, -1]
    slack = 1.01
    upper = jnp.sqrt(qn2[:, :, None] * kn2[:, None, :]) * slack + (f_first[:, :, None] - f_last[:, None, :])
    lower = -jnp.sqrt(qn2 * kn2_own) * slack
    weightless = upper - lower[:, :, None] <= -(SKIP_LOG2 + 2.0)
    j = jnp.arange(nk, dtype=I32)[None, None, :, None]
    n_full = ((jnp.arange(nq, dtype=I32) * tq + 1) // tk)[None, :, None, None]
    first = jnp.min(jnp.where(weightless | (j >= n_full), n_full, j), axis=2)
    first = jnp.min(first.reshape(B, nq, N_HEADS // 2, 2), axis=3)
    return jnp.swapaxes(first, 1, 2).reshape(-1).astype(I32)


GATE_CHUNKS = 4


def _gate_chunk(xn_bf, wg_ref, bg_ref, c):
    cols = slice(c * (2 * D_MODEL // GATE_CHUNKS), (c + 1) * (2 * D_MODEL // GATE_CHUNKS))
    return jax.nn.sigmoid(jnp.dot(xn_bf, wg_ref[:, cols], preferred_element_type=F32) + bg_ref[:, cols]), cols


def _mixer_kernel(x_ref, xnext_ref, u_ref, uprev_ref, hist_ref, ya_ref, cnt0_ref,
                  lng_ref, lnb_ref, wg_ref, bg_ref, wpool_ref, spool_ref, wpp_ref, watt_ref, wout_ref,
                  ln1g_ref, ln1b_ref, wr_ref, br_ref, *rest, tm, start_pos):
    h_ref, hp_ref, tw_ref, code_ref, cnt_ref, uext_ref, gates_ref = rest[-7:]
    b = pl.program_id(0)
    t = pl.program_id(1)
    xn = _layer_norm(x_ref[...], lng_ref[...], lnb_ref[...])

    @pl.when((b == 0) & (t == 0))
    def _():
        cnt_ref[...] = cnt0_ref[...]
        for c in range(GATE_CHUNKS):
            g, cols = _gate_chunk(xn.astype(BF16), wg_ref, bg_ref, c)
            gates_ref[:, cols] = g

    @pl.when(t == 0)
    def _():
        uext_ref[0:HALO, :] = hist_ref[...]

    @pl.when(t > 0)
    def _():
        uext_ref[0:HALO, :] = uprev_ref[...]

    uext_ref[HALO:HALO + tm, :] = u_ref[...]

    pos = start_pos + t * tm + lax.broadcasted_iota(I32, (tm, 1), 0)
    groups = []
    for gi, w in enumerate(POOL_WINDOWS):
        sl = slice(gi * POOL_GC, (gi + 1) * POOL_GC)
        cur = uext_ref[HALO:HALO + tm, sl]
        s = cur
        for j in range(1, w):
            s = s + uext_ref[HALO - j:HALO - j + tm, sl]
        count = jnp.minimum(pos + 1, w).astype(F32)
        d = s / count - cur
        yg = jnp.dot(d.astype(BF16), wpool_ref[gi], preferred_element_type=F32) * spool_ref[:, sl]
        groups.append(yg.astype(BF16))
    pooled = jnp.dot(jnp.concatenate(groups, axis=1), wpp_ref[...], preferred_element_type=F32)
    att = jnp.dot(ya_ref[...], watt_ref[...], preferred_element_type=F32)
    m = gates_ref[:, :D_MODEL] * pooled + gates_ref[:, D_MODEL:] * att
    mix = jnp.dot(m.astype(BF16), wout_ref[...], preferred_element_type=F32)

    xn_next = _layer_norm(xnext_ref[...], lng_ref[...], lnb_ref[...]).astype(BF16)

    def emit_next_gates(c):
        g, cols = _gate_chunk(xn_next, wg_ref, bg_ref, c)
        gates_ref[:, cols] = g

    emit_next_gates(0)

    h = _layer_norm(DEEPNORM_ALPHA * xn + mix, ln1g_ref[...], ln1b_ref[...])
    h_ref[...] = h
    packed = _pack_halves(h)
    hp_ref[0] = packed[:, :PLANE]
    hp_ref[1] = packed[:, PLANE:]
    emit_next_gates(1)

    hh = h.astype(BF16)
    hl = (h - hh.astype(F32)).astype(BF16)
    hi_pass = jnp.dot(hh, wr_ref[...], preferred_element_type=F32)
    lo_pass = jnp.dot(hl, wr_ref[:, :LANES], preferred_element_type=F32)
    logits = hi_pass[:, :LANES] + lo_pass + hi_pass[:, LANES:] + br_ref[...]
    emit_next_gates(2)
    rows = tm
    lane = lax.broadcasted_iota(I32, (rows, LANES), 1)
    work = jnp.where(lane < N_EXPERTS, logits, -jnp.inf)
    vals, idxs = [], []
    for _ in range(TOP_K):
        mx = jnp.max(work, axis=-1, keepdims=True)
        idx = jnp.min(jnp.where(work == mx, lane, LANES), axis=-1, keepdims=True)
        vals.append(mx)
        idxs.append(idx)
        work = jnp.where(lane == idx, -jnp.inf, work)
    exps = [jnp.exp(v - vals[0]) for v in vals]
    denom = exps[0] + exps[1] + exps[2] + exps[3]
    emit_next_gates(3)

    onehot = jnp.zeros((rows, LANES), F32)
    for idx in idxs:
        onehot = onehot + (lane == idx).astype(F32)
    base = _prefix_rows(onehot, rows, False, small_ints=True) + cnt_ref[...]
    code = jnp.zeros((rows, LANES), I32)
    tw = jnp.zeros((rows, LANES), F32)
    for k in range(TOP_K):
        rank = jnp.sum(jnp.where(lane == idxs[k], base, 0.0), axis=-1, keepdims=True)
        code = jnp.where(lane == k, idxs[k] * RANK_SPAN + rank.astype(I32), code)
        tw = jnp.where(lane == k, exps[k] / denom, tw)
    tw_ref[...] = tw
    if code_ref.shape[0] == CODE_ROWS:
        code_ref[...] = jnp.transpose(code)[:CODE_ROWS, :]
    else:
        code_ref[...] = code
    cnt_ref[...] = cnt_ref[...] + jnp.sum(onehot, axis=0, keepdims=True)


def _mixer(x, u, hist, yatt, cnt0, start_pos, weights, n_tok, row_off, prev):
    B, T, D = x.shape
    tm = min(SEQ_TILE, T)
    nt = T // tm
    row = lambda b, t: (b, t, 0)
    tok = lambda b, t: (row_off // tm + b * nt + t, 0)
    hpt = tm // HALO
    full = lambda a: pl.BlockSpec(a.shape, lambda b, t, _n=a.ndim: (0,) * _n)
    tok3 = lambda b, t: (0, row_off // tm + b * nt + t, 0)
    assert n_tok <= RANK_SPAN
    if tm % LANES == 0:
        code_spec = pl.BlockSpec((CODE_ROWS, tm), lambda b, t: (0, b * nt + t))
        code_shape = jax.ShapeDtypeStruct((CODE_ROWS, B * T), I32)
    else:
        code_spec = pl.BlockSpec((tm, LANES), lambda b, t: (b * nt + t, 0))
        code_shape = jax.ShapeDtypeStruct((B * T, LANES), I32)
    out_specs = [pl.BlockSpec((tm, D), tok), pl.BlockSpec((2, tm, PLANE), tok3), pl.BlockSpec((tm, LANES), tok),
                 code_spec, pl.BlockSpec((1, LANES), lambda b, t: (0, 0))]
    out_shape = [jax.ShapeDtypeStruct((n_tok, D), F32), jax.ShapeDtypeStruct((2, n_tok, PLANE), U32),
                 jax.ShapeDtypeStruct((n_tok, LANES), F32), code_shape, jax.ShapeDtypeStruct((1, LANES), F32)]
    n_in = 7 + len(weights)

    def next_row(b, t):
        lin = jnp.minimum(b * nt + t + 1, B * nt - 1)
        return (lin // nt, lin % nt, 0)

    return pl.pallas_call(
        functools.partial(_mixer_kernel, tm=tm, start_pos=start_pos),
        grid=(B, nt),
        in_specs=[
            pl.BlockSpec((None, tm, D), row),
            pl.BlockSpec((None, tm, D), next_row),
            pl.BlockSpec((None, tm, POOL_W), row),
            pl.BlockSpec((None, HALO, POOL_W), lambda b, t: (b, jnp.maximum(t * hpt - 1, 0), 0)),
            pl.BlockSpec((None, HALO, POOL_W), lambda b, t: (b, 0, 0)),
            pl.BlockSpec((None, tm, ATT_W), row),
            pl.BlockSpec((1, LANES), lambda b, t: (0, 0)),
        ] + [full(w) for w in weights] + [pl.BlockSpec(memory_space=pl.ANY) for _ in prev],
        out_specs=out_specs,
        out_shape=out_shape,
        scratch_shapes=[pltpu.VMEM((HALO + tm, POOL_W), F32), pltpu.VMEM((tm, 2 * D), F32)],
        input_output_aliases={n_in + i: i for i in range(len(prev))},
        compiler_params=pltpu.CompilerParams(
            dimension_semantics=("arbitrary", "arbitrary"), vmem_limit_bytes=VMEM_LIMIT),
    )(x, x, u, u, hist, yatt, cnt0, *weights, *prev)


def _sc_mesh():
    return plsc.VectorSubcoreMesh(core_axis_name="c", subcore_axis_name="s")


def _sc_scatter_rows(x, idx, n_rows):
    n = x.shape[0]

    @pl.kernel(out_type=jax.ShapeDtypeStruct((n_rows, PLANE), x.dtype), mesh=_sc_mesh(), scratch_types=[])
    def scatter(x_hbm, i_hbm, o_hbm):
        def body(x_vmem, i_vmem):
            for k in range(TOP_K):
                pltpu.sync_copy(x_vmem, o_hbm.at[i_vmem.at[k]])

        pltpu.emit_pipeline(
            body,
            grid=(n // SC_WINDOW,),
            in_specs=[pl.BlockSpec((SC_WINDOW, PLANE), index_map=lambda i: (i, 0)),
                      pl.BlockSpec((TOP_K, SC_WINDOW), index_map=lambda i: (0, i))],
            out_specs=[],
            core_axis_name=("c", "s"),
            dimension_semantics=(pltpu.PARALLEL,),
        )(x_hbm, i_hbm)

    return scatter(x, idx)


def _sc_gather_rows(y, idx):
    n = idx.shape[1]

    @pl.kernel(out_type=jax.ShapeDtypeStruct((n, PLANE), y.dtype), mesh=_sc_mesh(), scratch_types=[])
    def gather(y_hbm, i_hbm, o_hbm):
        def body(i_vmem, o_vmem):
            pltpu.sync_copy(y_hbm.at[i_vmem.at[0]], o_vmem)

        pltpu.emit_pipeline(
            body,
            grid=(n // SC_WINDOW,),
            in_specs=[pl.BlockSpec((1, SC_WINDOW), index_map=lambda i: (0, i))],
            out_specs=[pl.BlockSpec((SC_WINDOW, PLANE), index_map=lambda i: (i, 0))],
            core_axis_name=("c", "s"),
            dimension_semantics=(pltpu.PARALLEL,),
        )(i_hbm, o_hbm)

    return gather(y, idx)


def _join_planes(ref):
    return jnp.concatenate([ref[0], ref[1]], axis=1)


def _unpack_rows(xb_ref):
    xa, xc = _unpack_halves(_join_planes(xb_ref))
    return jnp.concatenate([xa.astype(BF16), xc.astype(BF16)], axis=1)


def _expert_kernel(be_ref, nb_ref, first_ref, xb_ref, xbnext_ref, w1_ref, b1_ref, w2_ref, b2_ref,
                   yb_ref, w1b_ref, w2b_ref, x_ref):
    del be_ref
    i = pl.program_id(0)

    @pl.when(i == 0)
    def _():
        x_ref[...] = _unpack_rows(xb_ref)

    @pl.when(first_ref[i] == 1)
    def _():
        w1b_ref[...] = w1_ref[...].astype(BF16)
        w2b_ref[...] = w2_ref[...].astype(BF16)

    @pl.when(i < nb_ref[0])
    def _():
        hfull = jnp.dot(x_ref[...], w1b_ref[...], preferred_element_type=F32) + b1_ref[...]
        glu = jnp.minimum(hfull[:, :D_FF], SWIGLU_LIMIT)
        lin = jnp.clip(hfull[:, D_FF:], -SWIGLU_LIMIT, SWIGLU_LIMIT)
        a = glu * jax.nn.sigmoid(SWIGLU_ALPHA * glu) * (lin + 1.0)
        x_next = _unpack_rows(xbnext_ref)
        y = jnp.dot(a.astype(BF16), w2b_ref[...], preferred_element_type=F32) + b2_ref[...]
        packed = _pack_halves(y)
        yb_ref[0] = packed[:, :PLANE]
        yb_ref[1] = packed[:, PLANE:]
        x_ref[...] = x_next

    @pl.when(i >= nb_ref[0])
    def _():
        yb_ref[...] = jnp.zeros_like(yb_ref)


def _experts(block_e, n_used, first, xb, w1, b1, w2, b2):
    n_rows = xb.shape[1]
    bm = MOE_BLOCK
    n_blocks = n_rows // bm
    rows = pl.BlockSpec((2, bm, PLANE), lambda i, be, nb, fi: (0, i, 0))
    next_rows = pl.BlockSpec((2, bm, PLANE), lambda i, be, nb, fi: (0, jnp.minimum(i + 1, n_blocks - 1), 0))
    per_expert = lambda r, c: pl.BlockSpec((None, r, c), lambda i, be, nb, fi: (be[i], 0, 0))
    return pl.pallas_call(
        _expert_kernel,
        grid_spec=pltpu.PrefetchScalarGridSpec(
            num_scalar_prefetch=3,
            grid=(n_blocks,),
            in_specs=[rows, next_rows, per_expert(D_MODEL, 2 * D_FF), per_expert(1, 2 * D_FF),
                      per_expert(D_FF, D_MODEL), per_expert(1, D_MODEL)],
            out_specs=rows,
            scratch_shapes=[pltpu.VMEM((D_MODEL, 2 * D_FF), BF16), pltpu.VMEM((D_FF, D_MODEL), BF16),
                            pltpu.VMEM((bm, D_MODEL), BF16)],
        ),
        out_shape=jax.ShapeDtypeStruct((2, n_rows, PLANE), U32),
        compiler_params=pltpu.CompilerParams(
            dimension_semantics=("arbitrary",), vmem_limit_bytes=VMEM_LIMIT),
    )(block_e, n_used, first, xb, xb, w1, b1, w2, b2)


def _combine_kernel(h_ref, tw_ref, g_ref, b_ref, y4_ref, out_ref):
    tw = tw_ref[...]
    acc_hi = None
    acc_lo = None
    for k in range(TOP_K):
        hi, lo = _unpack_halves(_join_planes(y4_ref.at[k]))
        w = tw[:, k:k + 1]
        acc_hi = w * hi if acc_hi is None else acc_hi + w * hi
        acc_lo = w * lo if acc_lo is None else acc_lo + w * lo
    moe = jnp.concatenate([acc_hi, acc_lo], axis=1)
    out_ref[...] = _layer_norm(DEEPNORM_ALPHA * h_ref[...] + moe, g_ref[...], b_ref[...])


def _combine(h, tw, ln2g, ln2b, y4, row_off, n_rows):
    D = h.shape[1]
    tn = min(COMBINE_TILE, n_rows)
    off = row_off // tn
    const = lambda i: (0, 0)
    return pl.pallas_call(
        _combine_kernel,
        grid=(n_rows // tn,),
        in_specs=[
            pl.BlockSpec((tn, D), lambda i: (off + i, 0)),
            pl.BlockSpec((tn, LANES), lambda i: (off + i, 0)),
            pl.BlockSpec((1, D), const), pl.BlockSpec((1, D), const),
            pl.BlockSpec((TOP_K, 2, tn, PLANE), lambda i: (0, 0, off + i, 0)),
        ],
        out_specs=pl.BlockSpec((tn, D), lambda i: (i, 0)),
        out_shape=jax.ShapeDtypeStruct((n_rows, D), F32),
        compiler_params=pltpu.CompilerParams(
            dimension_semantics=("arbitrary",), vmem_limit_bytes=VMEM_LIMIT),
    )(h, tw, ln2g, ln2b, y4)


def kernel(x_prompt, x_sample, cache_pool, cache_k, cache_v, cache_logf, ln_in_g, ln_in_b, w_in, b_in,
           w_pool, s_pool, w_pool_proj, w_att_proj, w_out, ln1_g, ln1_b, w_router, b_router,
           w1, b1, w2, b2, ln2_g, ln2_b):
    assert w_in.shape[0] == DEPTH
    B, T, D = x_prompt.shape
    Bs, Ts, _ = x_sample.shape
    P = cache_k.shape[2]
    row2 = lambda a: a.reshape(1, -1).astype(F32)

    f_off = MAIN_W
    g_off = MAIN_W + N_HEADS
    w_proj = jnp.pad(w_in[0][:, :g_off], ((0, 0), (0, LANES - N_HEADS))).astype(BF16)
    b_proj = row2(jnp.pad(b_in[0][:g_off], (0, LANES - N_HEADS)))
    wg = w_in[0][:, g_off:].astype(BF16)
    bg = row2(b_in[0][g_off:])
    lng, lnb = row2(ln_in_g), row2(ln_in_b)
    wr = jnp.pad(w_router[0], ((0, 0), (0, LANES - N_EXPERTS)))
    wr_hi = wr.astype(BF16)
    wr_lo = (wr - wr_hi.astype(F32)).astype(BF16)
    br = row2(jnp.pad(b_router[0], (0, LANES - N_EXPERTS)))
    mixer_weights = (lng, lnb, wg, bg, w_pool[0].astype(BF16), row2(s_pool[0]),
                     w_pool_proj[0].astype(BF16), w_att_proj[0].astype(BF16), w_out[0].astype(BF16),
                     row2(ln1_g[0]), row2(ln1_b[0]), jnp.concatenate([wr_hi, wr_lo], axis=1), br)

    zeros_f = jnp.zeros((B, 1, LANES), F32)
    u_p, k_p, v_p, logf_p, qa_p, ka_p, va_p, stats = _inproj(x_prompt, zeros_f, lng, lnb, w_proj, b_proj)
    tq, tk = min(ATT_TQ, T), min(ATT_TK, T)
    ya_p = _attention(_attn_skip_plan(stats, tq, tk, min(SEQ_TILE, T)), qa_p, ka_p, va_p, 0, tq, tk)

    clf = jnp.pad(cache_logf[0], ((0, 0), (0, 0), (0, LANES - N_HEADS)))
    tks = P + Ts + (-(P + Ts)) % LANES
    ka_c, va_c, f_tot = _cache_prep(cache_k[0].astype(BF16).reshape(Bs, P, ATT_W),
                                    cache_v[0].astype(BF16).reshape(Bs, P, ATT_W), clf, tks)
    u_s, k_s, v_s, logf_s, qa_s, ka_s, va_s, _ = _inproj(x_sample, f_tot, lng, lnb, w_proj, b_proj, (ka_c, va_c), P)
    visit_all = jnp.zeros((Bs * (N_HEADS // 2),), I32)
    ya_s = _attention(visit_all, qa_s, ka_s, va_s, P, Ts, tks)

    cnt0 = jnp.zeros((1, LANES), F32)
    hist_p = jnp.zeros((B, HALO, POOL_W), F32)
    n_tok = B * T + Bs * Ts
    *bufs, code_p, cnt_p = _mixer(x_prompt, u_p, hist_p, ya_p, cnt0, 0, mixer_weights, n_tok, 0, ())
    u_full_s = jnp.concatenate([cache_pool[0].astype(F32), u_s], axis=1)
    hist_s = jnp.pad(cache_pool[0].astype(F32), ((0, 0), (HALO - POOL_HIST, 0), (0, 0)))
    h_all, hp_all, tw, code_s, cnt = _mixer(x_sample, u_s, hist_s, ya_s, cnt_p, P, mixer_weights,
                                            n_tok, B * T, tuple(bufs))
    by_slot = lambda c: c[:TOP_K] if c.shape[0] == CODE_ROWS else c[:, :TOP_K].T
    code = jnp.concatenate([by_slot(code_p), by_slot(code_s)], axis=1)
    ti, rk = code // RANK_SPAN, code % RANK_SPAN

    counts = cnt[0, :N_EXPERTS].astype(I32)
    padded = (counts + MOE_BLOCK - 1) // MOE_BLOCK * MOE_BLOCK
    pad_ends = jnp.cumsum(padded)
    pad_starts = pad_ends - padded
    n_blocks = (n_tok * TOP_K + N_EXPERTS * (MOE_BLOCK - 1) + MOE_BLOCK - 1) // MOE_BLOCK
    n_rows = n_blocks * MOE_BLOCK
    block_start = jnp.arange(n_blocks, dtype=I32) * MOE_BLOCK
    block_e = jnp.minimum(jnp.sum((pad_ends[None, :] <= block_start[:, None]).astype(I32), axis=1), N_EXPERTS - 1)
    n_used = (pad_ends[-1:] // MOE_BLOCK).astype(I32)
    start_of = sum(jnp.where(ti == e, pad_starts[e], 0) for e in range(N_EXPERTS))
    dest_t = (start_of + rk).astype(I32)
    dest_planes = jnp.stack([dest_t, dest_t + n_rows], axis=1)

    xb = _sc_scatter_rows(hp_all.reshape(2 * n_tok, PLANE), dest_planes.reshape(TOP_K, 2 * n_tok), 2 * n_rows)
    first = jnp.concatenate([jnp.ones((1,), I32), (block_e[1:] != block_e[:-1]).astype(I32)])
    yb = _experts(block_e, n_used, first, xb.reshape(2, n_rows, PLANE),
                  w1[0], b1[0][:, None, :], w2[0], b2[0][:, None, :])
    y4 = _sc_gather_rows(yb.reshape(2 * n_rows, PLANE), dest_planes.reshape(1, TOP_K * 2 * n_tok))
    y4 = y4.reshape(TOP_K, 2, n_tok, PLANE)
    ln2g, ln2b = row2(ln2_g[0]), row2(ln2_b[0])
    y_prompt = _combine(h_all, tw, ln2g, ln2b, y4, 0, B * T).reshape(B, T, D)
    y_sample = _combine(h_all, tw, ln2g, ln2b, y4, B * T, Bs * Ts).reshape(Bs, Ts, D)
    heads = lambda a, b_, t_: a.reshape(1, b_, t_, N_HEADS, HEAD_DIM)
    return (y_prompt, y_sample,
            heads(k_p, B, T), heads(v_p, B, T), logf_p[None], u_p[:, -POOL_HIST:][None],
            heads(k_s, Bs, Ts), heads(v_s, Bs, Ts), logf_s[None], u_full_s[:, -POOL_HIST:][None])
```

```python
import functools

import jax
import jax.numpy as jnp
from jax import lax
from jax.experimental import pallas as pl
from jax.experimental.pallas import tpu as pltpu
from jax.experimental.pallas import tpu_sc as plsc

F32 = jnp.float32
BF16 = jnp.bfloat16
I32 = jnp.int32
U32 = jnp.uint32

D_MODEL = 1024
N_HEADS = 8
HEAD_DIM = 64
ATT_W = N_HEADS * HEAD_DIM
POOL_WINDOWS = (2, 4, 8, 16)
POOL_GC = 128
POOL_W = len(POOL_WINDOWS) * POOL_GC
POOL_HIST = max(POOL_WINDOWS) - 1
HALO = 16
N_EXPERTS = 32
TOP_K = 4
D_FF = D_MODEL
SWIGLU_ALPHA = 1.702
SWIGLU_LIMIT = 7.0
LN_EPS = 1e-5
DEPTH = 1
DEEPNORM_ALPHA = (2.0 * DEPTH) ** 0.25
ATT_SCALE = HEAD_DIM ** -0.5
LOG2E = 1.4426950408889634
MAIN_W = POOL_W + 3 * ATT_W
LANES = 128
AUG_W = LANES
HALF = D_MODEL // 2
VMEM_LIMIT = 56 * 1024 * 1024

SEQ_TILE = 512
COMBINE_TILE = 1024
ATT_TQ = 1024
ATT_TK = 1024
N_STATS = 4
SKIP_LOG2 = 150.0
MOE_BLOCK = 512
RANK_SPAN = 1 << 20
CODE_ROWS = 8
PLANE = HALF // 2
SC_WINDOW = 128


def _layer_norm(x, g, b):
    mu = jnp.mean(x, axis=-1, keepdims=True)
    xc = x - mu
    var = jnp.mean(xc * xc, axis=-1, keepdims=True)
    return xc * lax.rsqrt(var + LN_EPS) * g + b


def _split3(x):
    a = x.astype(BF16)
    r = x - a.astype(F32)
    b = r.astype(BF16)
    c = (r - b.astype(F32)).astype(BF16)
    return a, b, c


def _prefix_rows(x, tm, inclusive, small_ints=False):
    kp = max(tm, LANES)
    r = lax.broadcasted_iota(I32, (tm, kp), 0)
    c = lax.broadcasted_iota(I32, (tm, kp), 1)
    tri = ((c <= r) if inclusive else (c < r)).astype(BF16)
    if kp > tm:
        x = jnp.concatenate([x, jnp.zeros((kp - tm, x.shape[1]), x.dtype)], axis=0)
    out = None
    for piece in ((x.astype(BF16),) if small_ints else _split3(x)):
        y = jnp.dot(tri, piece, preferred_element_type=F32)
        out = y if out is None else out + y
    return out


def _pack_halves(y):
    hi = pltpu.bitcast(y[:, :HALF].astype(BF16).astype(F32), U32)
    lo = pltpu.bitcast(y[:, HALF:].astype(BF16).astype(F32), U32)
    return hi | (lo >> 16)


def _unpack_halves(w):
    hi = pltpu.bitcast(w & jnp.uint32(0xFFFF0000), F32)
    lo = pltpu.bitcast(w << 16, F32)
    return hi, lo


def _head_slab(p, off, h):
    s = p[:, off + (h // 2) * LANES: off + (h // 2) * LANES + LANES]
    return s if h % 2 == 0 else pltpu.roll(s, HEAD_DIM, 1)


def _augment(qs, ks, vs, f_col, lane):
    fc = jnp.broadcast_to(f_col, lane.shape)
    hi = fc.astype(BF16).astype(F32)
    r1 = fc - hi
    mid = r1.astype(BF16).astype(F32)
    lo = r1 - mid
    one = jnp.ones_like(fc)
    zero = jnp.zeros_like(fc)
    d = HEAD_DIM
    ka = jnp.where(lane < d, ks, jnp.where(lane < d + 3, one, jnp.where(
        lane == d + 3, -hi, jnp.where(lane == d + 4, -mid, jnp.where(lane == d + 5, -lo, zero)))))
    va = jnp.where(lane < d, vs, jnp.where(lane == d, one, zero))
    if qs is None:
        return None, ka.astype(BF16), va.astype(BF16)
    qa = jnp.where(lane < d, qs, jnp.where(lane == d, hi, jnp.where(
        lane == d + 1, mid, jnp.where(lane == d + 2, lo, jnp.where(lane < d + 6, one, zero)))))
    return qa.astype(BF16), ka.astype(BF16), va.astype(BF16)


PROJ_W = MAIN_W + LANES
PROJ_CHUNKS = ((0, POOL_W), (POOL_W, POOL_W + ATT_W), (POOL_W + ATT_W, POOL_W + 2 * ATT_W),
               (POOL_W + 2 * ATT_W, PROJ_W))


def _inproj_kernel(x_ref, xnext_ref, f0_ref, lng_ref, lnb_ref, w_ref, b_ref, *rest, tm):
    u_ref, k_ref, v_ref, logf_ref, qa_ref, ka_ref, va_ref, st_ref, carry_ref, proj_ref = rest[-10:]
    t = pl.program_id(1)
    lin = pl.program_id(0) * pl.num_programs(1) + t

    @pl.when(t == 0)
    def _():
        carry_ref[...] = f0_ref[...]

    def project(xn_bf, chunk):
        lo, hi = PROJ_CHUNKS[chunk]
        return jnp.dot(xn_bf, w_ref[:, lo:hi], preferred_element_type=F32) + b_ref[:, lo:hi]

    @pl.when(lin == 0)
    def _():
        xn = _layer_norm(x_ref[...], lng_ref[...], lnb_ref[...]).astype(BF16)
        for c, (lo, hi) in enumerate(PROJ_CHUNKS):
            proj_ref[:, lo:hi] = project(xn, c)

    p = proj_ref
    next_chunks = []
    xn_next = _layer_norm(xnext_ref[...], lng_ref[...], lnb_ref[...]).astype(BF16)
    fl = p[:, MAIN_W:]
    logf = jnp.minimum(fl, 0.0) - jnp.log(1.0 + jnp.exp(-jnp.abs(fl)))
    f_cum = _prefix_rows(logf, tm, True) + carry_ref[...]
    carry_ref[...] = f_cum[tm - 1:tm, :]

    u_ref[...] = p[:, :POOL_W]
    k_ref[...] = p[:, POOL_W + ATT_W:POOL_W + 2 * ATT_W]
    v_ref[...] = p[:, POOL_W + 2 * ATT_W:MAIN_W]
    logf_ref[...] = logf[:, :N_HEADS]

    lane = lax.broadcasted_iota(I32, (tm, AUG_W), 1)
    for h in range(N_HEADS):
        qs = _head_slab(p, POOL_W, h) * (ATT_SCALE * LOG2E)
        ks = _head_slab(p, POOL_W + ATT_W, h)
        vs = _head_slab(p, POOL_W + 2 * ATT_W, h)
        qa, ka, va = _augment(qs, ks, vs, f_cum[:, h:h + 1] * LOG2E, lane)
        qa_ref[h] = qa
        ka_ref[h] = ka
        va_ref[h] = va
        if h % 2 == 1:
            next_chunks.append(project(xn_next, h // 2))

    col = lax.broadcasted_iota(I32, (ATT_W, LANES), 0) // HEAD_DIM
    head_sum = (col == lax.broadcasted_iota(I32, (ATT_W, LANES), 1)).astype(BF16)
    qsec = (p[:, POOL_W:POOL_W + ATT_W] * (ATT_SCALE * LOG2E)).astype(BF16).astype(F32)
    ksec = p[:, POOL_W + ATT_W:POOL_W + 2 * ATT_W].astype(BF16).astype(F32)
    for r, sec in enumerate((qsec, ksec)):
        sq = jnp.dot((sec * sec).astype(BF16), head_sum, preferred_element_type=F32)
        st_ref[r:r + 1, :] = jnp.max(sq, axis=0, keepdims=True)
    st_ref[2:3, :] = f_cum[0:1, :] * LOG2E
    st_ref[3:4, :] = f_cum[tm - 1:tm, :] * LOG2E
    for (lo, hi), chunk in zip(PROJ_CHUNKS, next_chunks):
        proj_ref[:, lo:hi] = chunk


def _inproj(x, f0, lng, lnb, w, bias, kv_prev=(), kv_row_off=0):
    B, T, D = x.shape
    tm = min(SEQ_TILE, T)
    nt = T // tm
    grid = (B, nt)
    const = lambda b, t: (0, 0)
    row = lambda b, t: (b, t, 0)

    def next_row(b, t):
        lin = jnp.minimum(b * nt + t + 1, B * nt - 1)
        return (lin // nt, lin % nt, 0)

    aug = pl.BlockSpec((None, N_HEADS, tm, AUG_W), lambda b, t: (b, 0, t, 0))
    aug_shape = jax.ShapeDtypeStruct((B, N_HEADS, T, AUG_W), BF16)
    kv = pl.BlockSpec((None, N_HEADS, tm, AUG_W), lambda b, t: (b, 0, kv_row_off // tm + t, 0))
    kv_shape = jax.ShapeDtypeStruct(kv_prev[0].shape, BF16) if kv_prev else aug_shape
    n_in = 7
    return pl.pallas_call(
        functools.partial(_inproj_kernel, tm=tm),
        grid=grid,
        in_specs=[
            pl.BlockSpec((None, tm, D), row),
            pl.BlockSpec((None, tm, D), next_row),
            pl.BlockSpec((None, 1, LANES), lambda b, t: (b, 0, 0)),
            pl.BlockSpec((1, D), const), pl.BlockSpec((1, D), const),
            pl.BlockSpec((D, PROJ_W), const), pl.BlockSpec((1, PROJ_W), const),
        ] + [pl.BlockSpec(memory_space=pl.ANY) for _ in kv_prev],
        out_specs=[
            pl.BlockSpec((None, tm, POOL_W), row),
            pl.BlockSpec((None, tm, ATT_W), row),
            pl.BlockSpec((None, tm, ATT_W), row),
            pl.BlockSpec((None, tm, N_HEADS), row),
            aug, kv, kv,
            pl.BlockSpec((None, None, N_STATS, LANES), lambda b, t: (b, t, 0, 0)),
        ],
        out_shape=[
            jax.ShapeDtypeStruct((B, T, POOL_W), F32),
            jax.ShapeDtypeStruct((B, T, ATT_W), F32),
            jax.ShapeDtypeStruct((B, T, ATT_W), F32),
            jax.ShapeDtypeStruct((B, T, N_HEADS), F32),
            aug_shape, kv_shape, kv_shape,
            jax.ShapeDtypeStruct((B, T // tm, N_STATS, LANES), F32),
        ],
        scratch_shapes=[pltpu.VMEM((1, LANES), F32), pltpu.VMEM((tm, PROJ_W), F32)],
        input_output_aliases={n_in + i: 5 + i for i in range(len(kv_prev))},
        compiler_params=pltpu.CompilerParams(
            dimension_semantics=("arbitrary", "arbitrary"), vmem_limit_bytes=VMEM_LIMIT),
    )(x, x, f0, lng, lnb, w, bias, *kv_prev)


def _cache_kernel(ck_ref, cv_ref, clf_ref, ka_ref, va_ref, ftot_ref, carry_ref, *, tp):
    t = pl.program_id(1)

    @pl.when(t == 0)
    def _():
        carry_ref[...] = jnp.zeros_like(carry_ref)

    @pl.when(t < pl.num_programs(1) - 1)
    def _():
        f_cum = _prefix_rows(clf_ref[...], tp, True) + carry_ref[...]
        carry_ref[...] = f_cum[tp - 1:tp, :]
        ftot_ref[...] = f_cum[tp - 1:tp, :]
        ck = ck_ref[...].astype(F32)
        cv = cv_ref[...].astype(F32)
        lane = lax.broadcasted_iota(I32, (tp, AUG_W), 1)
        for h in range(N_HEADS):
            _, ka, va = _augment(None, _head_slab(ck, 0, h), _head_slab(cv, 0, h), f_cum[:, h:h + 1] * LOG2E, lane)
            ka_ref[h] = ka
            va_ref[h] = va

    @pl.when(t == pl.num_programs(1) - 1)
    def _():
        ka_ref[...] = jnp.zeros_like(ka_ref)
        va_ref[...] = jnp.zeros_like(va_ref)


def _cache_prep(ck, cv, clf, n_keys):
    B, P, _ = ck.shape
    tp = min(SEQ_TILE, P)
    nt = P // tp
    row = lambda b, t: (b, jnp.minimum(t, nt - 1), 0)
    aug = pl.BlockSpec((None, N_HEADS, tp, AUG_W), lambda b, t: (b, 0, t, 0))
    aug_shape = jax.ShapeDtypeStruct((B, N_HEADS, n_keys, AUG_W), BF16)
    assert P < n_keys <= P + tp
    return pl.pallas_call(
        functools.partial(_cache_kernel, tp=tp),
        grid=(B, nt + 1),
        in_specs=[pl.BlockSpec((None, tp, ATT_W), row), pl.BlockSpec((None, tp, ATT_W), row),
                  pl.BlockSpec((None, tp, LANES), row)],
        out_specs=[aug, aug, pl.BlockSpec((None, 1, LANES), lambda b, t: (b, 0, 0))],
        out_shape=[aug_shape, aug_shape, jax.ShapeDtypeStruct((B, 1, LANES), F32)],
        scratch_shapes=[pltpu.VMEM((1, LANES), F32)],
        compiler_params=pltpu.CompilerParams(
            dimension_semantics=("arbitrary", "arbitrary"), vmem_limit_bytes=VMEM_LIMIT),
    )(ck, cv, clf)


def _attn_kernel(jmin_ref, qa_ref, ka_ref, va_ref, o_ref, *, tq, tk, q_off):
    b, hp, iq = pl.program_id(0), pl.program_id(1), pl.program_id(2)
    q_lo = q_off + iq * tq
    n_full = (q_lo + 1) // tk
    j_first = jmin_ref[(b * pl.num_programs(1) + hp) * pl.num_programs(2) + iq]
    lane = lax.broadcasted_iota(I32, (tq, AUG_W), 1)
    qs = (qa_ref[0], qa_ref[1])

    def visit(carry, k0, nk, r0, masked):
        new = []
        for hh in range(2):
            m_all, acc_all = carry[hh]
            m, acc, q = m_all[r0:], acc_all[r0:], qs[hh][r0:]
            k = ka_ref[hh, pl.ds(k0, nk), :]
            v = va_ref[hh, pl.ds(k0, nk), :]
            s = lax.dot_general(q, k, (((1,), (1,)), ((), ())), preferred_element_type=F32)
            if masked:
                qpos = q_lo + r0 + lax.broadcasted_iota(I32, (tq - r0, nk), 0)
                kpos = k0 + lax.broadcasted_iota(I32, (tq - r0, nk), 1)
                s = jnp.where(qpos >= kpos, s, -jnp.inf)
            m_new = jnp.maximum(m, jnp.max(s, axis=-1, keepdims=True))
            alpha = jnp.exp2(m - m_new)
            p = jnp.exp2(s - m_new)
            acc = acc * alpha + jnp.dot(p.astype(BF16), v, preferred_element_type=F32)
            if r0:
                m_new = jnp.concatenate([m_all[:r0], m_new], axis=0)
                acc = jnp.concatenate([acc_all[:r0], acc], axis=0)
            new.append((m_new, acc))
        return tuple(new)

    def step(j, carry):
        return visit(carry, pl.multiple_of(j * tk, tk), tk, 0, False)

    init = (jnp.full((tq, 1), -jnp.inf, F32), jnp.zeros((tq, AUG_W), F32))
    carry = lax.fori_loop(j_first, n_full, step, (init, init))
    k_diag = pl.multiple_of(n_full * tk, tk)
    if tq == tk and tq % (2 * LANES) == 0:
        half = tq // 2
        carry = visit(carry, k_diag, half, 0, True)
        carry = visit(carry, pl.multiple_of(k_diag + half, half), half, half, True)
    else:
        carry = visit(carry, k_diag, tk, 0, True)
    outs = [acc / acc[:, HEAD_DIM:HEAD_DIM + 1] for _, acc in carry]
    o_ref[...] = jnp.where(lane < HEAD_DIM, outs[0], pltpu.roll(outs[1], HEAD_DIM, 1)).astype(BF16)


def _attention(jmin, qa, ka, va, q_off, tq, tk):
    B, H, Tq, _ = qa.shape
    Tk = ka.shape[2]
    assert tk % tq == 0 and q_off % tq == 0 and tq > 1
    return pl.pallas_call(
        functools.partial(_attn_kernel, tq=tq, tk=tk, q_off=q_off),
        grid_spec=pltpu.PrefetchScalarGridSpec(
            num_scalar_prefetch=1,
            grid=(B, H // 2, Tq // tq),
            in_specs=[
                pl.BlockSpec((None, 2, tq, AUG_W), lambda b, hp, iq, jm: (b, hp, iq, 0)),
                pl.BlockSpec((None, 2, Tk, AUG_W), lambda b, hp, iq, jm: (b, hp, 0, 0)),
                pl.BlockSpec((None, 2, Tk, AUG_W), lambda b, hp, iq, jm: (b, hp, 0, 0)),
            ],
            out_specs=pl.BlockSpec((None, tq, 2 * HEAD_DIM), lambda b, hp, iq, jm: (b, iq, hp)),
        ),
        out_shape=jax.ShapeDtypeStruct((B, Tq, ATT_W), BF16),
        compiler_params=pltpu.CompilerParams(
            dimension_semantics=("arbitrary", "arbitrary", "arbitrary"), vmem_limit_bytes=VMEM_LIMIT),
    )(jmin, qa, ka, va)


def _attn_skip_plan(stats, tq, tk, tm):
    B, nt = stats.shape[:2]
    st = stats[..., :N_HEADS]
    per = lambda row, r: st[:, :, row].reshape(B, nt // r, r, N_HEADS)
    rq, rk = tq // tm, tk // tm
    nq, nk = nt // rq, nt // rk
    qn2, kn2, kn2_own = per(0, rq).max(2), per(1, rk).max(2), per(1, rq).max(2)
    f_first, f_last = per(2, rq)[:, :, 0], per(3, rk)[:, :, -1]
    slack = 1.01
    upper = jnp.sqrt(qn2[:, :, None] * kn2[:, None, :]) * slack + (f_first[:, :, None] - f_last[:, None, :])
    lower = -jnp.sqrt(qn2 * kn2_own) * slack
    weightless = upper - lower[:, :, None] <= -(SKIP_LOG2 + 2.0)
    j = jnp.arange(nk, dtype=I32)[None, None, :, None]
    n_full = ((jnp.arange(nq, dtype=I32) * tq + 1) // tk)[None, :, None, None]
    first = jnp.min(jnp.where(weightless | (j >= n_full), n_full, j), axis=2)
    first = jnp.min(first.reshape(B, nq, N_HEADS // 2, 2), axis=3)
    return jnp.swapaxes(first, 1, 2).reshape(-1).astype(I32)


GATE_CHUNKS = 8


def _gate_chunk(xn_bf, wg_ref, bg_ref, c):
    cols = slice(c * (2 * D_MODEL // GATE_CHUNKS), (c + 1) * (2 * D_MODEL // GATE_CHUNKS))
    return jax.nn.sigmoid(jnp.dot(xn_bf, wg_ref[:, cols], preferred_element_type=F32) + bg_ref[:, cols]), cols


def _mixer_kernel(x_ref, xnext_ref, u_ref, uprev_ref, hist_ref, ya_ref, cnt0_ref,
                  lng_ref, lnb_ref, wg_ref, bg_ref, wpool_ref, spool_ref, wpp_ref, watt_ref, wout_ref,
                  ln1g_ref, ln1b_ref, wr_ref, br_ref, *rest, tm, start_pos):
    h_ref, hp_ref, tw_ref, code_ref, cnt_ref, uext_ref, gates_ref = rest[-7:]
    b = pl.program_id(0)
    t = pl.program_id(1)
    xn = _layer_norm(x_ref[...], lng_ref[...], lnb_ref[...])

    @pl.when((b == 0) & (t == 0))
    def _():
        cnt_ref[...] = cnt0_ref[...]
        for c in range(GATE_CHUNKS):
            g, cols = _gate_chunk(xn.astype(BF16), wg_ref, bg_ref, c)
            gates_ref[:, cols] = g

    @pl.when(t == 0)
    def _():
        uext_ref[0:HALO, :] = hist_ref[...]

    @pl.when(t > 0)
    def _():
        uext_ref[0:HALO, :] = uprev_ref[...]

    uext_ref[HALO:HALO + tm, :] = u_ref[...]

    pos = start_pos + t * tm + lax.broadcasted_iota(I32, (tm, 1), 0)
    groups = []
    for gi, w in enumerate(POOL_WINDOWS):
        sl = slice(gi * POOL_GC, (gi + 1) * POOL_GC)
        cur = uext_ref[HALO:HALO + tm, sl]
        s = cur
        for j in range(1, w):
            s = s + uext_ref[HALO - j:HALO - j + tm, sl]
        count = jnp.minimum(pos + 1, w).astype(F32)
        d = s / count - cur
        yg = jnp.dot(d.astype(BF16), wpool_ref[gi], preferred_element_type=F32) * spool_ref[:, sl]
        groups.append(yg.astype(BF16))
    pooled = jnp.dot(jnp.concatenate(groups, axis=1), wpp_ref[...], preferred_element_type=F32)
    att = jnp.dot(ya_ref[...], watt_ref[...], preferred_element_type=F32)
    m = gates_ref[:, :D_MODEL] * pooled + gates_ref[:, D_MODEL:] * att
    mix = jnp.dot(m.astype(BF16), wout_ref[...], preferred_element_type=F32)

    xn_next = _layer_norm(xnext_ref[...], lng_ref[...], lnb_ref[...]).astype(BF16)

    def emit_next_gates(c):
        g, cols = _gate_chunk(xn_next, wg_ref, bg_ref, c)
        gates_ref[:, cols] = g

    emit_next_gates(0)

    h = _layer_norm(DEEPNORM_ALPHA * xn + mix, ln1g_ref[...], ln1b_ref[...])
    h_ref[...] = h
    packed = _pack_halves(h)
    hp_ref[0] = packed[:, :PLANE]
    hp_ref[1] = packed[:, PLANE:]
    emit_next_gates(1)
    emit_next_gates(2)

    hh = h.astype(BF16)
    hl = (h - hh.astype(F32)).astype(BF16)
    hi_pass = jnp.dot(hh, wr_ref[...], preferred_element_type=F32)
    lo_pass = jnp.dot(hl, wr_ref[:, :LANES], preferred_element_type=F32)
    logits = hi_pass[:, :LANES] + lo_pass + hi_pass[:, LANES:] + br_ref[...]
    emit_next_gates(3)
    lane = lax.broadcasted_iota(I32, (tm, LANES), 1)
    work = jnp.where(lane < N_EXPERTS, logits, -jnp.inf)
    vals, idxs = [], []
    for r in range(TOP_K):
        mx = jnp.max(work, axis=-1, keepdims=True)
        idx = jnp.min(jnp.where(work == mx, lane, LANES), axis=-1, keepdims=True)
        vals.append(mx)
        idxs.append(idx)
        work = jnp.where(lane == idx, -jnp.inf, work)
        emit_next_gates(GATE_CHUNKS - TOP_K + r)
    exps = [jnp.exp(v - vals[0]) for v in vals]
    denom = exps[0] + exps[1] + exps[2] + exps[3]

    onehot = jnp.zeros((tm, LANES), F32)
    for idx in idxs:
        onehot = onehot + (lane == idx).astype(F32)
    base = _prefix_rows(onehot, tm, False, small_ints=True) + cnt_ref[...]
    code = jnp.zeros((tm, LANES), I32)
    tw = jnp.zeros((tm, LANES), F32)
    for k in range(TOP_K):
        rank = jnp.sum(jnp.where(lane == idxs[k], base, 0.0), axis=-1, keepdims=True)
        code = jnp.where(lane == k, idxs[k] * RANK_SPAN + rank.astype(I32), code)
        tw = jnp.where(lane == k, exps[k] / denom, tw)
    tw_ref[...] = tw
    if code_ref.shape[0] == CODE_ROWS:
        code_ref[...] = jnp.transpose(code)[:CODE_ROWS, :]
    else:
        code_ref[...] = code
    cnt_ref[...] = cnt_ref[...] + jnp.sum(onehot, axis=0, keepdims=True)


def _mixer(x, u, hist, yatt, cnt0, start_pos, weights, n_tok, row_off, prev):
    B, T, D = x.shape
    tm = min(SEQ_TILE, T)
    nt = T // tm
    row = lambda b, t: (b, t, 0)
    tok = lambda b, t: (row_off // tm + b * nt + t, 0)
    hpt = tm // HALO
    full = lambda a: pl.BlockSpec(a.shape, lambda b, t, _n=a.ndim: (0,) * _n)
    tok3 = lambda b, t: (0, row_off // tm + b * nt + t, 0)
    assert n_tok <= RANK_SPAN
    if tm % LANES == 0:
        code_spec = pl.BlockSpec((CODE_ROWS, tm), lambda b, t: (0, b * nt + t))
        code_shape = jax.ShapeDtypeStruct((CODE_ROWS, B * T), I32)
    else:
        code_spec = pl.BlockSpec((tm, LANES), lambda b, t: (b * nt + t, 0))
        code_shape = jax.ShapeDtypeStruct((B * T, LANES), I32)
    out_specs = [pl.BlockSpec((tm, D), tok), pl.BlockSpec((2, tm, PLANE), tok3), pl.BlockSpec((tm, LANES), tok),
                 code_spec, pl.BlockSpec((1, LANES), lambda b, t: (0, 0))]
    out_shape = [jax.ShapeDtypeStruct((n_tok, D), F32), jax.ShapeDtypeStruct((2, n_tok, PLANE), U32),
                 jax.ShapeDtypeStruct((n_tok, LANES), F32), code_shape, jax.ShapeDtypeStruct((1, LANES), F32)]
    n_in = 7 + len(weights)

    def next_row(b, t):
        lin = jnp.minimum(b * nt + t + 1, B * nt - 1)
        return (lin // nt, lin % nt, 0)

    return pl.pallas_call(
        functools.partial(_mixer_kernel, tm=tm, start_pos=start_pos),
        grid=(B, nt),
        in_specs=[
            pl.BlockSpec((None, tm, D), row),
            pl.BlockSpec((None, tm, D), next_row),
            pl.BlockSpec((None, tm, POOL_W), row),
            pl.BlockSpec((None, HALO, POOL_W), lambda b, t: (b, jnp.maximum(t * hpt - 1, 0), 0)),
            pl.BlockSpec((None, HALO, POOL_W), lambda b, t: (b, 0, 0)),
            pl.BlockSpec((None, tm, ATT_W), row),
            pl.BlockSpec((1, LANES), lambda b, t: (0, 0)),
        ] + [full(w) for w in weights] + [pl.BlockSpec(memory_space=pl.ANY) for _ in prev],
        out_specs=out_specs,
        out_shape=out_shape,
        scratch_shapes=[pltpu.VMEM((HALO + tm, POOL_W), F32), pltpu.VMEM((tm, 2 * D), F32)],
        input_output_aliases={n_in + i: i for i in range(len(prev))},
        compiler_params=pltpu.CompilerParams(
            dimension_semantics=("arbitrary", "arbitrary"), vmem_limit_bytes=VMEM_LIMIT),
    )(x, x, u, u, hist, yatt, cnt0, *weights, *prev)


def _sc_mesh():
    return plsc.VectorSubcoreMesh(core_axis_name="c", subcore_axis_name="s")


def _sc_scatter_rows(x, idx, n_rows):
    n = x.shape[0]

    @pl.kernel(out_type=jax.ShapeDtypeStruct((n_rows, PLANE), x.dtype), mesh=_sc_mesh(), scratch_types=[])
    def scatter(x_hbm, i_hbm, o_hbm):
        def body(x_vmem, i_vmem):
            for k in range(TOP_K):
                pltpu.sync_copy(x_vmem, o_hbm.at[i_vmem.at[k]])

        pltpu.emit_pipeline(
            body,
            grid=(n // SC_WINDOW,),
            in_specs=[pl.BlockSpec((SC_WINDOW, PLANE), index_map=lambda i: (i, 0)),
                      pl.BlockSpec((TOP_K, SC_WINDOW), index_map=lambda i: (0, i))],
            out_specs=[],
            core_axis_name=("c", "s"),
            dimension_semantics=(pltpu.PARALLEL,),
        )(x_hbm, i_hbm)

    return scatter(x, idx)


def _sc_gather_rows(y, idx):
    n = idx.shape[1]

    @pl.kernel(out_type=jax.ShapeDtypeStruct((n, PLANE), y.dtype), mesh=_sc_mesh(), scratch_types=[])
    def gather(y_hbm, i_hbm, o_hbm):
        def body(i_vmem, o_vmem):
            pltpu.sync_copy(y_hbm.at[i_vmem.at[0]], o_vmem)

        pltpu.emit_pipeline(
            body,
            grid=(n // SC_WINDOW,),
            in_specs=[pl.BlockSpec((1, SC_WINDOW), index_map=lambda i: (0, i))],
            out_specs=[pl.BlockSpec((SC_WINDOW, PLANE), index_map=lambda i: (i, 0))],
            core_axis_name=("c", "s"),
            dimension_semantics=(pltpu.PARALLEL,),
        )(i_hbm, o_hbm)

    return gather(y, idx)


def _join_planes(ref):
    return jnp.concatenate([ref[0], ref[1]], axis=1)


def _unpack_rows(xb_ref):
    xa, xc = _unpack_halves(_join_planes(xb_ref))
    return jnp.concatenate([xa.astype(BF16), xc.astype(BF16)], axis=1)


def _expert_kernel(be_ref, nb_ref, first_ref, xb_ref, xbnext_ref, w1_ref, b1_ref, w2_ref, b2_ref,
                   yb_ref, w1b_ref, w2b_ref, x_ref):
    del be_ref
    i = pl.program_id(0)

    @pl.when(i == 0)
    def _():
        x_ref[...] = _unpack_rows(xb_ref)

    @pl.when(first_ref[i] == 1)
    def _():
        w1b_ref[...] = w1_ref[...].astype(BF16)
        w2b_ref[...] = w2_ref[...].astype(BF16)

    @pl.when(i < nb_ref[0])
    def _():
        hfull = jnp.dot(x_ref[...], w1b_ref[...], preferred_element_type=F32) + b1_ref[...]
        glu = jnp.minimum(hfull[:, :D_FF], SWIGLU_LIMIT)
        lin = jnp.clip(hfull[:, D_FF:], -SWIGLU_LIMIT, SWIGLU_LIMIT)
        a = glu * jax.nn.sigmoid(SWIGLU_ALPHA * glu) * (lin + 1.0)
        x_next = _unpack_rows(xbnext_ref)
        y = jnp.dot(a.astype(BF16), w2b_ref[...], preferred_element_type=F32) + b2_ref[...]
        packed = _pack_halves(y)
        yb_ref[0] = packed[:, :PLANE]
        yb_ref[1] = packed[:, PLANE:]
        x_ref[...] = x_next

    @pl.when(i >= nb_ref[0])
    def _():
        yb_ref[...] = jnp.zeros_like(yb_ref)


def _experts(block_e, n_used, first, xb, w1, b1, w2, b2):
    n_rows = xb.shape[1]
    bm = MOE_BLOCK
    n_blocks = n_rows // bm
    rows = pl.BlockSpec((2, bm, PLANE), lambda i, be, nb, fi: (0, i, 0))
    next_rows = pl.BlockSpec((2, bm, PLANE), lambda i, be, nb, fi: (0, jnp.minimum(i + 1, n_blocks - 1), 0))
    per_expert = lambda r, c: pl.BlockSpec((None, r, c), lambda i, be, nb, fi: (be[i], 0, 0))
    return pl.pallas_call(
        _expert_kernel,
        grid_spec=pltpu.PrefetchScalarGridSpec(
            num_scalar_prefetch=3,
            grid=(n_blocks,),
            in_specs=[rows, next_rows, per_expert(D_MODEL, 2 * D_FF), per_expert(1, 2 * D_FF),
                      per_expert(D_FF, D_MODEL), per_expert(1, D_MODEL)],
            out_specs=rows,
            scratch_shapes=[pltpu.VMEM((D_MODEL, 2 * D_FF), BF16), pltpu.VMEM((D_FF, D_MODEL), BF16),
                            pltpu.VMEM((bm, D_MODEL), BF16)],
        ),
        out_shape=jax.ShapeDtypeStruct((2, n_rows, PLANE), U32),
        compiler_params=pltpu.CompilerParams(
            dimension_semantics=("arbitrary",), vmem_limit_bytes=VMEM_LIMIT),
    )(block_e, n_used, first, xb, xb, w1, b1, w2, b2)


def _combine_kernel(h_ref, tw_ref, g_ref, b_ref, y4_ref, out_ref):
    tw = tw_ref[...]
    acc_hi = None
    acc_lo = None
    for k in range(TOP_K):
        hi, lo = _unpack_halves(_join_planes(y4_ref.at[k]))
        w = tw[:, k:k + 1]
        acc_hi = w * hi if acc_hi is None else acc_hi + w * hi
        acc_lo = w * lo if acc_lo is None else acc_lo + w * lo
    moe = jnp.concatenate([acc_hi, acc_lo], axis=1)
    out_ref[...] = _layer_norm(DEEPNORM_ALPHA * h_ref[...] + moe, g_ref[...], b_ref[...])


def _combine(h, tw, ln2g, ln2b, y4, row_off, n_rows):
    D = h.shape[1]
    tn = min(COMBINE_TILE, n_rows)
    off = row_off // tn
    const = lambda i: (0, 0)
    return pl.pallas_call(
        _combine_kernel,
        grid=(n_rows // tn,),
        in_specs=[
            pl.BlockSpec((tn, D), lambda i: (off + i, 0)),
            pl.BlockSpec((tn, LANES), lambda i: (off + i, 0)),
            pl.BlockSpec((1, D), const), pl.BlockSpec((1, D), const),
            pl.BlockSpec((TOP_K, 2, tn, PLANE), lambda i: (0, 0, off + i, 0)),
        ],
        out_specs=pl.BlockSpec((tn, D), lambda i: (i, 0)),
        out_shape=jax.ShapeDtypeStruct((n_rows, D), F32),
        compiler_params=pltpu.CompilerParams(
            dimension_semantics=("arbitrary",), vmem_limit_bytes=VMEM_LIMIT),
    )(h, tw, ln2g, ln2b, y4)


def kernel(x_prompt, x_sample, cache_pool, cache_k, cache_v, cache_logf, ln_in_g, ln_in_b, w_in, b_in,
           w_pool, s_pool, w_pool_proj, w_att_proj, w_out, ln1_g, ln1_b, w_router, b_router,
           w1, b1, w2, b2, ln2_g, ln2_b):
    assert w_in.shape[0] == DEPTH
    B, T, D = x_prompt.shape
    Bs, Ts, _ = x_sample.shape
    P = cache_k.shape[2]
    row2 = lambda a: a.reshape(1, -1).astype(F32)

    f_off = MAIN_W
    g_off = MAIN_W + N_HEADS
    w_proj = jnp.pad(w_in[0][:, :g_off], ((0, 0), (0, LANES - N_HEADS))).astype(BF16)
    b_proj = row2(jnp.pad(b_in[0][:g_off], (0, LANES - N_HEADS)))
    wg = w_in[0][:, g_off:].astype(BF16)
    bg = row2(b_in[0][g_off:])
    lng, lnb = row2(ln_in_g), row2(ln_in_b)
    wr = jnp.pad(w_router[0], ((0, 0), (0, LANES - N_EXPERTS)))
    wr_hi = wr.astype(BF16)
    wr_lo = (wr - wr_hi.astype(F32)).astype(BF16)
    br = row2(jnp.pad(b_router[0], (0, LANES - N_EXPERTS)))
    mixer_weights = (lng, lnb, wg, bg, w_pool[0].astype(BF16), row2(s_pool[0]),
                     w_pool_proj[0].astype(BF16), w_att_proj[0].astype(BF16), w_out[0].astype(BF16),
                     row2(ln1_g[0]), row2(ln1_b[0]), jnp.concatenate([wr_hi, wr_lo], axis=1), br)

    zeros_f = jnp.zeros((B, 1, LANES), F32)
    u_p, k_p, v_p, logf_p, qa_p, ka_p, va_p, stats = _inproj(x_prompt, zeros_f, lng, lnb, w_proj, b_proj)
    tq, tk = min(ATT_TQ, T), min(ATT_TK, T)
    ya_p = _attention(_attn_skip_plan(stats, tq, tk, min(SEQ_TILE, T)), qa_p, ka_p, va_p, 0, tq, tk)

    clf = jnp.pad(cache_logf[0], ((0, 0), (0, 0), (0, LANES - N_HEADS)))
    tks = P + Ts + (-(P + Ts)) % LANES
    ka_c, va_c, f_tot = _cache_prep(cache_k[0].astype(BF16).reshape(Bs, P, ATT_W),
                                    cache_v[0].astype(BF16).reshape(Bs, P, ATT_W), clf, tks)
    u_s, k_s, v_s, logf_s, qa_s, ka_s, va_s, _ = _inproj(x_sample, f_tot, lng, lnb, w_proj, b_proj, (ka_c, va_c), P)
    visit_all = jnp.zeros((Bs * (N_HEADS // 2),), I32)
    ya_s = _attention(visit_all, qa_s, ka_s, va_s, P, Ts, tks)

    cnt0 = jnp.zeros((1, LANES), F32)
    hist_p = jnp.zeros((B, HALO, POOL_W), F32)
    n_tok = B * T + Bs * Ts
    *bufs, code_p, cnt_p = _mixer(x_prompt, u_p, hist_p, ya_p, cnt0, 0, mixer_weights, n_tok, 0, ())
    u_full_s = jnp.concatenate([cache_pool[0].astype(F32), u_s], axis=1)
    hist_s = jnp.pad(cache_pool[0].astype(F32), ((0, 0), (HALO - POOL_HIST, 0), (0, 0)))
    h_all, hp_all, tw, code_s, cnt = _mixer(x_sample, u_s, hist_s, ya_s, cnt_p, P, mixer_weights,
                                            n_tok, B * T, tuple(bufs))
    by_slot = lambda c: c[:TOP_K] if c.shape[0] == CODE_ROWS else c[:, :TOP_K].T
    code = jnp.concatenate([by_slot(code_p), by_slot(code_s)], axis=1)
    ti, rk = code // RANK_SPAN, code % RANK_SPAN

    counts = cnt[0, :N_EXPERTS].astype(I32)
    padded = (counts + MOE_BLOCK - 1) // MOE_BLOCK * MOE_BLOCK
    pad_ends = jnp.cumsum(padded)
    pad_starts = pad_ends - padded
    n_blocks = (n_tok * TOP_K + N_EXPERTS * (MOE_BLOCK - 1) + MOE_BLOCK - 1) // MOE_BLOCK
    n_rows = n_blocks * MOE_BLOCK
    block_start = jnp.arange(n_blocks, dtype=I32) * MOE_BLOCK
    block_e = jnp.minimum(jnp.sum((pad_ends[None, :] <= block_start[:, None]).astype(I32), axis=1), N_EXPERTS - 1)
    n_used = (pad_ends[-1:] // MOE_BLOCK).astype(I32)
    start_of = sum(jnp.where(ti == e, pad_starts[e], 0) for e in range(N_EXPERTS))
    dest_t = (start_of + rk).astype(I32)
    dest_planes = jnp.stack([dest_t, dest_t + n_rows], axis=1)

    xb = _sc_scatter_rows(hp_all.reshape(2 * n_tok, PLANE), dest_planes.reshape(TOP_K, 2 * n_tok), 2 * n_rows)
    first = jnp.concatenate([jnp.ones((1,), I32), (block_e[1:] != block_e[:-1]).astype(I32)])
    yb = _experts(block_e, n_used, first, xb.reshape(2, n_rows, PLANE),
                  w1[0], b1[0][:, None, :], w2[0], b2[0][:, None, :])
    y4 = _sc_gather_rows(yb.reshape(2 * n_rows, PLANE), dest_planes.reshape(1, TOP_K * 2 * n_tok))
    y4 = y4.reshape(TOP_K, 2, n_tok, PLANE)
    ln2g, ln2b = row2(ln2_g[0]), row2(ln2_b[0])
    y_prompt = _combine(h_all, tw, ln2g, ln2b, y4, 0, B * T).reshape(B, T, D)
    y_sample = _combine(h_all, tw, ln2g, ln2b, y4, B * T, Bs * Ts).reshape(Bs, Ts, D)
    heads = lambda a, b_, t_: a.reshape(1, b_, t_, N_HEADS, HEAD_DIM)
    return (y_prompt, y_sample,
            heads(k_p, B, T), heads(v_p, B, T), logf_p[None], u_p[:, -POOL_HIST:][None],
            heads(k_s, Bs, Ts), heads(v_s, Bs, Ts), logf_s[None], u_full_s[:, -POOL_HIST:][None])
```

```python
import functools

import jax
import jax.numpy as jnp
from jax import lax
from jax.experimental import pallas as pl
from jax.experimental.pallas import tpu as pltpu
from jax.experimental.pallas import tpu_sc as plsc

F32 = jnp.float32
BF16 = jnp.bfloat16
I32 = jnp.int32
U32 = jnp.uint32

D_MODEL = 1024
N_HEADS = 8
HEAD_DIM = 64
ATT_W = N_HEADS * HEAD_DIM
POOL_WINDOWS = (2, 4, 8, 16)
POOL_GC = 128
POOL_W = len(POOL_WINDOWS) * POOL_GC
POOL_HIST = max(POOL_WINDOWS) - 1
HALO = 16
N_EXPERTS = 32
TOP_K = 4
D_FF = D_MODEL
SWIGLU_ALPHA = 1.702
SWIGLU_LIMIT = 7.0
LN_EPS = 1e-5
DEPTH = 1
DEEPNORM_ALPHA = (2.0 * DEPTH) ** 0.25
ATT_SCALE = HEAD_DIM ** -0.5
LOG2E = 1.4426950408889634
MAIN_W = POOL_W + 3 * ATT_W
LANES = 128
AUG_W = LANES
HALF = D_MODEL // 2
VMEM_LIMIT = 56 * 1024 * 1024

SEQ_TILE = 512
COMBINE_TILE = 1024
ATT_TQ = 1024
ATT_TK = 1024
N_STATS = 4
SKIP_LOG2 = 150.0
MOE_BLOCK = 512
RANK_SPAN = 1 << 20
CODE_ROWS = 8
PLANE = HALF // 2
SC_WINDOW = 128


def _layer_norm(x, g, b):
    mu = jnp.mean(x, axis=-1, keepdims=True)
    xc = x - mu
    var = jnp.mean(xc * xc, axis=-1, keepdims=True)
    return xc * lax.rsqrt(var + LN_EPS) * g + b


def _split3(x):
    a = x.astype(BF16)
    r = x - a.astype(F32)
    b = r.astype(BF16)
    c = (r - b.astype(F32)).astype(BF16)
    return a, b, c


def _prefix_rows(x, tm, inclusive, small_ints=False):
    kp = max(tm, LANES)
    r = lax.broadcasted_iota(I32, (tm, kp), 0)
    c = lax.broadcasted_iota(I32, (tm, kp), 1)
    tri = ((c <= r) if inclusive else (c < r)).astype(BF16)
    if kp > tm:
        x = jnp.concatenate([x, jnp.zeros((kp - tm, x.shape[1]), x.dtype)], axis=0)
    out = None
    for piece in ((x.astype(BF16),) if small_ints else _split3(x)):
        y = jnp.dot(tri, piece, preferred_element_type=F32)
        out = y if out is None else out + y
    return out


def _pack_halves(y):
    hi = pltpu.bitcast(y[:, :HALF].astype(BF16).astype(F32), U32)
    lo = pltpu.bitcast(y[:, HALF:].astype(BF16).astype(F32), U32)
    return hi | (lo >> 16)


def _unpack_halves(w):
    hi = pltpu.bitcast(w & jnp.uint32(0xFFFF0000), F32)
    lo = pltpu.bitcast(w << 16, F32)
    return hi, lo


def _head_slab(p, off, h):
    s = p[:, off + (h // 2) * LANES: off + (h // 2) * LANES + LANES]
    return s if h % 2 == 0 else pltpu.roll(s, HEAD_DIM, 1)


def _augment(qs, ks, vs, f_col, lane):
    fc = jnp.broadcast_to(f_col, lane.shape)
    hi = fc.astype(BF16).astype(F32)
    r1 = fc - hi
    mid = r1.astype(BF16).astype(F32)
    lo = r1 - mid
    one = jnp.ones_like(fc)
    zero = jnp.zeros_like(fc)
    d = HEAD_DIM
    ka = jnp.where(lane < d, ks, jnp.where(lane < d + 3, one, jnp.where(
        lane == d + 3, -hi, jnp.where(lane == d + 4, -mid, jnp.where(lane == d + 5, -lo, zero)))))
    va = jnp.where(lane < d, vs, jnp.where(lane == d, one, zero))
    if qs is None:
        return None, ka.astype(BF16), va.astype(BF16)
    qa = jnp.where(lane < d, qs, jnp.where(lane == d, hi, jnp.where(
        lane == d + 1, mid, jnp.where(lane == d + 2, lo, jnp.where(lane < d + 6, one, zero)))))
    return qa.astype(BF16), ka.astype(BF16), va.astype(BF16)


PROJ_W = MAIN_W + LANES
PROJ_CHUNKS = ((0, POOL_W), (POOL_W, POOL_W + ATT_W), (POOL_W + ATT_W, POOL_W + 2 * ATT_W),
               (POOL_W + 2 * ATT_W, PROJ_W))


def _inproj_kernel(x_ref, xnext_ref, f0_ref, lng_ref, lnb_ref, w_ref, b_ref, *rest, tm):
    u_ref, k_ref, v_ref, logf_ref, qa_ref, ka_ref, va_ref, st_ref, carry_ref, proj_ref = rest[-10:]
    t = pl.program_id(1)
    lin = pl.program_id(0) * pl.num_programs(1) + t

    @pl.when(t == 0)
    def _():
        carry_ref[...] = f0_ref[...]

    def project(xn_bf, chunk):
        lo, hi = PROJ_CHUNKS[chunk]
        return jnp.dot(xn_bf, w_ref[:, lo:hi], preferred_element_type=F32) + b_ref[:, lo:hi]

    @pl.when(lin == 0)
    def _():
        xn = _layer_norm(x_ref[...], lng_ref[...], lnb_ref[...]).astype(BF16)
        for c, (lo, hi) in enumerate(PROJ_CHUNKS):
            proj_ref[:, lo:hi] = project(xn, c)

    p = proj_ref
    next_chunks = []
    xn_next = _layer_norm(xnext_ref[...], lng_ref[...], lnb_ref[...]).astype(BF16)
    fl = p[:, MAIN_W:]
    logf = jnp.minimum(fl, 0.0) - jnp.log(1.0 + jnp.exp(-jnp.abs(fl)))
    f_cum = _prefix_rows(logf, tm, True) + carry_ref[...]
    carry_ref[...] = f_cum[tm - 1:tm, :]

    u_ref[...] = p[:, :POOL_W]
    k_ref[...] = p[:, POOL_W + ATT_W:POOL_W + 2 * ATT_W]
    v_ref[...] = p[:, POOL_W + 2 * ATT_W:MAIN_W]
    logf_ref[...] = logf[:, :N_HEADS]

    lane = lax.broadcasted_iota(I32, (tm, AUG_W), 1)
    for h in range(N_HEADS):
        qs = _head_slab(p, POOL_W, h) * (ATT_SCALE * LOG2E)
        ks = _head_slab(p, POOL_W + ATT_W, h)
        vs = _head_slab(p, POOL_W + 2 * ATT_W, h)
        qa, ka, va = _augment(qs, ks, vs, f_cum[:, h:h + 1] * LOG2E, lane)
        qa_ref[h] = qa
        ka_ref[h] = ka
        va_ref[h] = va
        if h % 2 == 1:
            next_chunks.append(project(xn_next, h // 2))

    col = lax.broadcasted_iota(I32, (ATT_W, LANES), 0) // HEAD_DIM
    head_sum = (col == lax.broadcasted_iota(I32, (ATT_W, LANES), 1)).astype(BF16)
    qsec = (p[:, POOL_W:POOL_W + ATT_W] * (ATT_SCALE * LOG2E)).astype(BF16).astype(F32)
    ksec = p[:, POOL_W + ATT_W:POOL_W + 2 * ATT_W].astype(BF16).astype(F32)
    for r, sec in enumerate((qsec, ksec)):
        sq = jnp.dot((sec * sec).astype(BF16), head_sum, preferred_element_type=F32)
        st_ref[r:r + 1, :] = jnp.max(sq, axis=0, keepdims=True)
    st_ref[2:3, :] = f_cum[0:1, :] * LOG2E
    st_ref[3:4, :] = f_cum[tm - 1:tm, :] * LOG2E
    for (lo, hi), chunk in zip(PROJ_CHUNKS, next_chunks):
        proj_ref[:, lo:hi] = chunk


def _inproj(x, f0, lng, lnb, w, bias, kv_prev=(), kv_row_off=0):
    B, T, D = x.shape
    tm = min(SEQ_TILE, T)
    nt = T // tm
    grid = (B, nt)
    const = lambda b, t: (0, 0)
    row = lambda b, t: (b, t, 0)

    def next_row(b, t):
        lin = jnp.minimum(b * nt + t + 1, B * nt - 1)
        return (lin // nt, lin % nt, 0)

    aug = pl.BlockSpec((None, N_HEADS, tm, AUG_W), lambda b, t: (b, 0, t, 0))
    aug_shape = jax.ShapeDtypeStruct((B, N_HEADS, T, AUG_W), BF16)
    kv = pl.BlockSpec((None, N_HEADS, tm, AUG_W), lambda b, t: (b, 0, kv_row_off // tm + t, 0))
    kv_shape = jax.ShapeDtypeStruct(kv_prev[0].shape, BF16) if kv_prev else aug_shape
    n_in = 7
    return pl.pallas_call(
        functools.partial(_inproj_kernel, tm=tm),
        grid=grid,
        in_specs=[
            pl.BlockSpec((None, tm, D), row),
            pl.BlockSpec((None, tm, D), next_row),
            pl.BlockSpec((None, 1, LANES), lambda b, t: (b, 0, 0)),
            pl.BlockSpec((1, D), const), pl.BlockSpec((1, D), const),
            pl.BlockSpec((D, PROJ_W), const), pl.BlockSpec((1, PROJ_W), const),
        ] + [pl.BlockSpec(memory_space=pl.ANY) for _ in kv_prev],
        out_specs=[
            pl.BlockSpec((None, tm, POOL_W), row),
            pl.BlockSpec((None, tm, ATT_W), row),
            pl.BlockSpec((None, tm, ATT_W), row),
            pl.BlockSpec((None, tm, N_HEADS), row),
            aug, kv, kv,
            pl.BlockSpec((None, None, N_STATS, LANES), lambda b, t: (b, t, 0, 0)),
        ],
        out_shape=[
            jax.ShapeDtypeStruct((B, T, POOL_W), F32),
            jax.ShapeDtypeStruct((B, T, ATT_W), F32),
            jax.ShapeDtypeStruct((B, T, ATT_W), F32),
            jax.ShapeDtypeStruct((B, T, N_HEADS), F32),
            aug_shape, kv_shape, kv_shape,
            jax.ShapeDtypeStruct((B, T // tm, N_STATS, LANES), F32),
        ],
        scratch_shapes=[pltpu.VMEM((1, LANES), F32), pltpu.VMEM((tm, PROJ_W), F32)],
        input_output_aliases={n_in + i: 5 + i for i in range(len(kv_prev))},
        compiler_params=pltpu.CompilerParams(
            dimension_semantics=("arbitrary", "arbitrary"), vmem_limit_bytes=VMEM_LIMIT),
    )(x, x, f0, lng, lnb, w, bias, *kv_prev)


def _cache_kernel(ck_ref, cv_ref, clf_ref, ka_ref, va_ref, ftot_ref, carry_ref, *, tp):
    t = pl.program_id(1)

    @pl.when(t == 0)
    def _():
        carry_ref[...] = jnp.zeros_like(carry_ref)

    @pl.when(t < pl.num_programs(1) - 1)
    def _():
        f_cum = _prefix_rows(clf_ref[...], tp, True) + carry_ref[...]
        carry_ref[...] = f_cum[tp - 1:tp, :]
        ftot_ref[...] = f_cum[tp - 1:tp, :]
        ck = ck_ref[...].astype(F32)
        cv = cv_ref[...].astype(F32)
        lane = lax.broadcasted_iota(I32, (tp, AUG_W), 1)
        for h in range(N_HEADS):
            _, ka, va = _augment(None, _head_slab(ck, 0, h), _head_slab(cv, 0, h), f_cum[:, h:h + 1] * LOG2E, lane)
            ka_ref[h] = ka
            va_ref[h] = va

    @pl.when(t == pl.num_programs(1) - 1)
    def _():
        ka_ref[...] = jnp.zeros_like(ka_ref)
        va_ref[...] = jnp.zeros_like(va_ref)


def _cache_prep(ck, cv, clf, n_keys):
    B, P, _ = ck.shape
    tp = min(SEQ_TILE, P)
    nt = P // tp
    row = lambda b, t: (b, jnp.minimum(t, nt - 1), 0)
    aug = pl.BlockSpec((None, N_HEADS, tp, AUG_W), lambda b, t: (b, 0, t, 0))
    aug_shape = jax.ShapeDtypeStruct((B, N_HEADS, n_keys, AUG_W), BF16)
    assert P < n_keys <= P + tp
    return pl.pallas_call(
        functools.partial(_cache_kernel, tp=tp),
        grid=(B, nt + 1),
        in_specs=[pl.BlockSpec((None, tp, ATT_W), row), pl.BlockSpec((None, tp, ATT_W), row),
                  pl.BlockSpec((None, tp, LANES), row)],
        out_specs=[aug, aug, pl.BlockSpec((None, 1, LANES), lambda b, t: (b, 0, 0))],
        out_shape=[aug_shape, aug_shape, jax.ShapeDtypeStruct((B, 1, LANES), F32)],
        scratch_shapes=[pltpu.VMEM((1, LANES), F32)],
        compiler_params=pltpu.CompilerParams(
            dimension_semantics=("arbitrary", "arbitrary"), vmem_limit_bytes=VMEM_LIMIT),
    )(ck, cv, clf)


def _attn_kernel(jmin_ref, qa_ref, ka_ref, va_ref, o_ref, *, tq, tk, q_off):
    b, hp, iq = pl.program_id(0), pl.program_id(1), pl.program_id(2)
    q_lo = q_off + iq * tq
    n_full = (q_lo + 1) // tk
    j_first = jmin_ref[(b * pl.num_programs(1) + hp) * pl.num_programs(2) + iq]
    lane = lax.broadcasted_iota(I32, (tq, AUG_W), 1)
    qs = (qa_ref[0], qa_ref[1])

    def visit(carry, k0, nk, r0, masked):
        new = []
        for hh in range(2):
            m_all, acc_all = carry[hh]
            m, acc, q = m_all[r0:], acc_all[r0:], qs[hh][r0:]
            k = ka_ref[hh, pl.ds(k0, nk), :]
            v = va_ref[hh, pl.ds(k0, nk), :]
            s = lax.dot_general(q, k, (((1,), (1,)), ((), ())), preferred_element_type=F32)
            if masked:
                qpos = q_lo + r0 + lax.broadcasted_iota(I32, (tq - r0, nk), 0)
                kpos = k0 + lax.broadcasted_iota(I32, (tq - r0, nk), 1)
                s = jnp.where(qpos >= kpos, s, -jnp.inf)
            m_new = jnp.maximum(m, jnp.max(s, axis=-1, keepdims=True))
            alpha = jnp.exp2(m - m_new)
            p = jnp.exp2(s - m_new)
            acc = acc * alpha + jnp.dot(p.astype(BF16), v, preferred_element_type=F32)
            if r0:
                m_new = jnp.concatenate([m_all[:r0], m_new], axis=0)
                acc = jnp.concatenate([acc_all[:r0], acc], axis=0)
            new.append((m_new, acc))
        return tuple(new)

    def step(j, carry):
        return visit(carry, pl.multiple_of(j * tk, tk), tk, 0, False)

    init = (jnp.full((tq, 1), -jnp.inf, F32), jnp.zeros((tq, AUG_W), F32))
    carry = lax.fori_loop(j_first, n_full, step, (init, init))
    k_diag = pl.multiple_of(n_full * tk, tk)
    if tq == tk and tq % (2 * LANES) == 0:
        half = tq // 2
        carry = visit(carry, k_diag, half, 0, True)
        carry = visit(carry, pl.multiple_of(k_diag + half, half), half, half, True)
    else:
        carry = visit(carry, k_diag, tk, 0, True)
    outs = [acc / acc[:, HEAD_DIM:HEAD_DIM + 1] for _, acc in carry]
    o_ref[...] = jnp.where(lane < HEAD_DIM, outs[0], pltpu.roll(outs[1], HEAD_DIM, 1)).astype(BF16)


def _attention(jmin, qa, ka, va, q_off, tq, tk):
    B, H, Tq, _ = qa.shape
    Tk = ka.shape[2]
    assert tk % tq == 0 and q_off % tq == 0 and tq > 1
    return pl.pallas_call(
        functools.partial(_attn_kernel, tq=tq, tk=tk, q_off=q_off),
        grid_spec=pltpu.PrefetchScalarGridSpec(
            num_scalar_prefetch=1,
            grid=(B, H // 2, Tq // tq),
            in_specs=[
                pl.BlockSpec((None, 2, tq, AUG_W), lambda b, hp, iq, jm: (b, hp, iq, 0)),
                pl.BlockSpec((None, 2, Tk, AUG_W), lambda b, hp, iq, jm: (b, hp, 0, 0)),
                pl.BlockSpec((None, 2, Tk, AUG_W), lambda b, hp, iq, jm: (b, hp, 0, 0)),
            ],
            out_specs=pl.BlockSpec((None, tq, 2 * HEAD_DIM), lambda b, hp, iq, jm: (b, iq, hp)),
        ),
        out_shape=jax.ShapeDtypeStruct((B, Tq, ATT_W), BF16),
        compiler_params=pltpu.CompilerParams(
            dimension_semantics=("arbitrary", "arbitrary", "arbitrary"), vmem_limit_bytes=VMEM_LIMIT),
    )(jmin, qa, ka, va)


def _attn_skip_plan(stats, tq, tk, tm):
    B, nt = stats.shape[:2]
    st = stats[..., :N_HEADS]
    per = lambda row, r: st[:, :, row].reshape(B, nt // r, r, N_HEADS)
    rq, rk = tq // tm, tk // tm
    nq, nk = nt // rq, nt // rk
    qn2, kn2, kn2_own = per(0, rq).max(2), per(1, rk).max(2), per(1, rq).max(2)
    f_first, f_last = per(2, rq)[:, :, 0], per(3, rk)[:, :, -1]
    slack = 1.01
    upper = jnp.sqrt(qn2[:, :, None] * kn2[:, None, :]) * slack + (f_first[:, :, None] - f_last[:, None, :])
    lower = -jnp.sqrt(qn2 * kn2_own) * slack
    weightless = upper - lower[:, :, None] <= -(SKIP_LOG2 + 2.0)
    j = jnp.arange(nk, dtype=I32)[None, None, :, None]
    n_full = ((jnp.arange(nq, dtype=I32) * tq + 1) // tk)[None, :, None, None]
    first = jnp.min(jnp.where(weightless | (j >= n_full), n_full, j), axis=2)
    first = jnp.min(first.reshape(B, nq, N_HEADS // 2, 2), axis=3)
    return jnp.swapaxes(first, 1, 2).reshape(-1).astype(I32)


GATE_CHUNKS = 8


def _gate_chunk(xn_bf, wg_ref, bg_ref, c):
    cols = slice(c * (2 * D_MODEL // GATE_CHUNKS), (c + 1) * (2 * D_MODEL // GATE_CHUNKS))
    return jax.nn.sigmoid(jnp.dot(xn_bf, wg_ref[:, cols], preferred_element_type=F32) + bg_ref[:, cols]), cols


def _mixer_kernel(x_ref, xnext_ref, u_ref, uprev_ref, hist_ref, ya_ref, cnt0_ref,
                  lng_ref, lnb_ref, wg_ref, bg_ref, wpool_ref, spool_ref, wpp_ref, watt_ref, wout_ref,
                  ln1g_ref, ln1b_ref, wr_ref, br_ref, *rest, tm, start_pos):
    h_ref, hp_ref, tw_ref, code_ref, cnt_ref, uext_ref, gates_ref, xn_ref = rest[-8:]
    b = pl.program_id(0)
    t = pl.program_id(1)

    @pl.when((b == 0) & (t == 0))
    def _():
        cnt_ref[...] = cnt0_ref[...]
        xn0 = _layer_norm(x_ref[...], lng_ref[...], lnb_ref[...])
        xn_ref[...] = xn0
        for c in range(GATE_CHUNKS):
            g, cols = _gate_chunk(xn0.astype(BF16), wg_ref, bg_ref, c)
            gates_ref[:, cols] = g

    @pl.when(t == 0)
    def _():
        uext_ref[0:HALO, :] = hist_ref[...]

    @pl.when(t > 0)
    def _():
        uext_ref[0:HALO, :] = uprev_ref[...]

    uext_ref[HALO:HALO + tm, :] = u_ref[...]

    pos = start_pos + t * tm + lax.broadcasted_iota(I32, (tm, 1), 0)
    groups = []
    for gi, w in enumerate(POOL_WINDOWS):
        sl = slice(gi * POOL_GC, (gi + 1) * POOL_GC)
        cur = uext_ref[HALO:HALO + tm, sl]
        s = cur
        for j in range(1, w):
            s = s + uext_ref[HALO - j:HALO - j + tm, sl]
        count = jnp.minimum(pos + 1, w).astype(F32)
        d = s / count - cur
        yg = jnp.dot(d.astype(BF16), wpool_ref[gi], preferred_element_type=F32) * spool_ref[:, sl]
        groups.append(yg.astype(BF16))
    pooled = jnp.dot(jnp.concatenate(groups, axis=1), wpp_ref[...], preferred_element_type=F32)
    att = jnp.dot(ya_ref[...], watt_ref[...], preferred_element_type=F32)
    m = gates_ref[:, :D_MODEL] * pooled + gates_ref[:, D_MODEL:] * att
    mix = jnp.dot(m.astype(BF16), wout_ref[...], preferred_element_type=F32)

    xn_next_f32 = _layer_norm(xnext_ref[...], lng_ref[...], lnb_ref[...])
    xn_next = xn_next_f32.astype(BF16)

    def emit_next_gates(c):
        g, cols = _gate_chunk(xn_next, wg_ref, bg_ref, c)
        gates_ref[:, cols] = g

    emit_next_gates(0)

    h = _layer_norm(DEEPNORM_ALPHA * xn_ref[...] + mix, ln1g_ref[...], ln1b_ref[...])
    xn_ref[...] = xn_next_f32
    h_ref[...] = h
    packed = _pack_halves(h)
    hp_ref[0] = packed[:, :PLANE]
    hp_ref[1] = packed[:, PLANE:]
    emit_next_gates(1)
    emit_next_gates(2)

    hh = h.astype(BF16)
    hl = (h - hh.astype(F32)).astype(BF16)
    hi_pass = jnp.dot(hh, wr_ref[...], preferred_element_type=F32)
    lo_pass = jnp.dot(hl, wr_ref[:, :LANES], preferred_element_type=F32)
    logits = hi_pass[:, :LANES] + lo_pass + hi_pass[:, LANES:] + br_ref[...]
    emit_next_gates(3)
    lane = lax.broadcasted_iota(I32, (tm, LANES), 1)
    work = jnp.where(lane < N_EXPERTS, logits, -jnp.inf)
    vals, idxs = [], []
    for r in range(TOP_K):
        mx = jnp.max(work, axis=-1, keepdims=True)
        idx = jnp.min(jnp.where(work == mx, lane, LANES), axis=-1, keepdims=True)
        vals.append(mx)
        idxs.append(idx)
        work = jnp.where(lane == idx, -jnp.inf, work)
        emit_next_gates(GATE_CHUNKS - TOP_K + r)
    exps = [jnp.exp(v - vals[0]) for v in vals]
    denom = exps[0] + exps[1] + exps[2] + exps[3]

    onehot = jnp.zeros((tm, LANES), F32)
    for idx in idxs:
        onehot = onehot + (lane == idx).astype(F32)
    base = _prefix_rows(onehot, tm, False, small_ints=True) + cnt_ref[...]
    code = jnp.zeros((tm, LANES), I32)
    tw = jnp.zeros((tm, LANES), F32)
    for k in range(TOP_K):
        rank = jnp.sum(jnp.where(lane == idxs[k], base, 0.0), axis=-1, keepdims=True)
        code = jnp.where(lane == k, idxs[k] * RANK_SPAN + rank.astype(I32), code)
        tw = jnp.where(lane == k, exps[k] / denom, tw)
    tw_ref[...] = tw
    if code_ref.shape[0] == CODE_ROWS:
        code_ref[...] = jnp.transpose(code)[:CODE_ROWS, :]
    else:
        code_ref[...] = code
    cnt_ref[...] = cnt_ref[...] + jnp.sum(onehot, axis=0, keepdims=True)


def _mixer(x, u, hist, yatt, cnt0, start_pos, weights, n_tok, row_off, prev):
    B, T, D = x.shape
    tm = min(SEQ_TILE, T)
    nt = T // tm
    row = lambda b, t: (b, t, 0)
    tok = lambda b, t: (row_off // tm + b * nt + t, 0)
    hpt = tm // HALO
    full = lambda a: pl.BlockSpec(a.shape, lambda b, t, _n=a.ndim: (0,) * _n)
    tok3 = lambda b, t: (0, row_off // tm + b * nt + t, 0)
    assert n_tok <= RANK_SPAN
    if tm % LANES == 0:
        code_spec = pl.BlockSpec((CODE_ROWS, tm), lambda b, t: (0, b * nt + t))
        code_shape = jax.ShapeDtypeStruct((CODE_ROWS, B * T), I32)
    else:
        code_spec = pl.BlockSpec((tm, LANES), lambda b, t: (b * nt + t, 0))
        code_shape = jax.ShapeDtypeStruct((B * T, LANES), I32)
    out_specs = [pl.BlockSpec((tm, D), tok), pl.BlockSpec((2, tm, PLANE), tok3), pl.BlockSpec((tm, LANES), tok),
                 code_spec, pl.BlockSpec((1, LANES), lambda b, t: (0, 0))]
    out_shape = [jax.ShapeDtypeStruct((n_tok, D), F32), jax.ShapeDtypeStruct((2, n_tok, PLANE), U32),
                 jax.ShapeDtypeStruct((n_tok, LANES), F32), code_shape, jax.ShapeDtypeStruct((1, LANES), F32)]
    n_in = 7 + len(weights)

    def next_row(b, t):
        lin = jnp.minimum(b * nt + t + 1, B * nt - 1)
        return (lin // nt, lin % nt, 0)

    return pl.pallas_call(
        functools.partial(_mixer_kernel, tm=tm, start_pos=start_pos),
        grid=(B, nt),
        in_specs=[
            pl.BlockSpec((None, tm, D), row),
            pl.BlockSpec((None, tm, D), next_row),
            pl.BlockSpec((None, tm, POOL_W), row),
            pl.BlockSpec((None, HALO, POOL_W), lambda b, t: (b, jnp.maximum(t * hpt - 1, 0), 0)),
            pl.BlockSpec((None, HALO, POOL_W), lambda b, t: (b, 0, 0)),
            pl.BlockSpec((None, tm, ATT_W), row),
            pl.BlockSpec((1, LANES), lambda b, t: (0, 0)),
        ] + [full(w) for w in weights] + [pl.BlockSpec(memory_space=pl.ANY) for _ in prev],
        out_specs=out_specs,
        out_shape=out_shape,
        scratch_shapes=[pltpu.VMEM((HALO + tm, POOL_W), F32), pltpu.VMEM((tm, 2 * D), F32),
                        pltpu.VMEM((tm, D), F32)],
        input_output_aliases={n_in + i: i for i in range(len(prev))},
        compiler_params=pltpu.CompilerParams(
            dimension_semantics=("arbitrary", "arbitrary"), vmem_limit_bytes=VMEM_LIMIT),
    )(x, x, u, u, hist, yatt, cnt0, *weights, *prev)


def _sc_mesh():
    return plsc.VectorSubcoreMesh(core_axis_name="c", subcore_axis_name="s")


def _sc_scatter_rows(x, idx, n_rows):
    n = x.shape[0]

    @pl.kernel(out_type=jax.ShapeDtypeStruct((n_rows, PLANE), x.dtype), mesh=_sc_mesh(), scratch_types=[])
    def scatter(x_hbm, i_hbm, o_hbm):
        def body(x_vmem, i_vmem):
            for k in range(TOP_K):
                pltpu.sync_copy(x_vmem, o_hbm.at[i_vmem.at[k]])

        pltpu.emit_pipeline(
            body,
            grid=(n // SC_WINDOW,),
            in_specs=[pl.BlockSpec((SC_WINDOW, PLANE), index_map=lambda i: (i, 0)),
                      pl.BlockSpec((TOP_K, SC_WINDOW), index_map=lambda i: (0, i))],
            out_specs=[],
            core_axis_name=("c", "s"),
            dimension_semantics=(pltpu.PARALLEL,),
        )(x_hbm, i_hbm)

    return scatter(x, idx)


def _sc_gather_rows(y, idx):
    n = idx.shape[1]

    @pl.kernel(out_type=jax.ShapeDtypeStruct((n, PLANE), y.dtype), mesh=_sc_mesh(), scratch_types=[])
    def gather(y_hbm, i_hbm, o_hbm):
        def body(i_vmem, o_vmem):
            pltpu.sync_copy(y_hbm.at[i_vmem.at[0]], o_vmem)

        pltpu.emit_pipeline(
            body,
            grid=(n // SC_WINDOW,),
            in_specs=[pl.BlockSpec((1, SC_WINDOW), index_map=lambda i: (0, i))],
            out_specs=[pl.BlockSpec((SC_WINDOW, PLANE), index_map=lambda i: (i, 0))],
            core_axis_name=("c", "s"),
            dimension_semantics=(pltpu.PARALLEL,),
        )(i_hbm, o_hbm)

    return gather(y, idx)


def _join_planes(ref):
    return jnp.concatenate([ref[0], ref[1]], axis=1)


def _unpack_rows(xb_ref):
    xa, xc = _unpack_halves(_join_planes(xb_ref))
    return jnp.concatenate([xa.astype(BF16), xc.astype(BF16)], axis=1)


def _expert_kernel(be_ref, nb_ref, first_ref, xb_ref, xbnext_ref, w1_ref, b1_ref, w2_ref, b2_ref,
                   yb_ref, w1b_ref, w2b_ref, x_ref):
    del be_ref
    i = pl.program_id(0)

    @pl.when(i == 0)
    def _():
        x_ref[...] = _unpack_rows(xb_ref)

    @pl.when(first_ref[i] == 1)
    def _():
        w1b_ref[...] = w1_ref[...].astype(BF16)
        w2b_ref[...] = w2_ref[...].astype(BF16)

    @pl.when(i < nb_ref[0])
    def _():
        hfull = jnp.dot(x_ref[...], w1b_ref[...], preferred_element_type=F32) + b1_ref[...]
        glu = jnp.minimum(hfull[:, :D_FF], SWIGLU_LIMIT)
        lin = jnp.clip(hfull[:, D_FF:], -SWIGLU_LIMIT, SWIGLU_LIMIT)
        a = glu * jax.nn.sigmoid(SWIGLU_ALPHA * glu) * (lin + 1.0)
        x_next = _unpack_rows(xbnext_ref)
        y = jnp.dot(a.astype(BF16), w2b_ref[...], preferred_element_type=F32) + b2_ref[...]
        packed = _pack_halves(y)
        yb_ref[0] = packed[:, :PLANE]
        yb_ref[1] = packed[:, PLANE:]
        x_ref[...] = x_next

    @pl.when(i >= nb_ref[0])
    def _():
        yb_ref[...] = jnp.zeros_like(yb_ref)


def _experts(block_e, n_used, first, xb, w1, b1, w2, b2):
    n_rows = xb.shape[1]
    bm = MOE_BLOCK
    n_blocks = n_rows // bm
    rows = pl.BlockSpec((2, bm, PLANE), lambda i, be, nb, fi: (0, i, 0))
    next_rows = pl.BlockSpec((2, bm, PLANE), lambda i, be, nb, fi: (0, jnp.minimum(i + 1, n_blocks - 1), 0))
    per_expert = lambda r, c: pl.BlockSpec((None, r, c), lambda i, be, nb, fi: (be[i], 0, 0))
    return pl.pallas_call(
        _expert_kernel,
        grid_spec=pltpu.PrefetchScalarGridSpec(
            num_scalar_prefetch=3,
            grid=(n_blocks,),
            in_specs=[rows, next_rows, per_expert(D_MODEL, 2 * D_FF), per_expert(1, 2 * D_FF),
                      per_expert(D_FF, D_MODEL), per_expert(1, D_MODEL)],
            out_specs=rows,
            scratch_shapes=[pltpu.VMEM((D_MODEL, 2 * D_FF), BF16), pltpu.VMEM((D_FF, D_MODEL), BF16),
                            pltpu.VMEM((bm, D_MODEL), BF16)],
        ),
        out_shape=jax.ShapeDtypeStruct((2, n_rows, PLANE), U32),
        compiler_params=pltpu.CompilerParams(
            dimension_semantics=("arbitrary",), vmem_limit_bytes=VMEM_LIMIT),
    )(block_e, n_used, first, xb, xb, w1, b1, w2, b2)


def _combine_kernel(h_ref, tw_ref, g_ref, b_ref, y4_ref, out_ref):
    tw = tw_ref[...]
    acc_hi = None
    acc_lo = None
    for k in range(TOP_K):
        hi, lo = _unpack_halves(_join_planes(y4_ref.at[k]))
        w = tw[:, k:k + 1]
        acc_hi = w * hi if acc_hi is None else acc_hi + w * hi
        acc_lo = w * lo if acc_lo is None else acc_lo + w * lo
    moe = jnp.concatenate([acc_hi, acc_lo], axis=1)
    out_ref[...] = _layer_norm(DEEPNORM_ALPHA * h_ref[...] + moe, g_ref[...], b_ref[...])


def _combine(h, tw, ln2g, ln2b, y4, row_off, n_rows):
    D = h.shape[1]
    tn = min(COMBINE_TILE, n_rows)
    off = row_off // tn
    const = lambda i: (0, 0)
    return pl.pallas_call(
        _combine_kernel,
        grid=(n_rows // tn,),
        in_specs=[
            pl.BlockSpec((tn, D), lambda i: (off + i, 0)),
            pl.BlockSpec((tn, LANES), lambda i: (off + i, 0)),
            pl.BlockSpec((1, D), const), pl.BlockSpec((1, D), const),
            pl.BlockSpec((TOP_K, 2, tn, PLANE), lambda i: (0, 0, off + i, 0)),
        ],
        out_specs=pl.BlockSpec((tn, D), lambda i: (i, 0)),
        out_shape=jax.ShapeDtypeStruct((n_rows, D), F32),
        compiler_params=pltpu.CompilerParams(
            dimension_semantics=("arbitrary",), vmem_limit_bytes=VMEM_LIMIT),
    )(h, tw, ln2g, ln2b, y4)


def kernel(x_prompt, x_sample, cache_pool, cache_k, cache_v, cache_logf, ln_in_g, ln_in_b, w_in, b_in,
           w_pool, s_pool, w_pool_proj, w_att_proj, w_out, ln1_g, ln1_b, w_router, b_router,
           w1, b1, w2, b2, ln2_g, ln2_b):
    assert w_in.shape[0] == DEPTH
    B, T, D = x_prompt.shape
    Bs, Ts, _ = x_sample.shape
    P = cache_k.shape[2]
    row2 = lambda a: a.reshape(1, -1).astype(F32)

    f_off = MAIN_W
    g_off = MAIN_W + N_HEADS
    w_proj = jnp.pad(w_in[0][:, :g_off], ((0, 0), (0, LANES - N_HEADS))).astype(BF16)
    b_proj = row2(jnp.pad(b_in[0][:g_off], (0, LANES - N_HEADS)))
    wg = w_in[0][:, g_off:].astype(BF16)
    bg = row2(b_in[0][g_off:])
    lng, lnb = row2(ln_in_g), row2(ln_in_b)
    wr = jnp.pad(w_router[0], ((0, 0), (0, LANES - N_EXPERTS)))
    wr_hi = wr.astype(BF16)
    wr_lo = (wr - wr_hi.astype(F32)).astype(BF16)
    br = row2(jnp.pad(b_router[0], (0, LANES - N_EXPERTS)))
    mixer_weights = (lng, lnb, wg, bg, w_pool[0].astype(BF16), row2(s_pool[0]),
                     w_pool_proj[0].astype(BF16), w_att_proj[0].astype(BF16), w_out[0].astype(BF16),
                     row2(ln1_g[0]), row2(ln1_b[0]), jnp.concatenate([wr_hi, wr_lo], axis=1), br)

    zeros_f = jnp.zeros((B, 1, LANES), F32)
    u_p, k_p, v_p, logf_p, qa_p, ka_p, va_p, stats = _inproj(x_prompt, zeros_f, lng, lnb, w_proj, b_proj)
    tq, tk = min(ATT_TQ, T), min(ATT_TK, T)
    ya_p = _attention(_attn_skip_plan(stats, tq, tk, min(SEQ_TILE, T)), qa_p, ka_p, va_p, 0, tq, tk)

    clf = jnp.pad(cache_logf[0], ((0, 0), (0, 0), (0, LANES - N_HEADS)))
    tks = P + Ts + (-(P + Ts)) % LANES
    ka_c, va_c, f_tot = _cache_prep(cache_k[0].astype(BF16).reshape(Bs, P, ATT_W),
                                    cache_v[0].astype(BF16).reshape(Bs, P, ATT_W), clf, tks)
    u_s, k_s, v_s, logf_s, qa_s, ka_s, va_s, _ = _inproj(x_sample, f_tot, lng, lnb, w_proj, b_proj, (ka_c, va_c), P)
    visit_all = jnp.zeros((Bs * (N_HEADS // 2),), I32)
    ya_s = _attention(visit_all, qa_s, ka_s, va_s, P, Ts, tks)

    cnt0 = jnp.zeros((1, LANES), F32)
    hist_p = jnp.zeros((B, HALO, POOL_W), F32)
    n_tok = B * T + Bs * Ts
    *bufs, code_p, cnt_p = _mixer(x_prompt, u_p, hist_p, ya_p, cnt0, 0, mixer_weights, n_tok, 0, ())
    u_full_s = jnp.concatenate([cache_pool[0].astype(F32), u_s], axis=1)
    hist_s = jnp.pad(cache_pool[0].astype(F32), ((0, 0), (HALO - POOL_HIST, 0), (0, 0)))
    h_all, hp_all, tw, code_s, cnt = _mixer(x_sample, u_s, hist_s, ya_s, cnt_p, P, mixer_weights,
                                            n_tok, B * T, tuple(bufs))
    by_slot = lambda c: c[:TOP_K] if c.shape[0] == CODE_ROWS else c[:, :TOP_K].T
    code = jnp.concatenate([by_slot(code_p), by_slot(code_s)], axis=1)
    ti, rk = code // RANK_SPAN, code % RANK_SPAN

    counts = cnt[0, :N_EXPERTS].astype(I32)
    padded = (counts + MOE_BLOCK - 1) // MOE_BLOCK * MOE_BLOCK
    pad_ends = jnp.cumsum(padded)
    pad_starts = pad_ends - padded
    n_blocks = (n_tok * TOP_K + N_EXPERTS * (MOE_BLOCK - 1) + MOE_BLOCK - 1) // MOE_BLOCK
    n_rows = n_blocks * MOE_BLOCK
    block_start = jnp.arange(n_blocks, dtype=I32) * MOE_BLOCK
    block_e = jnp.minimum(jnp.sum((pad_ends[None, :] <= block_start[:, None]).astype(I32), axis=1), N_EXPERTS - 1)
    n_used = (pad_ends[-1:] // MOE_BLOCK).astype(I32)
    start_of = sum(jnp.where(ti == e, pad_starts[e], 0) for e in range(N_EXPERTS))
    dest_t = (start_of + rk).astype(I32)
    dest_planes = jnp.stack([dest_t, dest_t + n_rows], axis=1)

    xb = _sc_scatter_rows(hp_all.reshape(2 * n_tok, PLANE), dest_planes.reshape(TOP_K, 2 * n_tok), 2 * n_rows)
    first = jnp.concatenate([jnp.ones((1,), I32), (block_e[1:] != block_e[:-1]).astype(I32)])
    yb = _experts(block_e, n_used, first, xb.reshape(2, n_rows, PLANE),
                  w1[0], b1[0][:, None, :], w2[0], b2[0][:, None, :])
    y4 = _sc_gather_rows(yb.reshape(2 * n_rows, PLANE), dest_planes.reshape(1, TOP_K * 2 * n_tok))
    y4 = y4.reshape(TOP_K, 2, n_tok, PLANE)
    ln2g, ln2b = row2(ln2_g[0]), row2(ln2_b[0])
    y_prompt = _combine(h_all, tw, ln2g, ln2b, y4, 0, B * T).reshape(B, T, D)
    y_sample = _combine(h_all, tw, ln2g, ln2b, y4, B * T, Bs * Ts).reshape(Bs, Ts, D)
    heads = lambda a, b_, t_: a.reshape(1, b_, t_, N_HEADS, HEAD_DIM)
    return (y_prompt, y_sample,
            heads(k_p, B, T), heads(v_p, B, T), logf_p[None], u_p[:, -POOL_HIST:][None],
            heads(k_s, Bs, Ts), heads(v_s, Bs, Ts), logf_s[None], u_full_s[:, -POOL_HIST:][None])
```

```python
import functools

import jax
import jax.numpy as jnp
from jax import lax
from jax.experimental import pallas as pl
from jax.experimental.pallas import tpu as pltpu
from jax.experimental.pallas import tpu_sc as plsc

F32 = jnp.float32
BF16 = jnp.bfloat16
I32 = jnp.int32
U32 = jnp.uint32

D_MODEL = 1024
N_HEADS = 8
HEAD_DIM = 64
ATT_W = N_HEADS * HEAD_DIM
POOL_WINDOWS = (2, 4, 8, 16)
POOL_GC = 128
POOL_W = len(POOL_WINDOWS) * POOL_GC
POOL_HIST = max(POOL_WINDOWS) - 1
HALO = 16
N_EXPERTS = 32
TOP_K = 4
D_FF = D_MODEL
SWIGLU_ALPHA = 1.702
SWIGLU_LIMIT = 7.0
LN_EPS = 1e-5
DEPTH = 1
DEEPNORM_ALPHA = (2.0 * DEPTH) ** 0.25
ATT_SCALE = HEAD_DIM ** -0.5
LOG2E = 1.4426950408889634
MAIN_W = POOL_W + 3 * ATT_W
LANES = 128
AUG_W = LANES
HALF = D_MODEL // 2
VMEM_LIMIT = 56 * 1024 * 1024

SEQ_TILE = 512
COMBINE_TILE = 1024
ATT_TQ = 1024
ATT_TK = 1024
N_STATS = 4
SKIP_LOG2 = 150.0
MOE_BLOCK = 512
RANK_SPAN = 1 << 20
CODE_ROWS = 8
PLANE = HALF // 2
SC_WINDOW = 128


def _layer_norm(x, g, b):
    mu = jnp.mean(x, axis=-1, keepdims=True)
    xc = x - mu
    var = jnp.mean(xc * xc, axis=-1, keepdims=True)
    return xc * lax.rsqrt(var + LN_EPS) * g + b


def _split3(x):
    a = x.astype(BF16)
    r = x - a.astype(F32)
    b = r.astype(BF16)
    c = (r - b.astype(F32)).astype(BF16)
    return a, b, c


def _prefix_rows(x, tm, inclusive, small_ints=False):
    kp = max(tm, LANES)
    r = lax.broadcasted_iota(I32, (tm, kp), 0)
    c = lax.broadcasted_iota(I32, (tm, kp), 1)
    tri = ((c <= r) if inclusive else (c < r)).astype(BF16)
    if kp > tm:
        x = jnp.concatenate([x, jnp.zeros((kp - tm, x.shape[1]), x.dtype)], axis=0)
    out = None
    for piece in ((x.astype(BF16),) if small_ints else _split3(x)):
        y = jnp.dot(tri, piece, preferred_element_type=F32)
        out = y if out is None else out + y
    return out


def _pack_halves(y):
    hi = pltpu.bitcast(y[:, :HALF].astype(BF16).astype(F32), U32)
    lo = pltpu.bitcast(y[:, HALF:].astype(BF16).astype(F32), U32)
    return hi | (lo >> 16)


def _unpack_halves(w):
    hi = pltpu.bitcast(w & jnp.uint32(0xFFFF0000), F32)
    lo = pltpu.bitcast(w << 16, F32)
    return hi, lo


def _head_slab(p, off, h):
    s = p[:, off + (h // 2) * LANES: off + (h // 2) * LANES + LANES]
    return s if h % 2 == 0 else pltpu.roll(s, HEAD_DIM, 1)


def _augment(qs, ks, vs, f_col, lane):
    fc = jnp.broadcast_to(f_col, lane.shape)
    hi = fc.astype(BF16).astype(F32)
    r1 = fc - hi
    mid = r1.astype(BF16).astype(F32)
    lo = r1 - mid
    one = jnp.ones_like(fc)
    zero = jnp.zeros_like(fc)
    d = HEAD_DIM
    ka = jnp.where(lane < d, ks, jnp.where(lane < d + 3, one, jnp.where(
        lane == d + 3, -hi, jnp.where(lane == d + 4, -mid, jnp.where(lane == d + 5, -lo, zero)))))
    va = jnp.where(lane < d, vs, jnp.where(lane == d, one, zero))
    if qs is None:
        return None, ka.astype(BF16), va.astype(BF16)
    qa = jnp.where(lane < d, qs, jnp.where(lane == d, hi, jnp.where(
        lane == d + 1, mid, jnp.where(lane == d + 2, lo, jnp.where(lane < d + 6, one, zero)))))
    return qa.astype(BF16), ka.astype(BF16), va.astype(BF16)


PROJ_W = MAIN_W + LANES
PROJ_CHUNKS = ((0, POOL_W), (POOL_W, POOL_W + ATT_W), (POOL_W + ATT_W, POOL_W + 2 * ATT_W),
               (POOL_W + 2 * ATT_W, PROJ_W))


def _inproj_kernel(x_ref, xnext_ref, f0_ref, lng_ref, lnb_ref, w_ref, b_ref, *rest, tm):
    u_ref, k_ref, v_ref, logf_ref, qa_ref, ka_ref, va_ref, st_ref, carry_ref, proj_ref = rest[-10:]
    t = pl.program_id(1)
    lin = pl.program_id(0) * pl.num_programs(1) + t

    @pl.when(t == 0)
    def _():
        carry_ref[...] = f0_ref[...]

    def project(xn_bf, chunk):
        lo, hi = PROJ_CHUNKS[chunk]
        return jnp.dot(xn_bf, w_ref[:, lo:hi], preferred_element_type=F32) + b_ref[:, lo:hi]

    @pl.when(lin == 0)
    def _():
        xn = _layer_norm(x_ref[...], lng_ref[...], lnb_ref[...]).astype(BF16)
        for c, (lo, hi) in enumerate(PROJ_CHUNKS):
            proj_ref[:, lo:hi] = project(xn, c)

    p = proj_ref
    next_chunks = []
    xn_next = _layer_norm(xnext_ref[...], lng_ref[...], lnb_ref[...]).astype(BF16)
    fl = p[:, MAIN_W:]
    logf = jnp.minimum(fl, 0.0) - jnp.log(1.0 + jnp.exp(-jnp.abs(fl)))
    f_cum = _prefix_rows(logf, tm, True) + carry_ref[...]
    carry_ref[...] = f_cum[tm - 1:tm, :]

    u_ref[...] = p[:, :POOL_W]
    k_ref[...] = p[:, POOL_W + ATT_W:POOL_W + 2 * ATT_W]
    v_ref[...] = p[:, POOL_W + 2 * ATT_W:MAIN_W]
    logf_ref[...] = logf[:, :N_HEADS]

    lane = lax.broadcasted_iota(I32, (tm, AUG_W), 1)
    for h in range(N_HEADS):
        qs = _head_slab(p, POOL_W, h) * (ATT_SCALE * LOG2E)
        ks = _head_slab(p, POOL_W + ATT_W, h)
        vs = _head_slab(p, POOL_W + 2 * ATT_W, h)
        qa, ka, va = _augment(qs, ks, vs, f_cum[:, h:h + 1] * LOG2E, lane)
        qa_ref[h] = qa
        ka_ref[h] = ka
        va_ref[h] = va
        if h % 2 == 1:
            next_chunks.append(project(xn_next, h // 2))

    col = lax.broadcasted_iota(I32, (ATT_W, LANES), 0) // HEAD_DIM
    head_sum = (col == lax.broadcasted_iota(I32, (ATT_W, LANES), 1)).astype(BF16)
    qsec = (p[:, POOL_W:POOL_W + ATT_W] * (ATT_SCALE * LOG2E)).astype(BF16).astype(F32)
    ksec = p[:, POOL_W + ATT_W:POOL_W + 2 * ATT_W].astype(BF16).astype(F32)
    for r, sec in enumerate((qsec, ksec)):
        sq = jnp.dot((sec * sec).astype(BF16), head_sum, preferred_element_type=F32)
        st_ref[r:r + 1, :] = jnp.max(sq, axis=0, keepdims=True)
    st_ref[2:3, :] = f_cum[0:1, :] * LOG2E
    st_ref[3:4, :] = f_cum[tm - 1:tm, :] * LOG2E
    for (lo, hi), chunk in zip(PROJ_CHUNKS, next_chunks):
        proj_ref[:, lo:hi] = chunk


def _inproj(x, f0, lng, lnb, w, bias, kv_prev=(), kv_row_off=0):
    B, T, D = x.shape
    tm = min(SEQ_TILE, T)
    nt = T // tm
    grid = (B, nt)
    const = lambda b, t: (0, 0)
    row = lambda b, t: (b, t, 0)

    def next_row(b, t):
        lin = jnp.minimum(b * nt + t + 1, B * nt - 1)
        return (lin // nt, lin % nt, 0)

    aug = pl.BlockSpec((None, N_HEADS, tm, AUG_W), lambda b, t: (b, 0, t, 0))
    aug_shape = jax.ShapeDtypeStruct((B, N_HEADS, T, AUG_W), BF16)
    kv = pl.BlockSpec((None, N_HEADS, tm, AUG_W), lambda b, t: (b, 0, kv_row_off // tm + t, 0))
    kv_shape = jax.ShapeDtypeStruct(kv_prev[0].shape, BF16) if kv_prev else aug_shape
    n_in = 7
    return pl.pallas_call(
        functools.partial(_inproj_kernel, tm=tm),
        grid=grid,
        in_specs=[
            pl.BlockSpec((None, tm, D), row),
            pl.BlockSpec((None, tm, D), next_row),
            pl.BlockSpec((None, 1, LANES), lambda b, t: (b, 0, 0)),
            pl.BlockSpec((1, D), const), pl.BlockSpec((1, D), const),
            pl.BlockSpec((D, PROJ_W), const), pl.BlockSpec((1, PROJ_W), const),
        ] + [pl.BlockSpec(memory_space=pl.ANY) for _ in kv_prev],
        out_specs=[
            pl.BlockSpec((None, tm, POOL_W), row),
            pl.BlockSpec((None, tm, ATT_W), row),
            pl.BlockSpec((None, tm, ATT_W), row),
            pl.BlockSpec((None, tm, N_HEADS), row),
            aug, kv, kv,
            pl.BlockSpec((None, None, N_STATS, LANES), lambda b, t: (b, t, 0, 0)),
        ],
        out_shape=[
            jax.ShapeDtypeStruct((B, T, POOL_W), F32),
            jax.ShapeDtypeStruct((B, T, ATT_W), F32),
            jax.ShapeDtypeStruct((B, T, ATT_W), F32),
            jax.ShapeDtypeStruct((B, T, N_HEADS), F32),
            aug_shape, kv_shape, kv_shape,
            jax.ShapeDtypeStruct((B, T // tm, N_STATS, LANES), F32),
        ],
        scratch_shapes=[pltpu.VMEM((1, LANES), F32), pltpu.VMEM((tm, PROJ_W), F32)],
        input_output_aliases={n_in + i: 5 + i for i in range(len(kv_prev))},
        compiler_params=pltpu.CompilerParams(
            dimension_semantics=("arbitrary", "arbitrary"), vmem_limit_bytes=VMEM_LIMIT),
    )(x, x, f0, lng, lnb, w, bias, *kv_prev)


def _cache_kernel(ck_ref, cv_ref, clf_ref, ka_ref, va_ref, ftot_ref, carry_ref, *, tp):
    t = pl.program_id(1)

    @pl.when(t == 0)
    def _():
        carry_ref[...] = jnp.zeros_like(carry_ref)

    @pl.when(t < pl.num_programs(1) - 1)
    def _():
        f_cum = _prefix_rows(clf_ref[...], tp, True) + carry_ref[...]
        carry_ref[...] = f_cum[tp - 1:tp, :]
        ftot_ref[...] = f_cum[tp - 1:tp, :]
        ck = ck_ref[...].astype(F32)
        cv = cv_ref[...].astype(F32)
        lane = lax.broadcasted_iota(I32, (tp, AUG_W), 1)
        for h in range(N_HEADS):
            _, ka, va = _augment(None, _head_slab(ck, 0, h), _head_slab(cv, 0, h), f_cum[:, h:h + 1] * LOG2E, lane)
            ka_ref[h] = ka
            va_ref[h] = va

    @pl.when(t == pl.num_programs(1) - 1)
    def _():
        ka_ref[...] = jnp.zeros_like(ka_ref)
        va_ref[...] = jnp.zeros_like(va_ref)


def _cache_prep(ck, cv, clf, n_keys):
    B, P, _ = ck.shape
    tp = min(SEQ_TILE, P)
    nt = P // tp
    row = lambda b, t: (b, jnp.minimum(t, nt - 1), 0)
    aug = pl.BlockSpec((None, N_HEADS, tp, AUG_W), lambda b, t: (b, 0, t, 0))
    aug_shape = jax.ShapeDtypeStruct((B, N_HEADS, n_keys, AUG_W), BF16)
    assert P < n_keys <= P + tp
    return pl.pallas_call(
        functools.partial(_cache_kernel, tp=tp),
        grid=(B, nt + 1),
        in_specs=[pl.BlockSpec((None, tp, ATT_W), row), pl.BlockSpec((None, tp, ATT_W), row),
                  pl.BlockSpec((None, tp, LANES), row)],
        out_specs=[aug, aug, pl.BlockSpec((None, 1, LANES), lambda b, t: (b, 0, 0))],
        out_shape=[aug_shape, aug_shape, jax.ShapeDtypeStruct((B, 1, LANES), F32)],
        scratch_shapes=[pltpu.VMEM((1, LANES), F32)],
        compiler_params=pltpu.CompilerParams(
            dimension_semantics=("arbitrary", "arbitrary"), vmem_limit_bytes=VMEM_LIMIT),
    )(ck, cv, clf)


def _attn_kernel(jmin_ref, qa_ref, ka_ref, va_ref, o_ref, *, tq, tk, q_off):
    b, hp, iq = pl.program_id(0), pl.program_id(1), pl.program_id(2)
    q_lo = q_off + iq * tq
    n_full = (q_lo + 1) // tk
    j_first = jmin_ref[(b * pl.num_programs(1) + hp) * pl.num_programs(2) + iq]
    lane = lax.broadcasted_iota(I32, (tq, AUG_W), 1)
    qs = (qa_ref[0], qa_ref[1])

    def visit(carry, k0, nk, r0, masked):
        new = []
        for hh in range(2):
            m_all, acc_all = carry[hh]
            m, acc, q = m_all[r0:], acc_all[r0:], qs[hh][r0:]
            k = ka_ref[hh, pl.ds(k0, nk), :]
            v = va_ref[hh, pl.ds(k0, nk), :]
            s = lax.dot_general(q, k, (((1,), (1,)), ((), ())), preferred_element_type=F32)
            if masked:
                qpos = q_lo + r0 + lax.broadcasted_iota(I32, (tq - r0, nk), 0)
                kpos = k0 + lax.broadcasted_iota(I32, (tq - r0, nk), 1)
                s = jnp.where(qpos >= kpos, s, -jnp.inf)
            m_new = jnp.maximum(m, jnp.max(s, axis=-1, keepdims=True))
            alpha = jnp.exp2(m - m_new)
            p = jnp.exp2(s - m_new)
            acc = acc * alpha + jnp.dot(p.astype(BF16), v, preferred_element_type=F32)
            if r0:
                m_new = jnp.concatenate([m_all[:r0], m_new], axis=0)
                acc = jnp.concatenate([acc_all[:r0], acc], axis=0)
            new.append((m_new, acc))
        return tuple(new)

    def step(j, carry):
        return visit(carry, pl.multiple_of(j * tk, tk), tk, 0, False)

    init = (jnp.full((tq, 1), -jnp.inf, F32), jnp.zeros((tq, AUG_W), F32))
    carry = lax.fori_loop(j_first, n_full, step, (init, init))
    k_diag = pl.multiple_of(n_full * tk, tk)
    if tq == tk and tq % (2 * LANES) == 0:
        half = tq // 2
        carry = visit(carry, k_diag, half, 0, True)
        carry = visit(carry, pl.multiple_of(k_diag + half, half), half, half, True)
    else:
        carry = visit(carry, k_diag, tk, 0, True)
    outs = [acc / acc[:, HEAD_DIM:HEAD_DIM + 1] for _, acc in carry]
    o_ref[...] = jnp.where(lane < HEAD_DIM, outs[0], pltpu.roll(outs[1], HEAD_DIM, 1)).astype(BF16)


def _attention(jmin, qa, ka, va, q_off, tq, tk):
    B, H, Tq, _ = qa.shape
    Tk = ka.shape[2]
    assert tk % tq == 0 and q_off % tq == 0 and tq > 1
    return pl.pallas_call(
        functools.partial(_attn_kernel, tq=tq, tk=tk, q_off=q_off),
        grid_spec=pltpu.PrefetchScalarGridSpec(
            num_scalar_prefetch=1,
            grid=(B, H // 2, Tq // tq),
            in_specs=[
                pl.BlockSpec((None, 2, tq, AUG_W), lambda b, hp, iq, jm: (b, hp, iq, 0)),
                pl.BlockSpec((None, 2, Tk, AUG_W), lambda b, hp, iq, jm: (b, hp, 0, 0)),
                pl.BlockSpec((None, 2, Tk, AUG_W), lambda b, hp, iq, jm: (b, hp, 0, 0)),
            ],
            out_specs=pl.BlockSpec((None, tq, 2 * HEAD_DIM), lambda b, hp, iq, jm: (b, iq, hp)),
        ),
        out_shape=jax.ShapeDtypeStruct((B, Tq, ATT_W), BF16),
        compiler_params=pltpu.CompilerParams(
            dimension_semantics=("arbitrary", "arbitrary", "arbitrary"), vmem_limit_bytes=VMEM_LIMIT),
    )(jmin, qa, ka, va)


def _attn_skip_plan(stats, tq, tk, tm):
    B, nt = stats.shape[:2]
    st = stats[..., :N_HEADS]
    per = lambda row, r: st[:, :, row].reshape(B, nt // r, r, N_HEADS)
    rq, rk = tq // tm, tk // tm
    nq, nk = nt // rq, nt // rk
    qn2, kn2, kn2_own = per(0, rq).max(2), per(1, rk).max(2), per(1, rq).max(2)
    f_first, f_last = per(2, rq)[:, :, 0], per(3, rk)[:, :, -1]
    slack = 1.01
    upper = jnp.sqrt(qn2[:, :, None] * kn2[:, None, :]) * slack + (f_first[:, :, None] - f_last[:, None, :])
    lower = -jnp.sqrt(qn2 * kn2_own) * slack
    weightless = upper - lower[:, :, None] <= -(SKIP_LOG2 + 2.0)
    j = jnp.arange(nk, dtype=I32)[None, None, :, None]
    n_full = ((jnp.arange(nq, dtype=I32) * tq + 1) // tk)[None, :, None, None]
    first = jnp.min(jnp.where(weightless | (j >= n_full), n_full, j), axis=2)
    first = jnp.min(first.reshape(B, nq, N_HEADS // 2, 2), axis=3)
    return jnp.swapaxes(first, 1, 2).reshape(-1).astype(I32)


GATE_CHUNKS = 8


def _gate_chunk(xn_bf, wg_ref, bg_ref, c):
    cols = slice(c * (2 * D_MODEL // GATE_CHUNKS), (c + 1) * (2 * D_MODEL // GATE_CHUNKS))
    return jax.nn.sigmoid(jnp.dot(xn_bf, wg_ref[:, cols], preferred_element_type=F32) + bg_ref[:, cols]), cols


def _mixer_kernel(x_ref, xnext_ref, u_ref, uprev_ref, hist_ref, ya_ref, cnt0_ref,
                  lng_ref, lnb_ref, wg_ref, bg_ref, wpool_ref, spool_ref, wpp_ref, watt_ref, wout_ref,
                  ln1g_ref, ln1b_ref, wr_ref, br_ref, *rest, tm, start_pos):
    h_ref, hp_ref, tw_ref, code_ref, cnt_ref, uext_ref, gates_ref, xn_ref = rest[-8:]
    b = pl.program_id(0)
    t = pl.program_id(1)

    @pl.when((b == 0) & (t == 0))
    def _():
        cnt_ref[...] = cnt0_ref[...]
        xn0 = _layer_norm(x_ref[...], lng_ref[...], lnb_ref[...])
        xn_ref[...] = xn0
        for c in range(GATE_CHUNKS):
            g, cols = _gate_chunk(xn0.astype(BF16), wg_ref, bg_ref, c)
            gates_ref[:, cols] = g

    @pl.when(t == 0)
    def _():
        uext_ref[0:HALO, :] = hist_ref[...]

    @pl.when(t > 0)
    def _():
        uext_ref[0:HALO, :] = uprev_ref[...]

    uext_ref[HALO:HALO + tm, :] = u_ref[...]

    pos = start_pos + t * tm + lax.broadcasted_iota(I32, (tm, 1), 0)
    groups = []
    for gi, w in enumerate(POOL_WINDOWS):
        sl = slice(gi * POOL_GC, (gi + 1) * POOL_GC)
        cur = uext_ref[HALO:HALO + tm, sl]
        s = cur
        for j in range(1, w):
            s = s + uext_ref[HALO - j:HALO - j + tm, sl]
        count = jnp.minimum(pos + 1, w).astype(F32)
        d = s / count - cur
        yg = jnp.dot(d.astype(BF16), wpool_ref[gi], preferred_element_type=F32) * spool_ref[:, sl]
        groups.append(yg.astype(BF16))
    pooled = jnp.dot(jnp.concatenate(groups, axis=1), wpp_ref[...], preferred_element_type=F32)
    att = jnp.dot(ya_ref[...], watt_ref[...], preferred_element_type=F32)
    m = gates_ref[:, :D_MODEL] * pooled + gates_ref[:, D_MODEL:] * att
    mix = jnp.dot(m.astype(BF16), wout_ref[...], preferred_element_type=F32)

    xn_next_f32 = _layer_norm(xnext_ref[...], lng_ref[...], lnb_ref[...])
    xn_next = xn_next_f32.astype(BF16)

    def emit_next_gates(c):
        g, cols = _gate_chunk(xn_next, wg_ref, bg_ref, c)
        gates_ref[:, cols] = g

    emit_next_gates(0)

    h = _layer_norm(DEEPNORM_ALPHA * xn_ref[...] + mix, ln1g_ref[...], ln1b_ref[...])
    xn_ref[...] = xn_next_f32
    h_ref[...] = h
    packed = _pack_halves(h)
    hp_ref[0] = packed[:, :PLANE]
    hp_ref[1] = packed[:, PLANE:]
    emit_next_gates(1)
    emit_next_gates(2)

    hh = h.astype(BF16)
    hl = (h - hh.astype(F32)).astype(BF16)
    hi_pass = jnp.dot(hh, wr_ref[...], preferred_element_type=F32)
    lo_pass = jnp.dot(hl, wr_ref[:, :LANES], preferred_element_type=F32)
    logits = hi_pass[:, :LANES] + lo_pass + hi_pass[:, LANES:] + br_ref[...]
    emit_next_gates(3)
    lane = lax.broadcasted_iota(I32, (tm, LANES), 1)
    work = jnp.where(lane < N_EXPERTS, logits, -jnp.inf)
    vals, idxs = [], []
    for r in range(TOP_K):
        mx = jnp.max(work, axis=-1, keepdims=True)
        idx = jnp.min(jnp.where(work == mx, lane, LANES), axis=-1, keepdims=True)
        vals.append(mx)
        idxs.append(idx)
        work = jnp.where(lane == idx, -jnp.inf, work)
        emit_next_gates(GATE_CHUNKS - TOP_K + r)
    exps = [jnp.exp(v - vals[0]) for v in vals]
    denom = exps[0] + exps[1] + exps[2] + exps[3]

    onehot = jnp.zeros((tm, LANES), F32)
    for idx in idxs:
        onehot = onehot + (lane == idx).astype(F32)
    base = _prefix_rows(onehot, tm, False, small_ints=True) + cnt_ref[...]
    code = jnp.zeros((tm, LANES), I32)
    tw = jnp.zeros((tm, LANES), F32)
    for k in range(TOP_K):
        rank = jnp.sum(jnp.where(lane == idxs[k], base, 0.0), axis=-1, keepdims=True)
        code = jnp.where(lane == k, idxs[k] * RANK_SPAN + rank.astype(I32), code)
        tw = jnp.where(lane == k, exps[k] / denom, tw)
    tw_ref[...] = tw
    if code_ref.shape[0] == CODE_ROWS:
        code_ref[...] = jnp.transpose(code)[:CODE_ROWS, :]
    else:
        code_ref[...] = code
    cnt_ref[...] = cnt_ref[...] + jnp.sum(onehot, axis=0, keepdims=True)


def _mixer(x, u, hist, yatt, cnt0, start_pos, weights, n_tok, row_off, prev):
    B, T, D = x.shape
    tm = min(SEQ_TILE, T)
    nt = T // tm
    row = lambda b, t: (b, t, 0)
    tok = lambda b, t: (row_off // tm + b * nt + t, 0)
    hpt = tm // HALO
    full = lambda a: pl.BlockSpec(a.shape, lambda b, t, _n=a.ndim: (0,) * _n)
    tok3 = lambda b, t: (0, row_off // tm + b * nt + t, 0)
    assert n_tok <= RANK_SPAN
    if tm % LANES == 0:
        code_spec = pl.BlockSpec((CODE_ROWS, tm), lambda b, t: (0, b * nt + t))
        code_shape = jax.ShapeDtypeStruct((CODE_ROWS, B * T), I32)
    else:
        code_spec = pl.BlockSpec((tm, LANES), lambda b, t: (b * nt + t, 0))
        code_shape = jax.ShapeDtypeStruct((B * T, LANES), I32)
    out_specs = [pl.BlockSpec((tm, D), tok), pl.BlockSpec((2, tm, PLANE), tok3), pl.BlockSpec((tm, LANES), tok),
                 code_spec, pl.BlockSpec((1, LANES), lambda b, t: (0, 0))]
    out_shape = [jax.ShapeDtypeStruct((n_tok, D), F32), jax.ShapeDtypeStruct((2, n_tok, PLANE), U32),
                 jax.ShapeDtypeStruct((n_tok, LANES), F32), code_shape, jax.ShapeDtypeStruct((1, LANES), F32)]
    n_in = 7 + len(weights)

    def next_row(b, t):
        lin = jnp.minimum(b * nt + t + 1, B * nt - 1)
        return (lin // nt, lin % nt, 0)

    return pl.pallas_call(
        functools.partial(_mixer_kernel, tm=tm, start_pos=start_pos),
        grid=(B, nt),
        in_specs=[
            pl.BlockSpec((None, tm, D), row),
            pl.BlockSpec((None, tm, D), next_row),
            pl.BlockSpec((None, tm, POOL_W), row),
            pl.BlockSpec((None, HALO, POOL_W), lambda b, t: (b, jnp.maximum(t * hpt - 1, 0), 0)),
            pl.BlockSpec((None, HALO, POOL_W), lambda b, t: (b, 0, 0)),
            pl.BlockSpec((None, tm, ATT_W), row),
            pl.BlockSpec((1, LANES), lambda b, t: (0, 0)),
        ] + [full(w) for w in weights] + [pl.BlockSpec(memory_space=pl.ANY) for _ in prev],
        out_specs=out_specs,
        out_shape=out_shape,
        scratch_shapes=[pltpu.VMEM((HALO + tm, POOL_W), F32), pltpu.VMEM((tm, 2 * D), F32),
                        pltpu.VMEM((tm, D), F32)],
        input_output_aliases={n_in + i: i for i in range(len(prev))},
        compiler_params=pltpu.CompilerParams(
            dimension_semantics=("arbitrary", "arbitrary"), vmem_limit_bytes=VMEM_LIMIT),
    )(x, x, u, u, hist, yatt, cnt0, *weights, *prev)


def _sc_mesh():
    return plsc.VectorSubcoreMesh(core_axis_name="c", subcore_axis_name="s")


def _sc_scatter_rows(x, idx, n_rows):
    n = x.shape[0]

    @pl.kernel(out_type=jax.ShapeDtypeStruct((n_rows, PLANE), x.dtype), mesh=_sc_mesh(), scratch_types=[])
    def scatter(x_hbm, i_hbm, o_hbm):
        def body(x_vmem, i_vmem):
            for k in range(TOP_K):
                pltpu.sync_copy(x_vmem, o_hbm.at[i_vmem.at[k]])

        pltpu.emit_pipeline(
            body,
            grid=(n // SC_WINDOW,),
            in_specs=[pl.BlockSpec((SC_WINDOW, PLANE), index_map=lambda i: (i, 0)),
                      pl.BlockSpec((TOP_K, SC_WINDOW), index_map=lambda i: (0, i))],
            out_specs=[],
            core_axis_name=("c", "s"),
            dimension_semantics=(pltpu.PARALLEL,),
        )(x_hbm, i_hbm)

    return scatter(x, idx)


def _sc_gather_rows(y, idx):
    n = idx.shape[1]

    @pl.kernel(out_type=jax.ShapeDtypeStruct((n, PLANE), y.dtype), mesh=_sc_mesh(), scratch_types=[])
    def gather(y_hbm, i_hbm, o_hbm):
        def body(i_vmem, o_vmem):
            pltpu.sync_copy(y_hbm.at[i_vmem.at[0]], o_vmem)

        pltpu.emit_pipeline(
            body,
            grid=(n // SC_WINDOW,),
            in_specs=[pl.BlockSpec((1, SC_WINDOW), index_map=lambda i: (0, i))],
            out_specs=[pl.BlockSpec((SC_WINDOW, PLANE), index_map=lambda i: (i, 0))],
            core_axis_name=("c", "s"),
            dimension_semantics=(pltpu.PARALLEL,),
        )(i_hbm, o_hbm)

    return gather(y, idx)


def _join_planes(ref):
    return jnp.concatenate([ref[0], ref[1]], axis=1)


def _unpack_rows(xb_ref):
    xa, xc = _unpack_halves(_join_planes(xb_ref))
    return jnp.concatenate([xa.astype(BF16), xc.astype(BF16)], axis=1)


def _expert_kernel(be_ref, nb_ref, first_ref, xb_ref, xbnext_ref, w1_ref, b1_ref, w2_ref, b2_ref,
                   yb_ref, w1b_ref, w2b_ref, x_ref):
    del be_ref
    i = pl.program_id(0)

    @pl.when(i == 0)
    def _():
        x_ref[...] = _unpack_rows(xb_ref)

    @pl.when(first_ref[i] == 1)
    def _():
        w1b_ref[...] = w1_ref[...].astype(BF16)
        w2b_ref[...] = w2_ref[...].astype(BF16)

    @pl.when(i < nb_ref[0])
    def _():
        hfull = jnp.dot(x_ref[...], w1b_ref[...], preferred_element_type=F32) + b1_ref[...]
        glu = jnp.minimum(hfull[:, :D_FF], SWIGLU_LIMIT)
        lin = jnp.clip(hfull[:, D_FF:], -SWIGLU_LIMIT, SWIGLU_LIMIT)
        a = glu * jax.nn.sigmoid(SWIGLU_ALPHA * glu) * (lin + 1.0)
        x_next = _unpack_rows(xbnext_ref)
        a_bf = a.astype(BF16)
        for pln in range(2):
            c = pln * PLANE
            hi = jnp.dot(a_bf, w2b_ref[:, c:c + PLANE], preferred_element_type=F32) + b2_ref[:, c:c + PLANE]
            lo = (jnp.dot(a_bf, w2b_ref[:, HALF + c:HALF + c + PLANE], preferred_element_type=F32)
                  + b2_ref[:, HALF + c:HALF + c + PLANE])
            yb_ref[pln] = (pltpu.bitcast(hi.astype(BF16).astype(F32), U32)
                           | (pltpu.bitcast(lo.astype(BF16).astype(F32), U32) >> 16))
        x_ref[...] = x_next

    @pl.when(i >= nb_ref[0])
    def _():
        yb_ref[...] = jnp.zeros_like(yb_ref)


def _experts(block_e, n_used, first, xb, w1, b1, w2, b2):
    n_rows = xb.shape[1]
    bm = MOE_BLOCK
    n_blocks = n_rows // bm
    rows = pl.BlockSpec((2, bm, PLANE), lambda i, be, nb, fi: (0, i, 0))
    next_rows = pl.BlockSpec((2, bm, PLANE), lambda i, be, nb, fi: (0, jnp.minimum(i + 1, n_blocks - 1), 0))
    per_expert = lambda r, c: pl.BlockSpec((None, r, c), lambda i, be, nb, fi: (be[i], 0, 0))
    return pl.pallas_call(
        _expert_kernel,
        grid_spec=pltpu.PrefetchScalarGridSpec(
            num_scalar_prefetch=3,
            grid=(n_blocks,),
            in_specs=[rows, next_rows, per_expert(D_MODEL, 2 * D_FF), per_expert(1, 2 * D_FF),
                      per_expert(D_FF, D_MODEL), per_expert(1, D_MODEL)],
            out_specs=rows,
            scratch_shapes=[pltpu.VMEM((D_MODEL, 2 * D_FF), BF16), pltpu.VMEM((D_FF, D_MODEL), BF16),
                            pltpu.VMEM((bm, D_MODEL), BF16)],
        ),
        out_shape=jax.ShapeDtypeStruct((2, n_rows, PLANE), U32),
        compiler_params=pltpu.CompilerParams(
            dimension_semantics=("arbitrary",), vmem_limit_bytes=VMEM_LIMIT),
    )(block_e, n_used, first, xb, xb, w1, b1, w2, b2)


def _combine_kernel(h_ref, tw_ref, g_ref, b_ref, y4_ref, out_ref):
    tw = tw_ref[...]
    acc_hi = None
    acc_lo = None
    for k in range(TOP_K):
        hi, lo = _unpack_halves(_join_planes(y4_ref.at[k]))
        w = tw[:, k:k + 1]
        acc_hi = w * hi if acc_hi is None else acc_hi + w * hi
        acc_lo = w * lo if acc_lo is None else acc_lo + w * lo
    moe = jnp.concatenate([acc_hi, acc_lo], axis=1)
    out_ref[...] = _layer_norm(DEEPNORM_ALPHA * h_ref[...] + moe, g_ref[...], b_ref[...])


def _combine(h, tw, ln2g, ln2b, y4, row_off, n_rows):
    D = h.shape[1]
    tn = min(COMBINE_TILE, n_rows)
    off = row_off // tn
    const = lambda i: (0, 0)
    return pl.pallas_call(
        _combine_kernel,
        grid=(n_rows // tn,),
        in_specs=[
            pl.BlockSpec((tn, D), lambda i: (off + i, 0)),
            pl.BlockSpec((tn, LANES), lambda i: (off + i, 0)),
            pl.BlockSpec((1, D), const), pl.BlockSpec((1, D), const),
            pl.BlockSpec((TOP_K, 2, tn, PLANE), lambda i: (0, 0, off + i, 0)),
        ],
        out_specs=pl.BlockSpec((tn, D), lambda i: (i, 0)),
        out_shape=jax.ShapeDtypeStruct((n_rows, D), F32),
        compiler_params=pltpu.CompilerParams(
            dimension_semantics=("arbitrary",), vmem_limit_bytes=VMEM_LIMIT),
    )(h, tw, ln2g, ln2b, y4)


def kernel(x_prompt, x_sample, cache_pool, cache_k, cache_v, cache_logf, ln_in_g, ln_in_b, w_in, b_in,
           w_pool, s_pool, w_pool_proj, w_att_proj, w_out, ln1_g, ln1_b, w_router, b_router,
           w1, b1, w2, b2, ln2_g, ln2_b):
    assert w_in.shape[0] == DEPTH
    B, T, D = x_prompt.shape
    Bs, Ts, _ = x_sample.shape
    P = cache_k.shape[2]
    row2 = lambda a: a.reshape(1, -1).astype(F32)

    f_off = MAIN_W
    g_off = MAIN_W + N_HEADS
    w_proj = jnp.pad(w_in[0][:, :g_off], ((0, 0), (0, LANES - N_HEADS))).astype(BF16)
    b_proj = row2(jnp.pad(b_in[0][:g_off], (0, LANES - N_HEADS)))
    wg = w_in[0][:, g_off:].astype(BF16)
    bg = row2(b_in[0][g_off:])
    lng, lnb = row2(ln_in_g), row2(ln_in_b)
    wr = jnp.pad(w_router[0], ((0, 0), (0, LANES - N_EXPERTS)))
    wr_hi = wr.astype(BF16)
    wr_lo = (wr - wr_hi.astype(F32)).astype(BF16)
    br = row2(jnp.pad(b_router[0], (0, LANES - N_EXPERTS)))
    mixer_weights = (lng, lnb, wg, bg, w_pool[0].astype(BF16), row2(s_pool[0]),
                     w_pool_proj[0].astype(BF16), w_att_proj[0].astype(BF16), w_out[0].astype(BF16),
                     row2(ln1_g[0]), row2(ln1_b[0]), jnp.concatenate([wr_hi, wr_lo], axis=1), br)

    zeros_f = jnp.zeros((B, 1, LANES), F32)
    u_p, k_p, v_p, logf_p, qa_p, ka_p, va_p, stats = _inproj(x_prompt, zeros_f, lng, lnb, w_proj, b_proj)
    tq, tk = min(ATT_TQ, T), min(ATT_TK, T)
    ya_p = _attention(_attn_skip_plan(stats, tq, tk, min(SEQ_TILE, T)), qa_p, ka_p, va_p, 0, tq, tk)

    clf = jnp.pad(cache_logf[0], ((0, 0), (0, 0), (0, LANES - N_HEADS)))
    tks = P + Ts + (-(P + Ts)) % LANES
    ka_c, va_c, f_tot = _cache_prep(cache_k[0].astype(BF16).reshape(Bs, P, ATT_W),
                                    cache_v[0].astype(BF16).reshape(Bs, P, ATT_W), clf, tks)
    u_s, k_s, v_s, logf_s, qa_s, ka_s, va_s, _ = _inproj(x_sample, f_tot, lng, lnb, w_proj, b_proj, (ka_c, va_c), P)
    visit_all = jnp.zeros((Bs * (N_HEADS // 2),), I32)
    ya_s = _attention(visit_all, qa_s, ka_s, va_s, P, Ts, tks)

    cnt0 = jnp.zeros((1, LANES), F32)
    hist_p = jnp.zeros((B, HALO, POOL_W), F32)
    n_tok = B * T + Bs * Ts
    *bufs, code_p, cnt_p = _mixer(x_prompt, u_p, hist_p, ya_p, cnt0, 0, mixer_weights, n_tok, 0, ())
    u_full_s = jnp.concatenate([cache_pool[0].astype(F32), u_s], axis=1)
    hist_s = jnp.pad(cache_pool[0].astype(F32), ((0, 0), (HALO - POOL_HIST, 0), (0, 0)))
    h_all, hp_all, tw, code_s, cnt = _mixer(x_sample, u_s, hist_s, ya_s, cnt_p, P, mixer_weights,
                                            n_tok, B * T, tuple(bufs))
    by_slot = lambda c: c[:TOP_K] if c.shape[0] == CODE_ROWS else c[:, :TOP_K].T
    code = jnp.concatenate([by_slot(code_p), by_slot(code_s)], axis=1)
    ti, rk = code // RANK_SPAN, code % RANK_SPAN

    counts = cnt[0, :N_EXPERTS].astype(I32)
    padded = (counts + MOE_BLOCK - 1) // MOE_BLOCK * MOE_BLOCK
    pad_ends = jnp.cumsum(padded)
    pad_starts = pad_ends - padded
    n_blocks = (n_tok * TOP_K + N_EXPERTS * (MOE_BLOCK - 1) + MOE_BLOCK - 1) // MOE_BLOCK
    n_rows = n_blocks * MOE_BLOCK
    block_start = jnp.arange(n_blocks, dtype=I32) * MOE_BLOCK
    block_e = jnp.minimum(jnp.sum((pad_ends[None, :] <= block_start[:, None]).astype(I32), axis=1), N_EXPERTS - 1)
    n_used = (pad_ends[-1:] // MOE_BLOCK).astype(I32)
    start_of = sum(jnp.where(ti == e, pad_starts[e], 0) for e in range(N_EXPERTS))
    dest_t = (start_of + rk).astype(I32)
    dest_planes = jnp.stack([dest_t, dest_t + n_rows], axis=1)

    xb = _sc_scatter_rows(hp_all.reshape(2 * n_tok, PLANE), dest_planes.reshape(TOP_K, 2 * n_tok), 2 * n_rows)
    first = jnp.concatenate([jnp.ones((1,), I32), (block_e[1:] != block_e[:-1]).astype(I32)])
    yb = _experts(block_e, n_used, first, xb.reshape(2, n_rows, PLANE),
                  w1[0], b1[0][:, None, :], w2[0], b2[0][:, None, :])
    y4 = _sc_gather_rows(yb.reshape(2 * n_rows, PLANE), dest_planes.reshape(1, TOP_K * 2 * n_tok))
    y4 = y4.reshape(TOP_K, 2, n_tok, PLANE)
    ln2g, ln2b = row2(ln2_g[0]), row2(ln2_b[0])
    y_prompt = _combine(h_all, tw, ln2g, ln2b, y4, 0, B * T).reshape(B, T, D)
    y_sample = _combine(h_all, tw, ln2g, ln2b, y4, B * T, Bs * Ts).reshape(Bs, Ts, D)
    heads = lambda a, b_, t_: a.reshape(1, b_, t_, N_HEADS, HEAD_DIM)
    return (y_prompt, y_sample,
            heads(k_p, B, T), heads(v_p, B, T), logf_p[None], u_p[:, -POOL_HIST:][None],
            heads(k_s, Bs, Ts), heads(v_s, Bs, Ts), logf_s[None], u_full_s[:, -POOL_HIST:][None])
```

```python
import functools

import jax
import jax.numpy as jnp
from jax import lax
from jax.experimental import pallas as pl
from jax.experimental.pallas import tpu as pltpu
from jax.experimental.pallas import tpu_sc as plsc

F32 = jnp.float32
BF16 = jnp.bfloat16
I32 = jnp.int32
U32 = jnp.uint32

D_MODEL = 1024
N_HEADS = 8
HEAD_DIM = 64
ATT_W = N_HEADS * HEAD_DIM
POOL_WINDOWS = (2, 4, 8, 16)
POOL_GC = 128
POOL_W = len(POOL_WINDOWS) * POOL_GC
POOL_HIST = max(POOL_WINDOWS) - 1
HALO = 16
N_EXPERTS = 32
TOP_K = 4
D_FF = D_MODEL
SWIGLU_ALPHA = 1.702
SWIGLU_LIMIT = 7.0
LN_EPS = 1e-5
DEPTH = 1
DEEPNORM_ALPHA = (2.0 * DEPTH) ** 0.25
ATT_SCALE = HEAD_DIM ** -0.5
LOG2E = 1.4426950408889634
MAIN_W = POOL_W + 3 * ATT_W
LANES = 128
AUG_W = LANES
HALF = D_MODEL // 2
VMEM_LIMIT = 56 * 1024 * 1024

SEQ_TILE = 512
COMBINE_TILE = 1024
ATT_TQ = 1024
ATT_TK = 1024
N_STATS = 4
SKIP_LOG2 = 150.0
MOE_BLOCK = 512
RANK_SPAN = 1 << 20
CODE_ROWS = 8
PLANE = HALF // 2
SC_WINDOW = 128


def _layer_norm(x, g, b):
    mu = jnp.mean(x, axis=-1, keepdims=True)
    xc = x - mu
    var = jnp.mean(xc * xc, axis=-1, keepdims=True)
    return xc * lax.rsqrt(var + LN_EPS) * g + b


def _split3(x):
    a = x.astype(BF16)
    r = x - a.astype(F32)
    b = r.astype(BF16)
    c = (r - b.astype(F32)).astype(BF16)
    return a, b, c


def _prefix_rows(x, tm, inclusive, small_ints=False):
    kp = max(tm, LANES)
    r = lax.broadcasted_iota(I32, (tm, kp), 0)
    c = lax.broadcasted_iota(I32, (tm, kp), 1)
    tri = ((c <= r) if inclusive else (c < r)).astype(BF16)
    if kp > tm:
        x = jnp.concatenate([x, jnp.zeros((kp - tm, x.shape[1]), x.dtype)], axis=0)
    out = None
    for piece in ((x.astype(BF16),) if small_ints else _split3(x)):
        y = jnp.dot(tri, piece, preferred_element_type=F32)
        out = y if out is None else out + y
    return out


def _pack_halves(y):
    hi = pltpu.bitcast(y[:, :HALF].astype(BF16).astype(F32), U32)
    lo = pltpu.bitcast(y[:, HALF:].astype(BF16).astype(F32), U32)
    return hi | (lo >> 16)


def _unpack_halves(w):
    hi = pltpu.bitcast(w & jnp.uint32(0xFFFF0000), F32)
    lo = pltpu.bitcast(w << 16, F32)
    return hi, lo


def _head_slab(p, off, h):
    s = p[:, off + (h // 2) * LANES: off + (h // 2) * LANES + LANES]
    return s if h % 2 == 0 else pltpu.roll(s, HEAD_DIM, 1)


def _augment(qs, ks, vs, f_col, lane):
    fc = jnp.broadcast_to(f_col, lane.shape)
    hi = fc.astype(BF16).astype(F32)
    r1 = fc - hi
    mid = r1.astype(BF16).astype(F32)
    lo = r1 - mid
    one = jnp.ones_like(fc)
    zero = jnp.zeros_like(fc)
    d = HEAD_DIM
    ka = jnp.where(lane < d, ks, jnp.where(lane < d + 3, one, jnp.where(
        lane == d + 3, -hi, jnp.where(lane == d + 4, -mid, jnp.where(lane == d + 5, -lo, zero)))))
    va = jnp.where(lane < d, vs, jnp.where(lane == d, one, zero))
    if qs is None:
        return None, ka.astype(BF16), va.astype(BF16)
    qa = jnp.where(lane < d, qs, jnp.where(lane == d, hi, jnp.where(
        lane == d + 1, mid, jnp.where(lane == d + 2, lo, jnp.where(lane < d + 6, one, zero)))))
    return qa.astype(BF16), ka.astype(BF16), va.astype(BF16)


PROJ_W = MAIN_W + LANES
PROJ_CHUNKS = ((0, POOL_W), (POOL_W, POOL_W + ATT_W), (POOL_W + ATT_W, POOL_W + 2 * ATT_W),
               (POOL_W + 2 * ATT_W, PROJ_W))


def _inproj_kernel(x_ref, xnext_ref, f0_ref, lng_ref, lnb_ref, w_ref, b_ref, *rest, tm):
    u_ref, k_ref, v_ref, logf_ref, qa_ref, ka_ref, va_ref, st_ref, carry_ref, proj_ref = rest[-10:]
    t = pl.program_id(1)
    lin = pl.program_id(0) * pl.num_programs(1) + t

    @pl.when(t == 0)
    def _():
        carry_ref[...] = f0_ref[...]

    def project(xn_bf, chunk):
        lo, hi = PROJ_CHUNKS[chunk]
        return jnp.dot(xn_bf, w_ref[:, lo:hi], preferred_element_type=F32) + b_ref[:, lo:hi]

    @pl.when(lin == 0)
    def _():
        xn = _layer_norm(x_ref[...], lng_ref[...], lnb_ref[...]).astype(BF16)
        for c, (lo, hi) in enumerate(PROJ_CHUNKS):
            proj_ref[:, lo:hi] = project(xn, c)

    p = proj_ref
    next_chunks = []
    xn_next = _layer_norm(xnext_ref[...], lng_ref[...], lnb_ref[...]).astype(BF16)
    fl = p[:, MAIN_W:]
    logf = jnp.minimum(fl, 0.0) - jnp.log(1.0 + jnp.exp(-jnp.abs(fl)))
    f_cum = _prefix_rows(logf, tm, True) + carry_ref[...]
    carry_ref[...] = f_cum[tm - 1:tm, :]

    u_ref[...] = p[:, :POOL_W]
    k_ref[...] = p[:, POOL_W + ATT_W:POOL_W + 2 * ATT_W]
    v_ref[...] = p[:, POOL_W + 2 * ATT_W:MAIN_W]
    logf_ref[...] = logf[:, :N_HEADS]

    lane = lax.broadcasted_iota(I32, (tm, AUG_W), 1)
    for h in range(N_HEADS):
        qs = _head_slab(p, POOL_W, h) * (ATT_SCALE * LOG2E)
        ks = _head_slab(p, POOL_W + ATT_W, h)
        vs = _head_slab(p, POOL_W + 2 * ATT_W, h)
        qa, ka, va = _augment(qs, ks, vs, f_cum[:, h:h + 1] * LOG2E, lane)
        qa_ref[h] = qa
        ka_ref[h] = ka
        va_ref[h] = va
        if h % 2 == 1:
            next_chunks.append(project(xn_next, h // 2))

    col = lax.broadcasted_iota(I32, (ATT_W, LANES), 0) // HEAD_DIM
    head_sum = (col == lax.broadcasted_iota(I32, (ATT_W, LANES), 1)).astype(BF16)
    qsec = (p[:, POOL_W:POOL_W + ATT_W] * (ATT_SCALE * LOG2E)).astype(BF16).astype(F32)
    ksec = p[:, POOL_W + ATT_W:POOL_W + 2 * ATT_W].astype(BF16).astype(F32)
    for r, sec in enumerate((qsec, ksec)):
        sq = jnp.dot((sec * sec).astype(BF16), head_sum, preferred_element_type=F32)
        st_ref[r:r + 1, :] = jnp.max(sq, axis=0, keepdims=True)
    st_ref[2:3, :] = f_cum[0:1, :] * LOG2E
    st_ref[3:4, :] = f_cum[tm - 1:tm, :] * LOG2E
    for (lo, hi), chunk in zip(PROJ_CHUNKS, next_chunks):
        proj_ref[:, lo:hi] = chunk


def _inproj(x, f0, lng, lnb, w, bias, kv_prev=(), kv_row_off=0):
    B, T, D = x.shape
    tm = min(SEQ_TILE, T)
    nt = T // tm
    grid = (B, nt)
    const = lambda b, t: (0, 0)
    row = lambda b, t: (b, t, 0)

    def next_row(b, t):
        lin = jnp.minimum(b * nt + t + 1, B * nt - 1)
        return (lin // nt, lin % nt, 0)

    aug = pl.BlockSpec((None, N_HEADS, tm, AUG_W), lambda b, t: (b, 0, t, 0))
    aug_shape = jax.ShapeDtypeStruct((B, N_HEADS, T, AUG_W), BF16)
    kv = pl.BlockSpec((None, N_HEADS, tm, AUG_W), lambda b, t: (b, 0, kv_row_off // tm + t, 0))
    kv_shape = jax.ShapeDtypeStruct(kv_prev[0].shape, BF16) if kv_prev else aug_shape
    n_in = 7
    return pl.pallas_call(
        functools.partial(_inproj_kernel, tm=tm),
        grid=grid,
        in_specs=[
            pl.BlockSpec((None, tm, D), row),
            pl.BlockSpec((None, tm, D), next_row),
            pl.BlockSpec((None, 1, LANES), lambda b, t: (b, 0, 0)),
            pl.BlockSpec((1, D), const), pl.BlockSpec((1, D), const),
            pl.BlockSpec((D, PROJ_W), const), pl.BlockSpec((1, PROJ_W), const),
        ] + [pl.BlockSpec(memory_space=pl.ANY) for _ in kv_prev],
        out_specs=[
            pl.BlockSpec((None, tm, POOL_W), row),
            pl.BlockSpec((None, tm, ATT_W), row),
            pl.BlockSpec((None, tm, ATT_W), row),
            pl.BlockSpec((None, tm, N_HEADS), row),
            aug, kv, kv,
            pl.BlockSpec((None, None, N_STATS, LANES), lambda b, t: (b, t, 0, 0)),
        ],
        out_shape=[
            jax.ShapeDtypeStruct((B, T, POOL_W), F32),
            jax.ShapeDtypeStruct((B, T, ATT_W), F32),
            jax.ShapeDtypeStruct((B, T, ATT_W), F32),
            jax.ShapeDtypeStruct((B, T, N_HEADS), F32),
            aug_shape, kv_shape, kv_shape,
            jax.ShapeDtypeStruct((B, T // tm, N_STATS, LANES), F32),
        ],
        scratch_shapes=[pltpu.VMEM((1, LANES), F32), pltpu.VMEM((tm, PROJ_W), F32)],
        input_output_aliases={n_in + i: 5 + i for i in range(len(kv_prev))},
        compiler_params=pltpu.CompilerParams(
            dimension_semantics=("arbitrary", "arbitrary"), vmem_limit_bytes=VMEM_LIMIT),
    )(x, x, f0, lng, lnb, w, bias, *kv_prev)


def _cache_kernel(ck_ref, cv_ref, clf_ref, ka_ref, va_ref, ftot_ref, carry_ref, *, tp):
    t = pl.program_id(1)

    @pl.when(t == 0)
    def _():
        carry_ref[...] = jnp.zeros_like(carry_ref)

    @pl.when(t < pl.num_programs(1) - 1)
    def _():
        f_cum = _prefix_rows(clf_ref[...], tp, True) + carry_ref[...]
        carry_ref[...] = f_cum[tp - 1:tp, :]
        ftot_ref[...] = f_cum[tp - 1:tp, :]
        ck = ck_ref[...].astype(F32)
        cv = cv_ref[...].astype(F32)
        lane = lax.broadcasted_iota(I32, (tp, AUG_W), 1)
        for h in range(N_HEADS):
            _, ka, va = _augment(None, _head_slab(ck, 0, h), _head_slab(cv, 0, h), f_cum[:, h:h + 1] * LOG2E, lane)
            ka_ref[h] = ka
            va_ref[h] = va

    @pl.when(t == pl.num_programs(1) - 1)
    def _():
        ka_ref[...] = jnp.zeros_like(ka_ref)
        va_ref[...] = jnp.zeros_like(va_ref)


def _cache_prep(ck, cv, clf, n_keys):
    B, P, _ = ck.shape
    tp = min(SEQ_TILE, P)
    nt = P // tp
    row = lambda b, t: (b, jnp.minimum(t, nt - 1), 0)
    aug = pl.BlockSpec((None, N_HEADS, tp, AUG_W), lambda b, t: (b, 0, t, 0))
    aug_shape = jax.ShapeDtypeStruct((B, N_HEADS, n_keys, AUG_W), BF16)
    assert P < n_keys <= P + tp
    return pl.pallas_call(
        functools.partial(_cache_kernel, tp=tp),
        grid=(B, nt + 1),
        in_specs=[pl.BlockSpec((None, tp, ATT_W), row), pl.BlockSpec((None, tp, ATT_W), row),
                  pl.BlockSpec((None, tp, LANES), row)],
        out_specs=[aug, aug, pl.BlockSpec((None, 1, LANES), lambda b, t: (b, 0, 0))],
        out_shape=[aug_shape, aug_shape, jax.ShapeDtypeStruct((B, 1, LANES), F32)],
        scratch_shapes=[pltpu.VMEM((1, LANES), F32)],
        compiler_params=pltpu.CompilerParams(
            dimension_semantics=("arbitrary", "arbitrary"), vmem_limit_bytes=VMEM_LIMIT,
            allow_input_fusion=[True, True, True]),
    )(ck, cv, clf)


def _attn_kernel(jmin_ref, qa_ref, ka_ref, va_ref, o_ref, *, tq, tk, q_off):
    b, hp, iq = pl.program_id(0), pl.program_id(1), pl.program_id(2)
    q_lo = q_off + iq * tq
    n_full = (q_lo + 1) // tk
    j_first = jmin_ref[(b * pl.num_programs(1) + hp) * pl.num_programs(2) + iq]
    lane = lax.broadcasted_iota(I32, (tq, AUG_W), 1)
    qs = (qa_ref[0], qa_ref[1])

    def visit(carry, k0, nk, r0, masked):
        new = []
        for hh in range(2):
            m_all, acc_all = carry[hh]
            m, acc, q = m_all[r0:], acc_all[r0:], qs[hh][r0:]
            k = ka_ref[hh, pl.ds(k0, nk), :]
            v = va_ref[hh, pl.ds(k0, nk), :]
            s = lax.dot_general(q, k, (((1,), (1,)), ((), ())), preferred_element_type=F32)
            if masked:
                qpos = q_lo + r0 + lax.broadcasted_iota(I32, (tq - r0, nk), 0)
                kpos = k0 + lax.broadcasted_iota(I32, (tq - r0, nk), 1)
                s = jnp.where(qpos >= kpos, s, -jnp.inf)
            m_new = jnp.maximum(m, jnp.max(s, axis=-1, keepdims=True))
            alpha = jnp.exp2(m - m_new)
            p = jnp.exp2(s - m_new)
            acc = acc * alpha + jnp.dot(p.astype(BF16), v, preferred_element_type=F32)
            if r0:
                m_new = jnp.concatenate([m_all[:r0], m_new], axis=0)
                acc = jnp.concatenate([acc_all[:r0], acc], axis=0)
            new.append((m_new, acc))
        return tuple(new)

    def step(j, carry):
        return visit(carry, pl.multiple_of(j * tk, tk), tk, 0, False)

    init = (jnp.full((tq, 1), -jnp.inf, F32), jnp.zeros((tq, AUG_W), F32))
    carry = lax.fori_loop(j_first, n_full, step, (init, init))
    k_diag = pl.multiple_of(n_full * tk, tk)
    if tq == tk and tq % (2 * LANES) == 0:
        half = tq // 2
        carry = visit(carry, k_diag, half, 0, True)
        carry = visit(carry, pl.multiple_of(k_diag + half, half), half, half, True)
    else:
        carry = visit(carry, k_diag, tk, 0, True)
    outs = [acc / acc[:, HEAD_DIM:HEAD_DIM + 1] for _, acc in carry]
    o_ref[...] = jnp.where(lane < HEAD_DIM, outs[0], pltpu.roll(outs[1], HEAD_DIM, 1)).astype(BF16)


def _attention(jmin, qa, ka, va, q_off, tq, tk):
    B, H, Tq, _ = qa.shape
    Tk = ka.shape[2]
    assert tk % tq == 0 and q_off % tq == 0 and tq > 1
    return pl.pallas_call(
        functools.partial(_attn_kernel, tq=tq, tk=tk, q_off=q_off),
        grid_spec=pltpu.PrefetchScalarGridSpec(
            num_scalar_prefetch=1,
            grid=(B, H // 2, Tq // tq),
            in_specs=[
                pl.BlockSpec((None, 2, tq, AUG_W), lambda b, hp, iq, jm: (b, hp, iq, 0)),
                pl.BlockSpec((None, 2, Tk, AUG_W), lambda b, hp, iq, jm: (b, hp, 0, 0)),
                pl.BlockSpec((None, 2, Tk, AUG_W), lambda b, hp, iq, jm: (b, hp, 0, 0)),
            ],
            out_specs=pl.BlockSpec((None, tq, 2 * HEAD_DIM), lambda b, hp, iq, jm: (b, iq, hp)),
        ),
        out_shape=jax.ShapeDtypeStruct((B, Tq, ATT_W), BF16),
        compiler_params=pltpu.CompilerParams(
            dimension_semantics=("arbitrary", "arbitrary", "arbitrary"), vmem_limit_bytes=VMEM_LIMIT),
    )(jmin, qa, ka, va)


def _attn_skip_plan(stats, tq, tk, tm):
    B, nt = stats.shape[:2]
    st = stats[..., :N_HEADS]
    per = lambda row, r: st[:, :, row].reshape(B, nt // r, r, N_HEADS)
    rq, rk = tq // tm, tk // tm
    nq, nk = nt // rq, nt // rk
    qn2, kn2, kn2_own = per(0, rq).max(2), per(1, rk).max(2), per(1, rq).max(2)
    f_first, f_last = per(2, rq)[:, :, 0], per(3, rk)[:, :, -1]
    slack = 1.01
    upper = jnp.sqrt(qn2[:, :, None] * kn2[:, None, :]) * slack + (f_first[:, :, None] - f_last[:, None, :])
    lower = -jnp.sqrt(qn2 * kn2_own) * slack
    weightless = upper - lower[:, :, None] <= -(SKIP_LOG2 + 2.0)
    j = jnp.arange(nk, dtype=I32)[None, None, :, None]
    n_full = ((jnp.arange(nq, dtype=I32) * tq + 1) // tk)[None, :, None, None]
    first = jnp.min(jnp.where(weightless | (j >= n_full), n_full, j), axis=2)
    first = jnp.min(first.reshape(B, nq, N_HEADS // 2, 2), axis=3)
    return jnp.swapaxes(first, 1, 2).reshape(-1).astype(I32)


GATE_CHUNKS = 8


def _gate_chunk(xn_bf, wg_ref, bg_ref, c):
    cols = slice(c * (2 * D_MODEL // GATE_CHUNKS), (c + 1) * (2 * D_MODEL // GATE_CHUNKS))
    return jax.nn.sigmoid(jnp.dot(xn_bf, wg_ref[:, cols], preferred_element_type=F32) + bg_ref[:, cols]), cols


def _mixer_kernel(x_ref, xnext_ref, u_ref, uprev_ref, hist_ref, ya_ref, cnt0_ref,
                  lng_ref, lnb_ref, wg_ref, bg_ref, wpool_ref, spool_ref, wpp_ref, watt_ref, wout_ref,
                  ln1g_ref, ln1b_ref, wr_ref, br_ref, *rest, tm, start_pos):
    h_ref, hp_ref, tw_ref, code_ref, cnt_ref, uext_ref, gates_ref, xn_ref = rest[-8:]
    b = pl.program_id(0)
    t = pl.program_id(1)

    @pl.when((b == 0) & (t == 0))
    def _():
        cnt_ref[...] = cnt0_ref[...]
        xn0 = _layer_norm(x_ref[...], lng_ref[...], lnb_ref[...])
        xn_ref[...] = xn0
        for c in range(GATE_CHUNKS):
            g, cols = _gate_chunk(xn0.astype(BF16), wg_ref, bg_ref, c)
            gates_ref[:, cols] = g

    @pl.when(t == 0)
    def _():
        uext_ref[0:HALO, :] = hist_ref[...]

    @pl.when(t > 0)
    def _():
        uext_ref[0:HALO, :] = uprev_ref[...]

    uext_ref[HALO:HALO + tm, :] = u_ref[...]

    pos = start_pos + t * tm + lax.broadcasted_iota(I32, (tm, 1), 0)
    groups = []
    for gi, w in enumerate(POOL_WINDOWS):
        sl = slice(gi * POOL_GC, (gi + 1) * POOL_GC)
        cur = uext_ref[HALO:HALO + tm, sl]
        s = cur
        for j in range(1, w):
            s = s + uext_ref[HALO - j:HALO - j + tm, sl]
        count = jnp.minimum(pos + 1, w).astype(F32)
        d = s / count - cur
        yg = jnp.dot(d.astype(BF16), wpool_ref[gi], preferred_element_type=F32) * spool_ref[:, sl]
        groups.append(yg.astype(BF16))
    pooled = jnp.dot(jnp.concatenate(groups, axis=1), wpp_ref[...], preferred_element_type=F32)
    att = jnp.dot(ya_ref[...], watt_ref[...], preferred_element_type=F32)
    m = gates_ref[:, :D_MODEL] * pooled + gates_ref[:, D_MODEL:] * att
    mix = jnp.dot(m.astype(BF16), wout_ref[...], preferred_element_type=F32)

    xn_next_f32 = _layer_norm(xnext_ref[...], lng_ref[...], lnb_ref[...])
    xn_next = xn_next_f32.astype(BF16)

    def emit_next_gates(c):
        g, cols = _gate_chunk(xn_next, wg_ref, bg_ref, c)
        gates_ref[:, cols] = g

    emit_next_gates(0)

    h = _layer_norm(DEEPNORM_ALPHA * xn_ref[...] + mix, ln1g_ref[...], ln1b_ref[...])
    xn_ref[...] = xn_next_f32
    h_ref[...] = h
    packed = _pack_halves(h)
    hp_ref[0] = packed[:, :PLANE]
    hp_ref[1] = packed[:, PLANE:]
    emit_next_gates(1)
    emit_next_gates(2)

    hh = h.astype(BF16)
    hl = (h - hh.astype(F32)).astype(BF16)
    hi_pass = jnp.dot(hh, wr_ref[...], preferred_element_type=F32)
    lo_pass = jnp.dot(hl, wr_ref[:, :LANES], preferred_element_type=F32)
    logits = hi_pass[:, :LANES] + lo_pass + hi_pass[:, LANES:] + br_ref[...]
    emit_next_gates(3)
    lane = lax.broadcasted_iota(I32, (tm, LANES), 1)
    work = jnp.where(lane < N_EXPERTS, logits, -jnp.inf)
    vals, idxs = [], []
    for r in range(TOP_K):
        mx = jnp.max(work, axis=-1, keepdims=True)
        idx = jnp.min(jnp.where(work == mx, lane, LANES), axis=-1, keepdims=True)
        vals.append(mx)
        idxs.append(idx)
        work = jnp.where(lane == idx, -jnp.inf, work)
        emit_next_gates(GATE_CHUNKS - TOP_K + r)
    exps = [jnp.exp(v - vals[0]) for v in vals]
    denom = exps[0] + exps[1] + exps[2] + exps[3]

    onehot = jnp.zeros((tm, LANES), F32)
    for idx in idxs:
        onehot = onehot + (lane == idx).astype(F32)
    base = _prefix_rows(onehot, tm, False, small_ints=True) + cnt_ref[...]
    code = jnp.zeros((tm, LANES), I32)
    tw = jnp.zeros((tm, LANES), F32)
    for k in range(TOP_K):
        rank = jnp.sum(jnp.where(lane == idxs[k], base, 0.0), axis=-1, keepdims=True)
        code = jnp.where(lane == k, idxs[k] * RANK_SPAN + rank.astype(I32), code)
        tw = jnp.where(lane == k, exps[k] / denom, tw)
    tw_ref[...] = tw
    if code_ref.shape[0] == CODE_ROWS:
        code_ref[...] = jnp.transpose(code)[:CODE_ROWS, :]
    else:
        code_ref[...] = code
    cnt_ref[...] = cnt_ref[...] + jnp.sum(onehot, axis=0, keepdims=True)


def _mixer(x, u, hist, yatt, cnt0, start_pos, weights, n_tok, row_off, prev):
    B, T, D = x.shape
    tm = min(SEQ_TILE, T)
    nt = T // tm
    row = lambda b, t: (b, t, 0)
    tok = lambda b, t: (row_off // tm + b * nt + t, 0)
    hpt = tm // HALO
    full = lambda a: pl.BlockSpec(a.shape, lambda b, t, _n=a.ndim: (0,) * _n)
    tok3 = lambda b, t: (0, row_off // tm + b * nt + t, 0)
    assert n_tok <= RANK_SPAN
    if tm % LANES == 0:
        code_spec = pl.BlockSpec((CODE_ROWS, tm), lambda b, t: (0, b * nt + t))
        code_shape = jax.ShapeDtypeStruct((CODE_ROWS, B * T), I32)
    else:
        code_spec = pl.BlockSpec((tm, LANES), lambda b, t: (b * nt + t, 0))
        code_shape = jax.ShapeDtypeStruct((B * T, LANES), I32)
    out_specs = [pl.BlockSpec((tm, D), tok), pl.BlockSpec((2, tm, PLANE), tok3), pl.BlockSpec((tm, LANES), tok),
                 code_spec, pl.BlockSpec((1, LANES), lambda b, t: (0, 0))]
    out_shape = [jax.ShapeDtypeStruct((n_tok, D), F32), jax.ShapeDtypeStruct((2, n_tok, PLANE), U32),
                 jax.ShapeDtypeStruct((n_tok, LANES), F32), code_shape, jax.ShapeDtypeStruct((1, LANES), F32)]
    n_in = 7 + len(weights)

    def next_row(b, t):
        lin = jnp.minimum(b * nt + t + 1, B * nt - 1)
        return (lin // nt, lin % nt, 0)

    return pl.pallas_call(
        functools.partial(_mixer_kernel, tm=tm, start_pos=start_pos),
        grid=(B, nt),
        in_specs=[
            pl.BlockSpec((None, tm, D), row),
            pl.BlockSpec((None, tm, D), next_row),
            pl.BlockSpec((None, tm, POOL_W), row),
            pl.BlockSpec((None, HALO, POOL_W), lambda b, t: (b, jnp.maximum(t * hpt - 1, 0), 0)),
            pl.BlockSpec((None, HALO, POOL_W), lambda b, t: (b, 0, 0)),
            pl.BlockSpec((None, tm, ATT_W), row),
            pl.BlockSpec((1, LANES), lambda b, t: (0, 0)),
        ] + [full(w) for w in weights] + [pl.BlockSpec(memory_space=pl.ANY) for _ in prev],
        out_specs=out_specs,
        out_shape=out_shape,
        scratch_shapes=[pltpu.VMEM((HALO + tm, POOL_W), F32), pltpu.VMEM((tm, 2 * D), F32),
                        pltpu.VMEM((tm, D), F32)],
        input_output_aliases={n_in + i: i for i in range(len(prev))},
        compiler_params=pltpu.CompilerParams(
            dimension_semantics=("arbitrary", "arbitrary"), vmem_limit_bytes=VMEM_LIMIT),
    )(x, x, u, u, hist, yatt, cnt0, *weights, *prev)


def _sc_mesh():
    return plsc.VectorSubcoreMesh(core_axis_name="c", subcore_axis_name="s")


def _sc_scatter_rows(x, idx, n_rows):
    n = x.shape[0]

    @pl.kernel(out_type=jax.ShapeDtypeStruct((n_rows, PLANE), x.dtype), mesh=_sc_mesh(), scratch_types=[])
    def scatter(x_hbm, i_hbm, o_hbm):
        def body(x_vmem, i_vmem):
            for k in range(TOP_K):
                pltpu.sync_copy(x_vmem, o_hbm.at[i_vmem.at[k]])

        pltpu.emit_pipeline(
            body,
            grid=(n // SC_WINDOW,),
            in_specs=[pl.BlockSpec((SC_WINDOW, PLANE), index_map=lambda i: (i, 0)),
                      pl.BlockSpec((TOP_K, SC_WINDOW), index_map=lambda i: (0, i))],
            out_specs=[],
            core_axis_name=("c", "s"),
            dimension_semantics=(pltpu.PARALLEL,),
        )(x_hbm, i_hbm)

    return scatter(x, idx)


def _sc_gather_rows(y, idx):
    n = idx.shape[1]

    @pl.kernel(out_type=jax.ShapeDtypeStruct((n, PLANE), y.dtype), mesh=_sc_mesh(), scratch_types=[])
    def gather(y_hbm, i_hbm, o_hbm):
        def body(i_vmem, o_vmem):
            pltpu.sync_copy(y_hbm.at[i_vmem.at[0]], o_vmem)

        pltpu.emit_pipeline(
            body,
            grid=(n // SC_WINDOW,),
            in_specs=[pl.BlockSpec((1, SC_WINDOW), index_map=lambda i: (0, i))],
            out_specs=[pl.BlockSpec((SC_WINDOW, PLANE), index_map=lambda i: (i, 0))],
            core_axis_name=("c", "s"),
            dimension_semantics=(pltpu.PARALLEL,),
        )(i_hbm, o_hbm)

    return gather(y, idx)


def _join_planes(ref):
    return jnp.concatenate([ref[0], ref[1]], axis=1)


def _unpack_rows(xb_ref):
    xa, xc = _unpack_halves(_join_planes(xb_ref))
    return jnp.concatenate([xa.astype(BF16), xc.astype(BF16)], axis=1)


def _expert_kernel(be_ref, nb_ref, first_ref, xb_ref, xbnext_ref, w1_ref, b1_ref, w2_ref, b2_ref,
                   yb_ref, w1b_ref, w2b_ref, x_ref):
    del be_ref
    i = pl.program_id(0)

    @pl.when(i == 0)
    def _():
        x_ref[...] = _unpack_rows(xb_ref)

    @pl.when(first_ref[i] == 1)
    def _():
        w1b_ref[...] = w1_ref[...].astype(BF16)
        w2b_ref[...] = w2_ref[...].astype(BF16)

    @pl.when(i < nb_ref[0])
    def _():
        hfull = jnp.dot(x_ref[...], w1b_ref[...], preferred_element_type=F32) + b1_ref[...]
        glu = jnp.minimum(hfull[:, :D_FF], SWIGLU_LIMIT)
        lin = jnp.clip(hfull[:, D_FF:], -SWIGLU_LIMIT, SWIGLU_LIMIT)
        a = glu * jax.nn.sigmoid(SWIGLU_ALPHA * glu) * (lin + 1.0)
        x_next = _unpack_rows(xbnext_ref)
        y = jnp.dot(a.astype(BF16), w2b_ref[...], preferred_element_type=F32) + b2_ref[...]
        packed = _pack_halves(y)
        yb_ref[0] = packed[:, :PLANE]
        yb_ref[1] = packed[:, PLANE:]
        x_ref[...] = x_next

    @pl.when(i >= nb_ref[0])
    def _():
        yb_ref[...] = jnp.zeros_like(yb_ref)


def _experts(block_e, n_used, first, xb, w1, b1, w2, b2):
    n_rows = xb.shape[1]
    bm = MOE_BLOCK
    n_blocks = n_rows // bm
    rows = pl.BlockSpec((2, bm, PLANE), lambda i, be, nb, fi: (0, i, 0))
    next_rows = pl.BlockSpec((2, bm, PLANE), lambda i, be, nb, fi: (0, jnp.minimum(i + 1, n_blocks - 1), 0))
    per_expert = lambda r, c: pl.BlockSpec((None, r, c), lambda i, be, nb, fi: (be[i], 0, 0))
    return pl.pallas_call(
        _expert_kernel,
        grid_spec=pltpu.PrefetchScalarGridSpec(
            num_scalar_prefetch=3,
            grid=(n_blocks,),
            in_specs=[rows, next_rows, per_expert(D_MODEL, 2 * D_FF), per_expert(1, 2 * D_FF),
                      per_expert(D_FF, D_MODEL), per_expert(1, D_MODEL)],
            out_specs=rows,
            scratch_shapes=[pltpu.VMEM((D_MODEL, 2 * D_FF), BF16), pltpu.VMEM((D_FF, D_MODEL), BF16),
                            pltpu.VMEM((bm, D_MODEL), BF16)],
        ),
        out_shape=jax.ShapeDtypeStruct((2, n_rows, PLANE), U32),
        compiler_params=pltpu.CompilerParams(
            dimension_semantics=("arbitrary",), vmem_limit_bytes=VMEM_LIMIT),
    )(block_e, n_used, first, xb, xb, w1, b1, w2, b2)


def _combine_kernel(h_ref, tw_ref, g_ref, b_ref, y4_ref, out_ref):
    tw = tw_ref[...]
    acc_hi = None
    acc_lo = None
    for k in range(TOP_K):
        hi, lo = _unpack_halves(_join_planes(y4_ref.at[k]))
        w = tw[:, k:k + 1]
        acc_hi = w * hi if acc_hi is None else acc_hi + w * hi
        acc_lo = w * lo if acc_lo is None else acc_lo + w * lo
    moe = jnp.concatenate([acc_hi, acc_lo], axis=1)
    out_ref[...] = _layer_norm(DEEPNORM_ALPHA * h_ref[...] + moe, g_ref[...], b_ref[...])


def _combine(h, tw, ln2g, ln2b, y4, row_off, n_rows):
    D = h.shape[1]
    tn = min(COMBINE_TILE, n_rows)
    off = row_off // tn
    const = lambda i: (0, 0)
    return pl.pallas_call(
        _combine_kernel,
        grid=(n_rows // tn,),
        in_specs=[
            pl.BlockSpec((tn, D), lambda i: (off + i, 0)),
            pl.BlockSpec((tn, LANES), lambda i: (off + i, 0)),
            pl.BlockSpec((1, D), const), pl.BlockSpec((1, D), const),
            pl.BlockSpec((TOP_K, 2, tn, PLANE), lambda i: (0, 0, off + i, 0)),
        ],
        out_specs=pl.BlockSpec((tn, D), lambda i: (i, 0)),
        out_shape=jax.ShapeDtypeStruct((n_rows, D), F32),
        compiler_params=pltpu.CompilerParams(
            dimension_semantics=("arbitrary",), vmem_limit_bytes=VMEM_LIMIT),
    )(h, tw, ln2g, ln2b, y4)


def kernel(x_prompt, x_sample, cache_pool, cache_k, cache_v, cache_logf, ln_in_g, ln_in_b, w_in, b_in,
           w_pool, s_pool, w_pool_proj, w_att_proj, w_out, ln1_g, ln1_b, w_router, b_router,
           w1, b1, w2, b2, ln2_g, ln2_b):
    assert w_in.shape[0] == DEPTH
    B, T, D = x_prompt.shape
    Bs, Ts, _ = x_sample.shape
    P = cache_k.shape[2]
    row2 = lambda a: a.reshape(1, -1).astype(F32)

    f_off = MAIN_W
    g_off = MAIN_W + N_HEADS
    w_proj = jnp.pad(w_in[0][:, :g_off], ((0, 0), (0, LANES - N_HEADS))).astype(BF16)
    b_proj = row2(jnp.pad(b_in[0][:g_off], (0, LANES - N_HEADS)))
    wg = w_in[0][:, g_off:].astype(BF16)
    bg = row2(b_in[0][g_off:])
    lng, lnb = row2(ln_in_g), row2(ln_in_b)
    wr = jnp.pad(w_router[0], ((0, 0), (0, LANES - N_EXPERTS)))
    wr_hi = wr.astype(BF16)
    wr_lo = (wr - wr_hi.astype(F32)).astype(BF16)
    br = row2(jnp.pad(b_router[0], (0, LANES - N_EXPERTS)))
    mixer_weights = (lng, lnb, wg, bg, w_pool[0].astype(BF16), row2(s_pool[0]),
                     w_pool_proj[0].astype(BF16), w_att_proj[0].astype(BF16), w_out[0].astype(BF16),
                     row2(ln1_g[0]), row2(ln1_b[0]), jnp.concatenate([wr_hi, wr_lo], axis=1), br)

    zeros_f = jnp.zeros((B, 1, LANES), F32)
    u_p, k_p, v_p, logf_p, qa_p, ka_p, va_p, stats = _inproj(x_prompt, zeros_f, lng, lnb, w_proj, b_proj)
    tq, tk = min(ATT_TQ, T), min(ATT_TK, T)
    ya_p = _attention(_attn_skip_plan(stats, tq, tk, min(SEQ_TILE, T)), qa_p, ka_p, va_p, 0, tq, tk)

    clf = jnp.pad(cache_logf[0], ((0, 0), (0, 0), (0, LANES - N_HEADS)))
    tks = P + Ts + (-(P + Ts)) % LANES
    ka_c, va_c, f_tot = _cache_prep(cache_k[0].astype(BF16).reshape(Bs, P, ATT_W),
                                    cache_v[0].astype(BF16).reshape(Bs, P, ATT_W), clf, tks)
    u_s, k_s, v_s, logf_s, qa_s, ka_s, va_s, _ = _inproj(x_sample, f_tot, lng, lnb, w_proj, b_proj, (ka_c, va_c), P)
    visit_all = jnp.zeros((Bs * (N_HEADS // 2),), I32)
    ya_s = _attention(visit_all, qa_s, ka_s, va_s, P, Ts, tks)

    cnt0 = jnp.zeros((1, LANES), F32)
    hist_p = jnp.zeros((B, HALO, POOL_W), F32)
    n_tok = B * T + Bs * Ts
    *bufs, code_p, cnt_p = _mixer(x_prompt, u_p, hist_p, ya_p, cnt0, 0, mixer_weights, n_tok, 0, ())
    u_full_s = jnp.concatenate([cache_pool[0].astype(F32), u_s], axis=1)
    hist_s = jnp.pad(cache_pool[0].astype(F32), ((0, 0), (HALO - POOL_HIST, 0), (0, 0)))
    h_all, hp_all, tw, code_s, cnt = _mixer(x_sample, u_s, hist_s, ya_s, cnt_p, P, mixer_weights,
                                            n_tok, B * T, tuple(bufs))
    by_slot = lambda c: c[:TOP_K] if c.shape[0] == CODE_ROWS else c[:, :TOP_K].T
    code = jnp.concatenate([by_slot(code_p), by_slot(code_s)], axis=1)
    ti, rk = code // RANK_SPAN, code % RANK_SPAN

    counts = cnt[0, :N_EXPERTS].astype(I32)
    padded = (counts + MOE_BLOCK - 1) // MOE_BLOCK * MOE_BLOCK
    pad_ends = jnp.cumsum(padded)
    pad_starts = pad_ends - padded
    n_blocks = (n_tok * TOP_K + N_EXPERTS * (MOE_BLOCK - 1) + MOE_BLOCK - 1) // MOE_BLOCK
    n_rows = n_blocks * MOE_BLOCK
    block_start = jnp.arange(n_blocks, dtype=I32) * MOE_BLOCK
    block_e = jnp.minimum(jnp.sum((pad_ends[None, :] <= block_start[:, None]).astype(I32), axis=1), N_EXPERTS - 1)
    n_used = (pad_ends[-1:] // MOE_BLOCK).astype(I32)
    start_of = sum(jnp.where(ti == e, pad_starts[e], 0) for e in range(N_EXPERTS))
    dest_t = (start_of + rk).astype(I32)
    dest_planes = jnp.stack([dest_t, dest_t + n_rows], axis=1)

    xb = _sc_scatter_rows(hp_all.reshape(2 * n_tok, PLANE), dest_planes.reshape(TOP_K, 2 * n_tok), 2 * n_rows)
    first = jnp.concatenate([jnp.ones((1,), I32), (block_e[1:] != block_e[:-1]).astype(I32)])
    yb = _experts(block_e, n_used, first, xb.reshape(2, n_rows, PLANE),
                  w1[0], b1[0][:, None, :], w2[0], b2[0][:, None, :])
    y4 = _sc_gather_rows(yb.reshape(2 * n_rows, PLANE), dest_planes.reshape(1, TOP_K * 2 * n_tok))
    y4 = y4.reshape(TOP_K, 2, n_tok, PLANE)
    ln2g, ln2b = row2(ln2_g[0]), row2(ln2_b[0])
    y_prompt = _combine(h_all, tw, ln2g, ln2b, y4, 0, B * T).reshape(B, T, D)
    y_sample = _combine(h_all, tw, ln2g, ln2b, y4, B * T, Bs * Ts).reshape(Bs, Ts, D)
    heads = lambda a, b_, t_: a.reshape(1, b_, t_, N_HEADS, HEAD_DIM)
    return (y_prompt, y_sample,
            heads(k_p, B, T), heads(v_p, B, T), logf_p[None], u_p[:, -POOL_HIST:][None],
            heads(k_s, Bs, Ts), heads(v_s, Bs, Ts), logf_s[None], u_full_s[:, -POOL_HIST:][None])
```

```python
import functools

import jax
import jax.numpy as jnp
from jax import lax
from jax.experimental import pallas as pl
from jax.experimental.pallas import tpu as pltpu
from jax.experimental.pallas import tpu_sc as plsc

F32 = jnp.float32
BF16 = jnp.bfloat16
I32 = jnp.int32
U32 = jnp.uint32

D_MODEL = 1024
N_HEADS = 8
HEAD_DIM = 64
ATT_W = N_HEADS * HEAD_DIM
POOL_WINDOWS = (2, 4, 8, 16)
POOL_GC = 128
POOL_W = len(POOL_WINDOWS) * POOL_GC
POOL_HIST = max(POOL_WINDOWS) - 1
HALO = 16
N_EXPERTS = 32
TOP_K = 4
D_FF = D_MODEL
SWIGLU_ALPHA = 1.702
SWIGLU_LIMIT = 7.0
LN_EPS = 1e-5
DEPTH = 1
DEEPNORM_ALPHA = (2.0 * DEPTH) ** 0.25
ATT_SCALE = HEAD_DIM ** -0.5
LOG2E = 1.4426950408889634
MAIN_W = POOL_W + 3 * ATT_W
LANES = 128
AUG_W = LANES
HALF = D_MODEL // 2
VMEM_LIMIT = 56 * 1024 * 1024

SEQ_TILE = 512
COMBINE_TILE = 1024
ATT_TQ = 1024
ATT_TK = 1024
N_STATS = 4
SKIP_LOG2 = 150.0
MOE_BLOCK = 512
RANK_SPAN = 1 << 20
CODE_ROWS = 8
PLANE = HALF // 2
SC_WINDOW = 128


def _layer_norm(x, g, b):
    mu = jnp.mean(x, axis=-1, keepdims=True)
    xc = x - mu
    var = jnp.mean(xc * xc, axis=-1, keepdims=True)
    return xc * lax.rsqrt(var + LN_EPS) * g + b


def _split3(x):
    a = x.astype(BF16)
    r = x - a.astype(F32)
    b = r.astype(BF16)
    c = (r - b.astype(F32)).astype(BF16)
    return a, b, c


def _prefix_rows(x, tm, inclusive, small_ints=False):
    kp = max(tm, LANES)
    r = lax.broadcasted_iota(I32, (tm, kp), 0)
    c = lax.broadcasted_iota(I32, (tm, kp), 1)
    tri = ((c <= r) if inclusive else (c < r)).astype(BF16)
    if kp > tm:
        x = jnp.concatenate([x, jnp.zeros((kp - tm, x.shape[1]), x.dtype)], axis=0)
    out = None
    for piece in ((x.astype(BF16),) if small_ints else _split3(x)):
        y = jnp.dot(tri, piece, preferred_element_type=F32)
        out = y if out is None else out + y
    return out


def _pack_halves(y):
    hi = pltpu.bitcast(y[:, :HALF].astype(BF16).astype(F32), U32)
    lo = pltpu.bitcast(y[:, HALF:].astype(BF16).astype(F32), U32)
    return hi | (lo >> 16)


def _unpack_halves(w):
    hi = pltpu.bitcast(w & jnp.uint32(0xFFFF0000), F32)
    lo = pltpu.bitcast(w << 16, F32)
    return hi, lo


def _head_slab(p, off, h):
    s = p[:, off + (h // 2) * LANES: off + (h // 2) * LANES + LANES]
    return s if h % 2 == 0 else pltpu.roll(s, HEAD_DIM, 1)


def _augment(qs, ks, vs, f_col, lane):
    fc = jnp.broadcast_to(f_col, lane.shape)
    hi = fc.astype(BF16).astype(F32)
    r1 = fc - hi
    mid = r1.astype(BF16).astype(F32)
    lo = r1 - mid
    one = jnp.ones_like(fc)
    zero = jnp.zeros_like(fc)
    d = HEAD_DIM
    ka = jnp.where(lane < d, ks, jnp.where(lane < d + 3, one, jnp.where(
        lane == d + 3, -hi, jnp.where(lane == d + 4, -mid, jnp.where(lane == d + 5, -lo, zero)))))
    va = jnp.where(lane < d, vs, jnp.where(lane == d, one, zero))
    if qs is None:
        return None, ka.astype(BF16), va.astype(BF16)
    qa = jnp.where(lane < d, qs, jnp.where(lane == d, hi, jnp.where(
        lane == d + 1, mid, jnp.where(lane == d + 2, lo, jnp.where(lane < d + 6, one, zero)))))
    return qa.astype(BF16), ka.astype(BF16), va.astype(BF16)


PROJ_W = MAIN_W + LANES
PROJ_CHUNKS = ((0, POOL_W), (POOL_W, POOL_W + ATT_W), (POOL_W + ATT_W, POOL_W + 2 * ATT_W),
               (POOL_W + 2 * ATT_W, PROJ_W))


def _inproj_kernel(x_ref, xnext_ref, f0_ref, lng_ref, lnb_ref, w_ref, b_ref, *rest, tm):
    u_ref, k_ref, v_ref, logf_ref, qa_ref, ka_ref, va_ref, st_ref, carry_ref, proj_ref = rest[-10:]
    t = pl.program_id(1)
    lin = pl.program_id(0) * pl.num_programs(1) + t

    @pl.when(t == 0)
    def _():
        carry_ref[...] = f0_ref[...]

    def project(xn_bf, chunk):
        lo, hi = PROJ_CHUNKS[chunk]
        return jnp.dot(xn_bf, w_ref[:, lo:hi], preferred_element_type=F32) + b_ref[:, lo:hi]

    @pl.when(lin == 0)
    def _():
        xn = _layer_norm(x_ref[...], lng_ref[...], lnb_ref[...]).astype(BF16)
        for c, (lo, hi) in enumerate(PROJ_CHUNKS):
            proj_ref[:, lo:hi] = project(xn, c)

    p = proj_ref
    next_chunks = []
    xn_next = _layer_norm(xnext_ref[...], lng_ref[...], lnb_ref[...]).astype(BF16)
    fl = p[:, MAIN_W:]
    logf = jnp.minimum(fl, 0.0) - jnp.log(1.0 + jnp.exp(-jnp.abs(fl)))
    f_cum = _prefix_rows(logf, tm, True) + carry_ref[...]
    carry_ref[...] = f_cum[tm - 1:tm, :]

    u_ref[...] = p[:, :POOL_W]
    k_ref[...] = p[:, POOL_W + ATT_W:POOL_W + 2 * ATT_W]
    v_ref[...] = p[:, POOL_W + 2 * ATT_W:MAIN_W]
    logf_ref[...] = logf[:, :N_HEADS]

    lane = lax.broadcasted_iota(I32, (tm, AUG_W), 1)
    for h in range(N_HEADS):
        qs = _head_slab(p, POOL_W, h) * (ATT_SCALE * LOG2E)
        ks = _head_slab(p, POOL_W + ATT_W, h)
        vs = _head_slab(p, POOL_W + 2 * ATT_W, h)
        qa, ka, va = _augment(qs, ks, vs, f_cum[:, h:h + 1] * LOG2E, lane)
        qa_ref[h] = qa
        ka_ref[h] = ka
        va_ref[h] = va
        if h % 2 == 1:
            next_chunks.append(project(xn_next, h // 2))

    col = lax.broadcasted_iota(I32, (ATT_W, LANES), 0) // HEAD_DIM
    head_sum = (col == lax.broadcasted_iota(I32, (ATT_W, LANES), 1)).astype(BF16)
    qsec = (p[:, POOL_W:POOL_W + ATT_W] * (ATT_SCALE * LOG2E)).astype(BF16).astype(F32)
    ksec = p[:, POOL_W + ATT_W:POOL_W + 2 * ATT_W].astype(BF16).astype(F32)
    for r, sec in enumerate((qsec, ksec)):
        sq = jnp.dot((sec * sec).astype(BF16), head_sum, preferred_element_type=F32)
        st_ref[r:r + 1, :] = jnp.max(sq, axis=0, keepdims=True)
    st_ref[2:3, :] = f_cum[0:1, :] * LOG2E
    st_ref[3:4, :] = f_cum[tm - 1:tm, :] * LOG2E
    for (lo, hi), chunk in zip(PROJ_CHUNKS, next_chunks):
        proj_ref[:, lo:hi] = chunk


def _inproj(x, f0, lng, lnb, w, bias, kv_prev=(), kv_row_off=0):
    B, T, D = x.shape
    tm = min(SEQ_TILE, T)
    nt = T // tm
    grid = (B, nt)
    const = lambda b, t: (0, 0)
    row = lambda b, t: (b, t, 0)

    def next_row(b, t):
        lin = jnp.minimum(b * nt + t + 1, B * nt - 1)
        return (lin // nt, lin % nt, 0)

    aug = pl.BlockSpec((None, N_HEADS, tm, AUG_W), lambda b, t: (b, 0, t, 0))
    aug_shape = jax.ShapeDtypeStruct((B, N_HEADS, T, AUG_W), BF16)
    kv = pl.BlockSpec((None, N_HEADS, tm, AUG_W), lambda b, t: (b, 0, kv_row_off // tm + t, 0))
    kv_shape = jax.ShapeDtypeStruct(kv_prev[0].shape, BF16) if kv_prev else aug_shape
    n_in = 7
    return pl.pallas_call(
        functools.partial(_inproj_kernel, tm=tm),
        grid=grid,
        in_specs=[
            pl.BlockSpec((None, tm, D), row),
            pl.BlockSpec((None, tm, D), next_row),
            pl.BlockSpec((None, 1, LANES), lambda b, t: (b, 0, 0)),
            pl.BlockSpec((1, D), const), pl.BlockSpec((1, D), const),
            pl.BlockSpec((D, PROJ_W), const), pl.BlockSpec((1, PROJ_W), const),
        ] + [pl.BlockSpec(memory_space=pl.ANY) for _ in kv_prev],
        out_specs=[
            pl.BlockSpec((None, tm, POOL_W), row),
            pl.BlockSpec((None, tm, ATT_W), row),
            pl.BlockSpec((None, tm, ATT_W), row),
            pl.BlockSpec((None, tm, N_HEADS), row),
            aug, kv, kv,
            pl.BlockSpec((None, None, N_STATS, LANES), lambda b, t: (b, t, 0, 0)),
        ],
        out_shape=[
            jax.ShapeDtypeStruct((B, T, POOL_W), F32),
            jax.ShapeDtypeStruct((B, T, ATT_W), F32),
            jax.ShapeDtypeStruct((B, T, ATT_W), F32),
            jax.ShapeDtypeStruct((B, T, N_HEADS), F32),
            aug_shape, kv_shape, kv_shape,
            jax.ShapeDtypeStruct((B, T // tm, N_STATS, LANES), F32),
        ],
        scratch_shapes=[pltpu.VMEM((1, LANES), F32), pltpu.VMEM((tm, PROJ_W), F32)],
        input_output_aliases={n_in + i: 5 + i for i in range(len(kv_prev))},
        compiler_params=pltpu.CompilerParams(
            dimension_semantics=("arbitrary", "arbitrary"), vmem_limit_bytes=VMEM_LIMIT),
    )(x, x, f0, lng, lnb, w, bias, *kv_prev)


def _cache_kernel(ck_ref, cv_ref, clf_ref, ka_ref, va_ref, ftot_ref, carry_ref, *, tp):
    t = pl.program_id(1)

    @pl.when(t == 0)
    def _():
        carry_ref[...] = jnp.zeros_like(carry_ref)

    @pl.when(t < pl.num_programs(1) - 1)
    def _():
        f_cum = _prefix_rows(clf_ref[...], tp, True) + carry_ref[...]
        carry_ref[...] = f_cum[tp - 1:tp, :]
        ftot_ref[...] = f_cum[tp - 1:tp, :]
        ck = ck_ref[...].astype(F32)
        cv = cv_ref[...].astype(F32)
        lane = lax.broadcasted_iota(I32, (tp, AUG_W), 1)
        for h in range(N_HEADS):
            _, ka, va = _augment(None, _head_slab(ck, 0, h), _head_slab(cv, 0, h), f_cum[:, h:h + 1] * LOG2E, lane)
            ka_ref[h] = ka
            va_ref[h] = va

    @pl.when(t == pl.num_programs(1) - 1)
    def _():
        ka_ref[...] = jnp.zeros_like(ka_ref)
        va_ref[...] = jnp.zeros_like(va_ref)


def _cache_prep(ck, cv, clf, n_keys):
    B, P, _ = ck.shape
    tp = min(SEQ_TILE, P)
    nt = P // tp
    row = lambda b, t: (b, jnp.minimum(t, nt - 1), 0)
    aug = pl.BlockSpec((None, N_HEADS, tp, AUG_W), lambda b, t: (b, 0, t, 0))
    aug_shape = jax.ShapeDtypeStruct((B, N_HEADS, n_keys, AUG_W), BF16)
    assert P < n_keys <= P + tp
    return pl.pallas_call(
        functools.partial(_cache_kernel, tp=tp),
        grid=(B, nt + 1),
        in_specs=[pl.BlockSpec((None, tp, ATT_W), row), pl.BlockSpec((None, tp, ATT_W), row),
                  pl.BlockSpec((None, tp, LANES), row)],
        out_specs=[aug, aug, pl.BlockSpec((None, 1, LANES), lambda b, t: (b, 0, 0))],
        out_shape=[aug_shape, aug_shape, jax.ShapeDtypeStruct((B, 1, LANES), F32)],
        scratch_shapes=[pltpu.VMEM((1, LANES), F32)],
        compiler_params=pltpu.CompilerParams(
            dimension_semantics=("arbitrary", "arbitrary"), vmem_limit_bytes=VMEM_LIMIT,
            allow_input_fusion=[True, True, True]),
    )(ck, cv, clf)


def _attn_kernel(jmin_ref, qa_ref, ka_ref, va_ref, o_ref, *, tq, tk, q_off):
    b, hp, iq = pl.program_id(0), pl.program_id(1), pl.program_id(2)
    q_lo = q_off + iq * tq
    n_full = (q_lo + 1) // tk
    j_first = jmin_ref[(b * pl.num_programs(1) + hp) * pl.num_programs(2) + iq]
    lane = lax.broadcasted_iota(I32, (tq, AUG_W), 1)
    qs = (qa_ref[0], qa_ref[1])

    def visit(carry, k0, nk, r0, masked):
        new = []
        for hh in range(2):
            m_all, acc_all = carry[hh]
            m, acc, q = m_all[r0:], acc_all[r0:], qs[hh][r0:]
            k = ka_ref[hh, pl.ds(k0, nk), :]
            v = va_ref[hh, pl.ds(k0, nk), :]
            s = lax.dot_general(q, k, (((1,), (1,)), ((), ())), preferred_element_type=F32)
            if masked:
                qpos = q_lo + r0 + lax.broadcasted_iota(I32, (tq - r0, nk), 0)
                kpos = k0 + lax.broadcasted_iota(I32, (tq - r0, nk), 1)
                s = jnp.where(qpos >= kpos, s, -jnp.inf)
            m_new = jnp.maximum(m, jnp.max(s, axis=-1, keepdims=True))
            alpha = jnp.exp2(m - m_new)
            p = jnp.exp2(s - m_new)
            acc = acc * alpha + jnp.dot(p.astype(BF16), v, preferred_element_type=F32)
            if r0:
                m_new = jnp.concatenate([m_all[:r0], m_new], axis=0)
                acc = jnp.concatenate([acc_all[:r0], acc], axis=0)
            new.append((m_new, acc))
        return tuple(new)

    def step(j, carry):
        return visit(carry, pl.multiple_of(j * tk, tk), tk, 0, False)

    init = (jnp.full((tq, 1), -jnp.inf, F32), jnp.zeros((tq, AUG_W), F32))
    carry = lax.fori_loop(j_first, n_full, step, (init, init))
    k_diag = pl.multiple_of(n_full * tk, tk)
    if tq == tk and tq % (2 * LANES) == 0:
        half = tq // 2
        carry = visit(carry, k_diag, half, 0, True)
        carry = visit(carry, pl.multiple_of(k_diag + half, half), half, half, True)
    else:
        carry = visit(carry, k_diag, tk, 0, True)
    outs = [acc / acc[:, HEAD_DIM:HEAD_DIM + 1] for _, acc in carry]
    o_ref[...] = jnp.where(lane < HEAD_DIM, outs[0], pltpu.roll(outs[1], HEAD_DIM, 1)).astype(BF16)


def _attention(jmin, qa, ka, va, q_off, tq, tk):
    B, H, Tq, _ = qa.shape
    Tk = ka.shape[2]
    assert tk % tq == 0 and q_off % tq == 0 and tq > 1
    return pl.pallas_call(
        functools.partial(_attn_kernel, tq=tq, tk=tk, q_off=q_off),
        grid_spec=pltpu.PrefetchScalarGridSpec(
            num_scalar_prefetch=1,
            grid=(B, H // 2, Tq // tq),
            in_specs=[
                pl.BlockSpec((None, 2, tq, AUG_W), lambda b, hp, iq, jm: (b, hp, iq, 0)),
                pl.BlockSpec((None, 2, Tk, AUG_W), lambda b, hp, iq, jm: (b, hp, 0, 0)),
                pl.BlockSpec((None, 2, Tk, AUG_W), lambda b, hp, iq, jm: (b, hp, 0, 0)),
            ],
            out_specs=pl.BlockSpec((None, tq, 2 * HEAD_DIM), lambda b, hp, iq, jm: (b, iq, hp)),
        ),
        out_shape=jax.ShapeDtypeStruct((B, Tq, ATT_W), BF16),
        compiler_params=pltpu.CompilerParams(
            dimension_semantics=("arbitrary", "arbitrary", "arbitrary"), vmem_limit_bytes=VMEM_LIMIT),
    )(jmin, qa, ka, va)


def _attn_skip_plan(stats, tq, tk, tm):
    B, nt = stats.shape[:2]
    st = stats[..., :N_HEADS]
    per = lambda row, r: st[:, :, row].reshape(B, nt // r, r, N_HEADS)
    rq, rk = tq // tm, tk // tm
    nq, nk = nt // rq, nt // rk
    qn2, kn2, kn2_own = per(0, rq).max(2), per(1, rk).max(2), per(1, rq).max(2)
    f_first, f_last = per(2, rq)[:, :, 0], per(3, rk)[:, :, -1]
    slack = 1.01
    upper = jnp.sqrt(qn2[:, :, None] * kn2[:, None, :]) * slack + (f_first[:, :, None] - f_last[:, None, :])
    lower = -jnp.sqrt(qn2 * kn2_own) * slack
    weightless = upper - lower[:, :, None] <= -(SKIP_LOG2 + 2.0)
    j = jnp.arange(nk, dtype=I32)[None, None, :, None]
    n_full = ((jnp.arange(nq, dtype=I32) * tq + 1) // tk)[None, :, None, None]
    first = jnp.min(jnp.where(weightless | (j >= n_full), n_full, j), axis=2)
    first = jnp.min(first.reshape(B, nq, N_HEADS // 2, 2), axis=3)
    return jnp.swapaxes(first, 1, 2).reshape(-1).astype(I32)


GATE_CHUNKS = 8


def _gate_chunk(xn_bf, wg_ref, bg_ref, c):
    cols = slice(c * (2 * D_MODEL // GATE_CHUNKS), (c + 1) * (2 * D_MODEL // GATE_CHUNKS))
    return jax.nn.sigmoid(jnp.dot(xn_bf, wg_ref[:, cols], preferred_element_type=F32) + bg_ref[:, cols]), cols


def _mixer_kernel(x_ref, xnext_ref, u_ref, uprev_ref, hist_ref, ya_ref, cnt0_ref,
                  lng_ref, lnb_ref, wg_ref, bg_ref, wpool_ref, spool_ref, wpp_ref, watt_ref, wout_ref,
                  ln1g_ref, ln1b_ref, wr_ref, br_ref, *rest, tm, start_pos):
    h_ref, hp_ref, tw_ref, code_ref, cnt_ref, uext_ref, gates_ref, xn_ref = rest[-8:]
    b = pl.program_id(0)
    t = pl.program_id(1)

    @pl.when((b == 0) & (t == 0))
    def _():
        cnt_ref[...] = cnt0_ref[...]
        xn0 = _layer_norm(x_ref[...], lng_ref[...], lnb_ref[...])
        xn_ref[...] = xn0
        for c in range(GATE_CHUNKS):
            g, cols = _gate_chunk(xn0.astype(BF16), wg_ref, bg_ref, c)
            gates_ref[:, cols] = g

    @pl.when(t == 0)
    def _():
        uext_ref[0:HALO, :] = hist_ref[...]

    @pl.when(t > 0)
    def _():
        uext_ref[0:HALO, :] = uprev_ref[...]

    uext_ref[HALO:HALO + tm, :] = u_ref[...]

    pos = start_pos + t * tm + lax.broadcasted_iota(I32, (tm, 1), 0)
    groups = []
    for gi, w in enumerate(POOL_WINDOWS):
        sl = slice(gi * POOL_GC, (gi + 1) * POOL_GC)
        cur = uext_ref[HALO:HALO + tm, sl]
        s = cur
        for j in range(1, w):
            s = s + uext_ref[HALO - j:HALO - j + tm, sl]
        count = jnp.minimum(pos + 1, w).astype(F32)
        d = s / count - cur
        yg = jnp.dot(d.astype(BF16), wpool_ref[gi], preferred_element_type=F32) * spool_ref[:, sl]
        groups.append(yg.astype(BF16))
    pooled = jnp.dot(jnp.concatenate(groups, axis=1), wpp_ref[...], preferred_element_type=F32)
    att = jnp.dot(ya_ref[...], watt_ref[...], preferred_element_type=F32)
    m = gates_ref[:, :D_MODEL] * pooled + gates_ref[:, D_MODEL:] * att
    mix = jnp.dot(m.astype(BF16), wout_ref[...], preferred_element_type=F32)

    xn_next_f32 = _layer_norm(xnext_ref[...], lng_ref[...], lnb_ref[...])
    xn_next = xn_next_f32.astype(BF16)

    def emit_next_gates(c):
        g, cols = _gate_chunk(xn_next, wg_ref, bg_ref, c)
        gates_ref[:, cols] = g

    emit_next_gates(0)

    h = _layer_norm(DEEPNORM_ALPHA * xn_ref[...] + mix, ln1g_ref[...], ln1b_ref[...])
    xn_ref[...] = xn_next_f32
    h_ref[...] = h
    packed = _pack_halves(h)
    hp_ref[0] = packed[:, :PLANE]
    hp_ref[1] = packed[:, PLANE:]
    emit_next_gates(1)
    emit_next_gates(2)

    hh = h.astype(BF16)
    hl = (h - hh.astype(F32)).astype(BF16)
    hi_pass = jnp.dot(hh, wr_ref[...], preferred_element_type=F32)
    lo_pass = jnp.dot(hl, wr_ref[:, :LANES], preferred_element_type=F32)
    logits = hi_pass[:, :LANES] + lo_pass + hi_pass[:, LANES:] + br_ref[...]
    emit_next_gates(3)
    lane = lax.broadcasted_iota(I32, (tm, LANES), 1)
    work = jnp.where(lane < N_EXPERTS, logits, -jnp.inf)
    vals, idxs = [], []
    for r in range(TOP_K):
        mx = jnp.max(work, axis=-1, keepdims=True)
        idx = jnp.min(jnp.where(work == mx, lane, LANES), axis=-1, keepdims=True)
        vals.append(mx)
        idxs.append(idx)
        work = jnp.where(lane == idx, -jnp.inf, work)
        emit_next_gates(GATE_CHUNKS - TOP_K + r)
    exps = [jnp.exp(v - vals[0]) for v in vals]
    denom = exps[0] + exps[1] + exps[2] + exps[3]

    onehot = jnp.zeros((tm, LANES), F32)
    for idx in idxs:
        onehot = onehot + (lane == idx).astype(F32)
    base = _prefix_rows(onehot, tm, False, small_ints=True) + cnt_ref[...]
    code = jnp.zeros((tm, LANES), I32)
    tw = jnp.zeros((tm, LANES), F32)
    for k in range(TOP_K):
        rank = jnp.sum(jnp.where(lane == idxs[k], base, 0.0), axis=-1, keepdims=True)
        code = jnp.where(lane == k, idxs[k] * RANK_SPAN + rank.astype(I32), code)
        tw = jnp.where(lane == k, exps[k] / denom, tw)
    tw_ref[...] = tw
    if code_ref.shape[0] == CODE_ROWS:
        code_ref[...] = jnp.transpose(code)[:CODE_ROWS, :]
    else:
        code_ref[...] = code
    cnt_ref[...] = cnt_ref[...] + jnp.sum(onehot, axis=0, keepdims=True)


def _mixer(x, u, hist, yatt, cnt0, start_pos, weights, n_tok, row_off, prev):
    B, T, D = x.shape
    tm = min(SEQ_TILE, T)
    nt = T // tm
    row = lambda b, t: (b, t, 0)
    tok = lambda b, t: (row_off // tm + b * nt + t, 0)
    hpt = tm // HALO
    full = lambda a: pl.BlockSpec(a.shape, lambda b, t, _n=a.ndim: (0,) * _n)
    tok3 = lambda b, t: (0, row_off // tm + b * nt + t, 0)
    assert n_tok <= RANK_SPAN
    if tm % LANES == 0:
        code_spec = pl.BlockSpec((CODE_ROWS, tm), lambda b, t: (0, b * nt + t))
        code_shape = jax.ShapeDtypeStruct((CODE_ROWS, B * T), I32)
    else:
        code_spec = pl.BlockSpec((tm, LANES), lambda b, t: (b * nt + t, 0))
        code_shape = jax.ShapeDtypeStruct((B * T, LANES), I32)
    out_specs = [pl.BlockSpec((tm, D), tok), pl.BlockSpec((2, tm, PLANE), tok3), pl.BlockSpec((tm, LANES), tok),
                 code_spec, pl.BlockSpec((1, LANES), lambda b, t: (0, 0))]
    out_shape = [jax.ShapeDtypeStruct((n_tok, D), F32), jax.ShapeDtypeStruct((2, n_tok, PLANE), U32),
                 jax.ShapeDtypeStruct((n_tok, LANES), F32), code_shape, jax.ShapeDtypeStruct((1, LANES), F32)]
    n_in = 7 + len(weights)

    def next_row(b, t):
        lin = jnp.minimum(b * nt + t + 1, B * nt - 1)
        return (lin // nt, lin % nt, 0)

    return pl.pallas_call(
        functools.partial(_mixer_kernel, tm=tm, start_pos=start_pos),
        grid=(B, nt),
        in_specs=[
            pl.BlockSpec((None, tm, D), row),
            pl.BlockSpec((None, tm, D), next_row),
            pl.BlockSpec((None, tm, POOL_W), row),
            pl.BlockSpec((None, HALO, POOL_W), lambda b, t: (b, jnp.maximum(t * hpt - 1, 0), 0)),
            pl.BlockSpec((None, HALO, POOL_W), lambda b, t: (b, 0, 0)),
            pl.BlockSpec((None, tm, ATT_W), row),
            pl.BlockSpec((1, LANES), lambda b, t: (0, 0)),
        ] + [full(w) for w in weights] + [pl.BlockSpec(memory_space=pl.ANY) for _ in prev],
        out_specs=out_specs,
        out_shape=out_shape,
        scratch_shapes=[pltpu.VMEM((HALO + tm, POOL_W), F32), pltpu.VMEM((tm, 2 * D), F32),
                        pltpu.VMEM((tm, D), F32)],
        input_output_aliases={n_in + i: i for i in range(len(prev))},
        compiler_params=pltpu.CompilerParams(
            dimension_semantics=("arbitrary", "arbitrary"), vmem_limit_bytes=VMEM_LIMIT),
    )(x, x, u, u, hist, yatt, cnt0, *weights, *prev)


def _sc_mesh():
    return plsc.VectorSubcoreMesh(core_axis_name="c", subcore_axis_name="s")


def _sc_scatter_rows(x, idx, n_rows):
    n = x.shape[0]

    @pl.kernel(out_type=jax.ShapeDtypeStruct((n_rows, PLANE), x.dtype), mesh=_sc_mesh(), scratch_types=[])
    def scatter(x_hbm, i_hbm, o_hbm):
        def body(x_vmem, i_vmem):
            for k in range(TOP_K):
                pltpu.sync_copy(x_vmem, o_hbm.at[i_vmem.at[k]])

        pltpu.emit_pipeline(
            body,
            grid=(n // SC_WINDOW,),
            in_specs=[pl.BlockSpec((SC_WINDOW, PLANE), index_map=lambda i: (i, 0)),
                      pl.BlockSpec((TOP_K, SC_WINDOW), index_map=lambda i: (0, i))],
            out_specs=[],
            core_axis_name=("c", "s"),
            dimension_semantics=(pltpu.PARALLEL,),
        )(x_hbm, i_hbm)

    return scatter(x, idx)


def _sc_gather_rows(y, idx):
    n = idx.shape[1]

    @pl.kernel(out_type=jax.ShapeDtypeStruct((n, PLANE), y.dtype), mesh=_sc_mesh(), scratch_types=[])
    def gather(y_hbm, i_hbm, o_hbm):
        def body(i_vmem, o_vmem):
            pltpu.sync_copy(y_hbm.at[i_vmem.at[0]], o_vmem)

        pltpu.emit_pipeline(
            body,
            grid=(n // SC_WINDOW,),
            in_specs=[pl.BlockSpec((1, SC_WINDOW), index_map=lambda i: (0, i))],
            out_specs=[pl.BlockSpec((SC_WINDOW, PLANE), index_map=lambda i: (i, 0))],
            core_axis_name=("c", "s"),
            dimension_semantics=(pltpu.PARALLEL,),
        )(i_hbm, o_hbm)

    return gather(y, idx)


def _join_planes(ref):
    return jnp.concatenate([ref[0], ref[1]], axis=1)


def _unpack_rows(xb_ref):
    xa, xc = _unpack_halves(_join_planes(xb_ref))
    return jnp.concatenate([xa.astype(BF16), xc.astype(BF16)], axis=1)


def _expert_kernel(be_ref, nb_ref, first_ref, xb_ref, xbnext_ref, w1_ref, b1_ref, w2_ref, b2_ref,
                   yb_ref, w1b_ref, w2b_ref, x_ref):
    del be_ref
    i = pl.program_id(0)

    @pl.when(i == 0)
    def _():
        x_ref[...] = _unpack_rows(xb_ref)

    @pl.when(first_ref[i] == 1)
    def _():
        w1b_ref[...] = w1_ref[...].astype(BF16)
        w2b_ref[...] = w2_ref[...].astype(BF16)

    @pl.when(i < nb_ref[0])
    def _():
        hfull = jnp.dot(x_ref[...], w1b_ref[...], preferred_element_type=F32) + b1_ref[...]
        glu = jnp.minimum(hfull[:, :D_FF], SWIGLU_LIMIT)
        lin = jnp.clip(hfull[:, D_FF:], -SWIGLU_LIMIT, SWIGLU_LIMIT)
        a = glu * jax.nn.sigmoid(SWIGLU_ALPHA * glu) * (lin + 1.0)
        x_next = _unpack_rows(xbnext_ref)
        y = jnp.dot(a.astype(BF16), w2b_ref[...], preferred_element_type=F32) + b2_ref[...]
        packed = _pack_halves(y)
        yb_ref[0] = packed[:, :PLANE]
        yb_ref[1] = packed[:, PLANE:]
        x_ref[...] = x_next

    @pl.when(i >= nb_ref[0])
    def _():
        yb_ref[...] = jnp.zeros_like(yb_ref)


def _experts(block_e, n_used, first, xb, w1, b1, w2, b2):
    n_rows = xb.shape[1]
    bm = MOE_BLOCK
    n_blocks = n_rows // bm
    rows = pl.BlockSpec((2, bm, PLANE), lambda i, be, nb, fi: (0, i, 0))
    next_rows = pl.BlockSpec((2, bm, PLANE), lambda i, be, nb, fi: (0, jnp.minimum(i + 1, n_blocks - 1), 0))
    per_expert = lambda r, c: pl.BlockSpec((None, r, c), lambda i, be, nb, fi: (be[i], 0, 0))
    return pl.pallas_call(
        _expert_kernel,
        grid_spec=pltpu.PrefetchScalarGridSpec(
            num_scalar_prefetch=3,
            grid=(n_blocks,),
            in_specs=[rows, next_rows, per_expert(D_MODEL, 2 * D_FF), per_expert(1, 2 * D_FF),
                      per_expert(D_FF, D_MODEL), per_expert(1, D_MODEL)],
            out_specs=rows,
            scratch_shapes=[pltpu.VMEM((D_MODEL, 2 * D_FF), BF16), pltpu.VMEM((D_FF, D_MODEL), BF16),
                            pltpu.VMEM((bm, D_MODEL), BF16)],
        ),
        out_shape=jax.ShapeDtypeStruct((2, n_rows, PLANE), U32),
        compiler_params=pltpu.CompilerParams(
            dimension_semantics=("arbitrary",), vmem_limit_bytes=VMEM_LIMIT),
    )(block_e, n_used, first, xb, xb, w1, b1, w2, b2)


def _combine_kernel(h_ref, tw_ref, g_ref, b_ref, y4_ref, out_ref):
    tw = tw_ref[...]
    acc_hi = None
    acc_lo = None
    for k in range(TOP_K):
        hi, lo = _unpack_halves(_join_planes(y4_ref.at[k]))
        w = tw[:, k:k + 1]
        acc_hi = w * hi if acc_hi is None else acc_hi + w * hi
        acc_lo = w * lo if acc_lo is None else acc_lo + w * lo
    moe = jnp.concatenate([acc_hi, acc_lo], axis=1)
    out_ref[...] = _layer_norm(DEEPNORM_ALPHA * h_ref[...] + moe, g_ref[...], b_ref[...])


def _combine(h, tw, ln2g, ln2b, y4, row_off, n_rows):
    D = h.shape[1]
    tn = min(COMBINE_TILE, n_rows)
    off = row_off // tn
    const = lambda i: (0, 0)
    return pl.pallas_call(
        _combine_kernel,
        grid=(n_rows // tn,),
        in_specs=[
            pl.BlockSpec((tn, D), lambda i: (off + i, 0)),
            pl.BlockSpec((tn, LANES), lambda i: (off + i, 0)),
            pl.BlockSpec((1, D), const), pl.BlockSpec((1, D), const),
            pl.BlockSpec((TOP_K, 2, tn, PLANE), lambda i: (0, 0, off + i, 0)),
        ],
        out_specs=pl.BlockSpec((tn, D), lambda i: (i, 0)),
        out_shape=jax.ShapeDtypeStruct((n_rows, D), F32),
        compiler_params=pltpu.CompilerParams(
            dimension_semantics=("arbitrary",), vmem_limit_bytes=VMEM_LIMIT),
    )(h, tw, ln2g, ln2b, y4)


def kernel(x_prompt, x_sample, cache_pool, cache_k, cache_v, cache_logf, ln_in_g, ln_in_b, w_in, b_in,
           w_pool, s_pool, w_pool_proj, w_att_proj, w_out, ln1_g, ln1_b, w_router, b_router,
           w1, b1, w2, b2, ln2_g, ln2_b):
    assert w_in.shape[0] == DEPTH
    B, T, D = x_prompt.shape
    Bs, Ts, _ = x_sample.shape
    P = cache_k.shape[2]
    row2 = lambda a: a.reshape(1, -1).astype(F32)

    f_off = MAIN_W
    g_off = MAIN_W + N_HEADS
    w_proj = jnp.pad(w_in[0][:, :g_off], ((0, 0), (0, LANES - N_HEADS))).astype(BF16)
    b_proj = row2(jnp.pad(b_in[0][:g_off], (0, LANES - N_HEADS)))
    wg = w_in[0][:, g_off:].astype(BF16)
    bg = row2(b_in[0][g_off:])
    lng, lnb = row2(ln_in_g), row2(ln_in_b)
    wr = jnp.pad(w_router[0], ((0, 0), (0, LANES - N_EXPERTS)))
    wr_hi = wr.astype(BF16)
    wr_lo = (wr - wr_hi.astype(F32)).astype(BF16)
    br = row2(jnp.pad(b_router[0], (0, LANES - N_EXPERTS)))
    mixer_weights = (lng, lnb, wg, bg, w_pool[0].astype(BF16), row2(s_pool[0]),
                     w_pool_proj[0].astype(BF16), w_att_proj[0].astype(BF16), w_out[0].astype(BF16),
                     row2(ln1_g[0]), row2(ln1_b[0]), jnp.concatenate([wr_hi, wr_lo], axis=1), br)

    zeros_f = jnp.zeros((B, 1, LANES), F32)
    u_p, k_p, v_p, logf_p, qa_p, ka_p, va_p, stats = _inproj(x_prompt, zeros_f, lng, lnb, w_proj, b_proj)
    tq, tk = min(ATT_TQ, T), min(ATT_TK, T)
    ya_p = _attention(_attn_skip_plan(stats, tq, tk, min(SEQ_TILE, T)), qa_p, ka_p, va_p, 0, tq, tk)

    clf = jnp.pad(cache_logf[0], ((0, 0), (0, 0), (0, LANES - N_HEADS)))
    tks = P + Ts + (-(P + Ts)) % LANES
    ka_c, va_c, f_tot = _cache_prep(cache_k[0].reshape(Bs, P, ATT_W), cache_v[0].reshape(Bs, P, ATT_W), clf, tks)
    u_s, k_s, v_s, logf_s, qa_s, ka_s, va_s, _ = _inproj(x_sample, f_tot, lng, lnb, w_proj, b_proj, (ka_c, va_c), P)
    visit_all = jnp.zeros((Bs * (N_HEADS // 2),), I32)
    ya_s = _attention(visit_all, qa_s, ka_s, va_s, P, Ts, tks)

    cnt0 = jnp.zeros((1, LANES), F32)
    hist_p = jnp.zeros((B, HALO, POOL_W), F32)
    n_tok = B * T + Bs * Ts
    *bufs, code_p, cnt_p = _mixer(x_prompt, u_p, hist_p, ya_p, cnt0, 0, mixer_weights, n_tok, 0, ())
    u_full_s = jnp.concatenate([cache_pool[0].astype(F32), u_s], axis=1)
    hist_s = jnp.pad(cache_pool[0].astype(F32), ((0, 0), (HALO - POOL_HIST, 0), (0, 0)))
    h_all, hp_all, tw, code_s, cnt = _mixer(x_sample, u_s, hist_s, ya_s, cnt_p, P, mixer_weights,
                                            n_tok, B * T, tuple(bufs))
    by_slot = lambda c: c[:TOP_K] if c.shape[0] == CODE_ROWS else c[:, :TOP_K].T
    code = jnp.concatenate([by_slot(code_p), by_slot(code_s)], axis=1)
    ti, rk = code // RANK_SPAN, code % RANK_SPAN

    counts = cnt[0, :N_EXPERTS].astype(I32)
    padded = (counts + MOE_BLOCK - 1) // MOE_BLOCK * MOE_BLOCK
    pad_ends = jnp.cumsum(padded)
    pad_starts = pad_ends - padded
    n_blocks = (n_tok * TOP_K + N_EXPERTS * (MOE_BLOCK - 1) + MOE_BLOCK - 1) // MOE_BLOCK
    n_rows = n_blocks * MOE_BLOCK
    block_start = jnp.arange(n_blocks, dtype=I32) * MOE_BLOCK
    block_e = jnp.minimum(jnp.sum((pad_ends[None, :] <= block_start[:, None]).astype(I32), axis=1), N_EXPERTS - 1)
    n_used = (pad_ends[-1:] // MOE_BLOCK).astype(I32)
    start_of = sum(jnp.where(ti == e, pad_starts[e], 0) for e in range(N_EXPERTS))
    dest_t = (start_of + rk).astype(I32)
    dest_planes = jnp.stack([dest_t, dest_t + n_rows], axis=1)

    xb = _sc_scatter_rows(hp_all.reshape(2 * n_tok, PLANE), dest_planes.reshape(TOP_K, 2 * n_tok), 2 * n_rows)
    first = jnp.concatenate([jnp.ones((1,), I32), (block_e[1:] != block_e[:-1]).astype(I32)])
    yb = _experts(block_e, n_used, first, xb.reshape(2, n_rows, PLANE),
                  w1[0], b1[0][:, None, :], w2[0], b2[0][:, None, :])
    y4 = _sc_gather_rows(yb.reshape(2 * n_rows, PLANE), dest_planes.reshape(1, TOP_K * 2 * n_tok))
    y4 = y4.reshape(TOP_K, 2, n_tok, PLANE)
    ln2g, ln2b = row2(ln2_g[0]), row2(ln2_b[0])
    y_prompt = _combine(h_all, tw, ln2g, ln2b, y4, 0, B * T).reshape(B, T, D)
    y_sample = _combine(h_all, tw, ln2g, ln2b, y4, B * T, Bs * Ts).reshape(Bs, Ts, D)
    heads = lambda a, b_, t_: a.reshape(1, b_, t_, N_HEADS, HEAD_DIM)
    return (y_prompt, y_sample,
            heads(k_p, B, T), heads(v_p, B, T), logf_p[None], u_p[:, -POOL_HIST:][None],
            heads(k_s, Bs, Ts), heads(v_s, Bs, Ts), logf_s[None], u_full_s[:, -POOL_HIST:][None])
```
